```python
import jax, jax.numpy as jnp
from jax import lax
import numpy as np

D_MODEL = 2048
BATCH = 2
SEQ = 4096
DEPTH = 2
DEC_BATCH = 8
DEC_SEQ = 1
PAST_LEN = 16384
PAGE_SIZE = 128

N_META = 16
CHUNK = 128
RMS_EPS = 1e-6
GN_EPS = 1e-5
RET_HEADS = 4
RET_DK = 256
RET_DV = 256
RET_W = RET_HEADS * RET_DK
ROPE_BASE = 10000.0
FOX_HEADS = 8
FOX_HD = 128
FOX_W = FOX_HEADS * FOX_HD
RWKV_HEADS = 16
RWKV_HD = 64
RWKV_W = RWKV_HEADS * RWKV_HD
RWKV_W_RANK = 64
RWKV_A_RANK = 64
RWKV_G_RANK = 128
RWKV_GN_EPS = 64e-5
RWKV_SPLIT = (RWKV_W, RWKV_W, RWKV_W, RWKV_W_RANK, RWKV_A_RANK, RWKV_G_RANK)
RWKV_PROJ = 3 * RWKV_W + RWKV_W_RANK + RWKV_A_RANK + RWKV_G_RANK
N_BRANCH = 3
PROJ_SPLIT = (RET_W, RET_W, RET_W, RET_W, FOX_W, FOX_W, FOX_W, FOX_HEADS, RWKV_PROJ, N_BRANCH * D_MODEL)
PROJ_DIM = sum(PROJ_SPLIT)
N_GROUPS = 4
EXPERTS_PER_GROUP = 8
N_EXPERTS = N_GROUPS * EXPERTS_PER_GROUP
TOP_K = 2
EXPERT_FF = 1024
MOE_BLOCK = 128
MOE_BLOCK_SMALL = 8

kernel_name = "hybrid_retention_fox_rwkv7_hmoe_step"


def _offsets(sizes):
    return [int(s) for s in np.cumsum(sizes)[:-1]]


def rmsnorm(x, g):
    xf = x.astype(jnp.float32)
    y = xf * lax.rsqrt(jnp.mean(xf * xf, axis=-1, keepdims=True) + RMS_EPS)
    return (y * g.astype(jnp.float32)).astype(x.dtype)


def head_norm(y, eps):
    mu = jnp.mean(y, axis=-1, keepdims=True)
    var = jnp.mean(jnp.square(y - mu), axis=-1, keepdims=True)
    return (y - mu) * lax.rsqrt(var + eps)


def rope(x, pos):
    half = x.shape[-1] // 2
    inv = ROPE_BASE ** (-jnp.arange(half, dtype=jnp.float32) / half)
    ang = pos.astype(jnp.float32)[:, None] * inv[None, :]
    cos = jnp.cos(ang)[None, :, None, :]
    sin = jnp.sin(ang)[None, :, None, :]
    x1, x2 = x[..., :half], x[..., half:]
    return jnp.concatenate([x1 * cos - x2 * sin, x1 * sin + x2 * cos], axis=-1)


def ret_chunk(S, q, k, v, lg):
    L = q.shape[1]
    i = jnp.arange(L, dtype=jnp.float32)
    diff = i[:, None] - i[None, :]
    dmask = jnp.where(diff >= 0, jnp.exp(lg[:, None, None] * jnp.maximum(diff, 0.0)), 0.0)
    scores = jnp.einsum('bihd,bjhd->bhij', q, k) * dmask[None]
    intra = jnp.einsum('bhij,bjhv->bihv', scores, v)
    cross = jnp.einsum('bihd,bhdv->bihv', q, S) * jnp.exp(lg[None, None, :, None] * (i + 1.0)[None, :, None, None])
    kdec = k * jnp.exp(lg[None, None, :, None] * (L - 1.0 - i)[None, :, None, None])
    S_new = jnp.exp(lg * L)[None, :, None, None] * S + jnp.einsum('bjhd,bjhv->bhdv', kdec, v)
    return intra + cross, S_new


def retention_branch(rq, rk, rv, rg, pos, S0, prompt):
    B, T, _ = rq.shape
    f32 = jnp.float32
    q = rope(rq.astype(f32).reshape(B, T, RET_HEADS, RET_DK), pos)
    k = rope(rk.astype(f32).reshape(B, T, RET_HEADS, RET_DK), pos) * (RET_DK ** -0.5)
    v = rv.astype(f32).reshape(B, T, RET_HEADS, RET_DV)
    lg = jnp.log1p(-jnp.power(2.0, -5.0 - jnp.arange(RET_HEADS, dtype=f32)))
    S0 = S0.astype(f32)
    if prompt:
        o_meta, S = ret_chunk(S0, q[:, :N_META], k[:, :N_META], v[:, :N_META], lg)
        n_c = (T - N_META) // CHUNK
        to_chunks = lambda a: a[:, N_META:].reshape(B, n_c, CHUNK, *a.shape[2:]).swapaxes(0, 1)

        def body(S_c, qkv):
            o_c, S_c = ret_chunk(S_c, qkv[0], qkv[1], qkv[2], lg)
            return S_c, o_c

        S, o_rest = lax.scan(body, S, (to_chunks(q), to_chunks(k), to_chunks(v)))
        o_rest = o_rest.swapaxes(0, 1).reshape(B, T - N_META, RET_HEADS, RET_DV)
        o = jnp.concatenate([o_meta, o_rest], axis=1)
    else:
        o, S = ret_chunk(S0, q, k, v, lg)
    o = head_norm(o, GN_EPS).reshape(B, T, RET_W) * jax.nn.silu(rg.astype(f32))
    return o.astype(rq.dtype), S


def fox_branch(fq, fk, fv, ff, b_f, pos, past, prompt):
    B, T, _ = fq.shape
    f32 = jnp.float32
    q = fq.reshape(B, T, FOX_HEADS, FOX_HD)
    k = fk.reshape(B, T, FOX_HEADS, FOX_HD)
    v = fv.reshape(B, T, FOX_HEADS, FOX_HD)
    logf = jax.nn.log_sigmoid(ff.astype(f32) + b_f.astype(f32))
    scale = FOX_HD ** -0.5
    if prompt:
        c = jnp.cumsum(logf, axis=1).transpose(0, 2, 1)

        def attend(args):
            qb, cb, pb = args
            s = jnp.einsum('bqhd,bkhd->bhqk', qb, k, preferred_element_type=f32) * scale
            s = s + (cb[..., None] - c[:, :, None, :])
            s = jnp.where(pb[:, None] >= pos[None, :], s, -jnp.inf)
            p = jax.nn.softmax(s, axis=-1)
            return jnp.einsum('bhqk,bkhd->bqhd', p.astype(v.dtype), v)

        o_meta = attend((q[:, :N_META], c[:, :, :N_META], pos[:N_META]))
        n_b = (T - N_META) // CHUNK
        qb = q[:, N_META:].reshape(B, n_b, CHUNK, FOX_HEADS, FOX_HD).swapaxes(0, 1)
        cb = c[:, :, N_META:].reshape(B, FOX_HEADS, n_b, CHUNK).transpose(2, 0, 1, 3)
        pb = pos[N_META:].reshape(n_b, CHUNK)
        o_rest = lax.map(attend, (qb, cb, pb)).swapaxes(0, 1).reshape(B, T - N_META, FOX_HEADS, FOX_HD)
        o = jnp.concatenate([o_meta, o_rest], axis=1)
    else:
        pk, pv, plogf = past
        P = pk.shape[1]
        plogf = plogf.astype(f32)
        dsuf = (lax.cumsum(plogf, axis=1, reverse=True) - plogf).transpose(0, 2, 1)
        cnew = jnp.cumsum(logf, axis=1).transpose(0, 2, 1)
        s_past = jnp.einsum('bqhd,bkhd->bhqk', q, pk, preferred_element_type=f32) * scale
        s_past = s_past + dsuf[:, :, None, :] + cnew[..., None]
        s_new = jnp.einsum('bqhd,bkhd->bhqk', q, k, preferred_element_type=f32) * scale
        s_new = s_new + cnew[..., None] - cnew[:, :, None, :]
        s_new = jnp.where(pos[:, None] >= pos[None, :], s_new, -jnp.inf)
        p = jax.nn.softmax(jnp.concatenate([s_past, s_new], axis=-1), axis=-1).astype(v.dtype)
        o = jnp.einsum('bhqk,bkhd->bqhd', p[..., :P], pv.astype(v.dtype)) + jnp.einsum('bhqk,bkhd->bqhd', p[..., P:], v)
    return o.reshape(B, T, FOX_W).astype(fq.dtype), k, v, logf


def rwkv_branch(cols, lp, S0, shift0):
    B, T, _ = cols.shape
    f32 = jnp.float32
    c = cols.astype(f32)
    prev = jnp.concatenate([shift0.astype(f32)[:, None], c[:, :-1]], axis=1)
    xm = c + lp['mu'].astype(f32) * (prev - c)
    r, k, v, wd, ad, gd = jnp.split(xm, _offsets(RWKV_SPLIT), axis=-1)
    w_log = -jax.nn.softplus(-(lp['w0'].astype(f32) + jnp.tanh(wd) @ lp['w2'].astype(f32))) - 0.5
    decay = jnp.exp(-jnp.exp(w_log))
    a = jax.nn.sigmoid(lp['a0'].astype(f32) + ad @ lp['a2'].astype(f32))
    g = jax.nn.sigmoid(gd) @ lp['g2'].astype(f32)
    hs = lambda t: t.reshape(B, T, RWKV_HEADS, RWKV_HD)
    r, k, v, decay, a = hs(r), hs(k), hs(v), hs(decay), hs(a)
    kk = k * lp['kk'].astype(f32).reshape(RWKV_HEADS, RWKV_HD)
    kk = kk * lax.rsqrt(jnp.sum(kk * kk, axis=-1, keepdims=True) + 1e-12)
    k = k * (1.0 + (a - 1.0) * lp['ka'].astype(f32).reshape(RWKV_HEADS, RWKV_HD))

    def step(S, inp):
        r_t, w_t, k_t, v_t, kk_t, b_t = inp
        sa = jnp.einsum('bhvk,bhk->bhv', S, -kk_t)
        S = S * w_t[:, :, None, :] + sa[..., None] * b_t[:, :, None, :] + v_t[..., None] * k_t[:, :, None, :]
        return S, jnp.einsum('bhvk,bhk->bhv', S, r_t)

    tm = lambda t: jnp.swapaxes(t, 0, 1)
    S, y = lax.scan(step, S0.astype(f32), (tm(r), tm(decay), tm(k), tm(v), tm(kk), tm(kk * a)))
    y = tm(y)
    yn = head_norm(y, RWKV_GN_EPS) * lp['ln_w'].astype(f32).reshape(RWKV_HEADS, RWKV_HD) + lp['ln_b'].astype(f32).reshape(RWKV_HEADS, RWKV_HD)
    bonus = jnp.sum(r * k * lp['rk'].astype(f32), axis=-1, keepdims=True) * v
    out = (yn + bonus).reshape(B, T, RWKV_W) * g
    return out.astype(cols.dtype), S, cols[:, -1]


def mixer_layer(h, lp, pos, ret_S0, rwkv_S0, shift0, past, prompt):
    B, T, _ = h.shape
    p = h @ lp['w_in']
    rq, rk, rv, rg, fq, fk, fv, ff, rw_cols, gate_cols = jnp.split(p, _offsets(PROJ_SPLIT), axis=-1)
    ret_o, ret_S = retention_branch(rq, rk, rv, rg, pos, ret_S0, prompt)
    fox_o, k_rows, v_rows, logf_rows = fox_branch(fq, fk, fv, ff, lp['fox_b'], pos, past, prompt)
    rw_o, rw_S, shift_new = rwkv_branch(rw_cols, lp, rwkv_S0, shift0)
    gates = jax.nn.sigmoid(gate_cols.astype(jnp.float32)).reshape(B, T, N_BRANCH, D_MODEL).astype(h.dtype)
    merged = (gates[:, :, 0] * (ret_o @ lp['wb_ret'])
              + gates[:, :, 1] * (fox_o @ lp['wb_fox'])
              + gates[:, :, 2] * (rw_o @ lp['wb_rwkv']))
    return merged @ lp['w_out'], (k_rows, v_rows, logf_rows, ret_S, rw_S, shift_new)


def hier_moe(h, wg, bg, we, be, w1, w3, w2):
    N, D = h.shape
    f32 = jnp.float32
    gp = jax.nn.softmax(jnp.einsum('nd,dg->ng', h, wg, preferred_element_type=f32) + bg.astype(f32), axis=-1)
    gidx = jnp.argmax(gp, axis=-1)
    pg = jnp.take_along_axis(gp, gidx[:, None], axis=-1)
    el = jnp.einsum('nd,gde->nge', h, we, preferred_element_type=f32) + be.astype(f32)[None]
    el = jnp.take_along_axis(el, gidx[:, None, None], axis=1)[:, 0]
    topv, topi = lax.top_k(jax.nn.softmax(el, axis=-1), TOP_K)
    gate = pg * topv / jnp.sum(topv, axis=-1, keepdims=True)
    eid = (gidx[:, None] * EXPERTS_PER_GROUP + topi).reshape(-1).astype(jnp.int32)
    tok = jnp.repeat(jnp.arange(N, dtype=jnp.int32), TOP_K)
    wflat = gate.reshape(-1)
    A = N * TOP_K
    blk = MOE_BLOCK if A >= MOE_BLOCK * N_EXPERTS else MOE_BLOCK_SMALL
    n_blk = (A + N_EXPERTS * (blk - 1) + blk - 1) // blk
    cap = n_blk * blk
    order = jnp.argsort(eid)
    se = eid[order]
    counts = jnp.bincount(eid, length=N_EXPERTS)
    padded = (counts + blk - 1) // blk * blk
    pad_end = jnp.cumsum(padded)
    pad_start = pad_end - padded
    raw_start = jnp.cumsum(counts) - counts
    dest = pad_start[se] + jnp.arange(A, dtype=jnp.int32) - raw_start[se]
    tok_buf = jnp.full((cap,), N, jnp.int32).at[dest].set(tok[order])
    w_buf = jnp.zeros((cap,), f32).at[dest].set(wflat[order])
    blk_expert = jnp.minimum(jnp.searchsorted(pad_end, jnp.arange(n_blk, dtype=pad_end.dtype) * blk, side='right'), N_EXPERTS - 1)
    h_pad = jnp.concatenate([h, jnp.zeros((1, D), h.dtype)], axis=0)
    xb = h_pad[tok_buf].reshape(n_blk, blk, D)

    def expert_block(args):
        xblk, e = args
        return (jax.nn.silu(xblk @ w1[e]) * (xblk @ w3[e])) @ w2[e]

    yb = lax.map(expert_block, (xb, blk_expert)).reshape(cap, D)
    y = jnp.zeros((N + 1, D), f32).at[tok_buf].add(yb.astype(f32) * w_buf[:, None])
    return y[:N].astype(h.dtype)


def setup_inputs(seed: int = 0) -> dict:
    key = jax.random.key(seed)
    ks = iter(jax.random.split(key, 48))
    f32 = jnp.float32
    nrm = lambda shape, s: jax.random.normal(next(ks), shape, f32) * s
    uni = lambda shape, lo, hi: jax.random.uniform(next(ks), shape, f32, lo, hi)
    n_pages = PAST_LEN // PAGE_SIZE
    n_used = DEC_BATCH * n_pages
    n_pool = n_used + max(1, n_used // 4)
    D = D_MODEL
    inp = {}
    inp['x_prompt'] = nrm((BATCH, SEQ, D), 1.0)
    inp['x_sample'] = nrm((DEC_BATCH, DEC_SEQ, D), 1.0)
    inp['cache_k'] = nrm((DEPTH, n_pool, PAGE_SIZE, FOX_HEADS, FOX_HD), 1.0)
    inp['cache_v'] = nrm((DEPTH, n_pool, PAGE_SIZE, FOX_HEADS, FOX_HD), 1.0)
    inp['cache_logf'] = jax.nn.log_sigmoid(4.0 + nrm((DEPTH, n_pool, PAGE_SIZE, FOX_HEADS), 1.0))
    inp['page_table'] = jax.random.permutation(next(ks), n_pool)[:n_used].reshape(DEC_BATCH, n_pages).astype(jnp.int32)
    inp['state_ret'] = nrm((DEPTH, DEC_BATCH, RET_HEADS, RET_DK, RET_DV), 0.5)
    inp['state_rwkv'] = nrm((DEPTH, DEC_BATCH, RWKV_HEADS, RWKV_HD, RWKV_HD), 1.0)
    inp['state_shift'] = nrm((DEPTH, DEC_BATCH, RWKV_PROJ), 1.0)
    inp['meta_tokens'] = nrm((N_META, D), 1.0)
    inp['norm_mix'] = 1.0 + nrm((DEPTH, D), 0.1)
    inp['norm_ffn'] = 1.0 + nrm((DEPTH, D), 0.1)
    inp['norm_final'] = 1.0 + nrm((D,), 0.1)
    inp['w_in'] = nrm((DEPTH, D, PROJ_DIM), D ** -0.5)
    inp['fox_forget_bias'] = uni((DEPTH, FOX_HEADS), 2.0, 6.0)
    inp['rwkv_mu'] = uni((DEPTH, RWKV_PROJ), 0.0, 1.0)
    inp['rwkv_w0'] = uni((DEPTH, RWKV_W), -6.0, 1.0)
    inp['rwkv_w2'] = nrm((DEPTH, RWKV_W_RANK, RWKV_W), 0.1)
    inp['rwkv_a0'] = nrm((DEPTH, RWKV_W), 0.5)
    inp['rwkv_a2'] = nrm((DEPTH, RWKV_A_RANK, RWKV_W), RWKV_A_RANK ** -0.5)
    inp['rwkv_g2'] = nrm((DEPTH, RWKV_G_RANK, RWKV_W), RWKV_G_RANK ** -0.5)
    inp['rwkv_kk'] = 0.85 + nrm((DEPTH, RWKV_W), 0.1)
    inp['rwkv_ka'] = 1.0 + nrm((DEPTH, RWKV_W), 0.1)
    inp['rwkv_rk'] = nrm((DEPTH, RWKV_HEADS, RWKV_HD), 0.1)
    inp['rwkv_ln_w'] = 1.0 + nrm((DEPTH, RWKV_W), 0.1)
    inp['rwkv_ln_b'] = nrm((DEPTH, RWKV_W), 0.1)
    inp['w_branch_ret'] = nrm((DEPTH, RET_W, D), RET_W ** -0.5)
    inp['w_branch_fox'] = nrm((DEPTH, FOX_W, D), FOX_W ** -0.5)
    inp['w_branch_rwkv'] = nrm((DEPTH, RWKV_W, D), RWKV_W ** -0.5)
    inp['w_out'] = nrm((DEPTH, D, D), D ** -0.5)
    inp['router_group_w'] = nrm((DEPTH, D, N_GROUPS), D ** -0.5)
    inp['router_group_b'] = nrm((DEPTH, N_GROUPS), 0.01)
    inp['router_expert_w'] = nrm((DEPTH, N_GROUPS, D, EXPERTS_PER_GROUP), D ** -0.5)
    inp['router_expert_b'] = nrm((DEPTH, N_GROUPS, EXPERTS_PER_GROUP), 0.01)
    inp['expert_w1'] = nrm((DEPTH, N_EXPERTS, D, EXPERT_FF), D ** -0.5)
    inp['expert_w3'] = nrm((DEPTH, N_EXPERTS, D, EXPERT_FF), D ** -0.5)
    inp['expert_w2'] = nrm((DEPTH, N_EXPERTS, EXPERT_FF, D), EXPERT_FF ** -0.5)
    return inp


def reference(x_prompt, x_sample, cache_k, cache_v, cache_logf, page_table, state_ret, state_rwkv, state_shift,
              meta_tokens, norm_mix, norm_ffn, norm_final, w_in, fox_forget_bias,
              rwkv_mu, rwkv_w0, rwkv_w2, rwkv_a0, rwkv_a2, rwkv_g2, rwkv_kk, rwkv_ka, rwkv_rk, rwkv_ln_w, rwkv_ln_b,
              w_branch_ret, w_branch_fox, w_branch_rwkv, w_out,
              router_group_w, router_group_b, router_expert_w, router_expert_b, expert_w1, expert_w3, expert_w2):
    B, S_len, D = x_prompt.shape
    DB, n_new, _ = x_sample.shape
    f32 = jnp.float32
    past_len = page_table.shape[1] * cache_k.shape[2]
    xp = jnp.concatenate([jnp.broadcast_to(meta_tokens[None].astype(x_prompt.dtype), (B, N_META, D)), x_prompt], axis=1)
    Tp = S_len + N_META
    pos_p = jnp.arange(Tp, dtype=jnp.int32)
    pos_s = past_len + jnp.arange(n_new, dtype=jnp.int32)
    xs = x_sample
    zero_ret = jnp.zeros((B, RET_HEADS, RET_DK, RET_DV), f32)
    zero_rwkv = jnp.zeros((B, RWKV_HEADS, RWKV_HD, RWKV_HD), f32)
    zero_shift = jnp.zeros((B, RWKV_PROJ), x_prompt.dtype)
    outs_p = [[] for _ in range(6)]
    outs_s = [[] for _ in range(6)]
    for l in range(DEPTH):
        lp = dict(w_in=w_in[l], fox_b=fox_forget_bias[l], mu=rwkv_mu[l], w0=rwkv_w0[l], w2=rwkv_w2[l],
                  a0=rwkv_a0[l], a2=rwkv_a2[l], g2=rwkv_g2[l], kk=rwkv_kk[l], ka=rwkv_ka[l], rk=rwkv_rk[l],
                  ln_w=rwkv_ln_w[l], ln_b=rwkv_ln_b[l], wb_ret=w_branch_ret[l], wb_fox=w_branch_fox[l],
                  wb_rwkv=w_branch_rwkv[l], w_out=w_out[l])
        moe_args = (router_group_w[l], router_group_b[l], router_expert_w[l], router_expert_b[l],
                    expert_w1[l], expert_w3[l], expert_w2[l])
        m, st = mixer_layer(rmsnorm(xp, norm_mix[l]), lp, pos_p, zero_ret, zero_rwkv, zero_shift, None, True)
        xp = xp + m
        xp = xp + hier_moe(rmsnorm(xp, norm_ffn[l]).reshape(B * Tp, D), *moe_args).reshape(B, Tp, D)
        for j in range(6):
            outs_p[j].append(st[j])
        pk = cache_k[l][page_table].reshape(DB, past_len, FOX_HEADS, FOX_HD)
        pv = cache_v[l][page_table].reshape(DB, past_len, FOX_HEADS, FOX_HD)
        plf = cache_logf[l][page_table].reshape(DB, past_len, FOX_HEADS)
        m, st = mixer_layer(rmsnorm(xs, norm_mix[l]), lp, pos_s, state_ret[l], state_rwkv[l], state_shift[l], (pk, pv, plf), False)
        xs = xs + m
        xs = xs + hier_moe(rmsnorm(xs, norm_ffn[l]).reshape(DB * n_new, D), *moe_args).reshape(DB, n_new, D)
        for j in range(6):
            outs_s[j].append(st[j])
    y_prompt = rmsnorm(xp, norm_final)[:, N_META:]
    y_sample = rmsnorm(xs, norm_final)
    k_prompt, v_prompt, logf_prompt, ret_prompt, rwkv_prompt, shift_prompt = [jnp.stack(o, axis=0) for o in outs_p]
    k_sample, v_sample, logf_sample, ret_sample, rwkv_sample, shift_sample = [jnp.stack(o, axis=0) for o in outs_s]
    return (y_prompt, y_sample, k_prompt, v_prompt, logf_prompt, ret_prompt, rwkv_prompt, shift_prompt,
            k_sample, v_sample, logf_sample, ret_sample, rwkv_sample, shift_sample)
```

```python
import functools

import numpy as np
import jax
import jax.numpy as jnp
from jax import lax
from jax.experimental import pallas as pl
from jax.experimental.pallas import tpu as pltpu

F32 = jnp.float32
BF16 = jnp.bfloat16
HIGHEST = lax.Precision.HIGHEST

D_MODEL = 2048
N_META = 16
CHUNK = 128
PAD_FRONT = CHUNK - N_META
RMS_EPS = 1e-6
GN_EPS = 1e-5
RET_HEADS = 4
RET_DK = 256
RET_DV = 256
RET_W = RET_HEADS * RET_DK
ROPE_BASE = 10000.0
FOX_HEADS = 8
FOX_HD = 128
FOX_W = FOX_HEADS * FOX_HD
RWKV_HEADS = 16
RWKV_HD = 64
RWKV_W = RWKV_HEADS * RWKV_HD
RWKV_W_RANK = 64
RWKV_A_RANK = 64
RWKV_G_RANK = 128
RWKV_GN_EPS = 64e-5
RWKV_PROJ = 3 * RWKV_W + RWKV_W_RANK + RWKV_A_RANK + RWKV_G_RANK
RWKV_CHUNK = 64
N_BRANCH = 3
N_GROUPS = 4
EXPERTS_PER_GROUP = 8
N_EXPERTS = N_GROUPS * EXPERTS_PER_GROUP
TOP_K = 2
EXPERT_FF = 1024
MOE_BLOCK = 128
ROUTER_COLS = 128

FF_PAD = 256
COL_RW = 0
COL_RQ = COL_RW + RWKV_PROJ
COL_RK = COL_RQ + RET_W
COL_RV = COL_RK + RET_W
COL_RG = COL_RV + RET_W
COL_FQ = COL_RG + RET_W
COL_FK = COL_FQ + FOX_W
COL_FV = COL_FK + FOX_W
COL_FF = COL_FV + FOX_W
COL_GATE = COL_FF + FF_PAD
PROJ_PACKED = COL_GATE + N_BRANCH * D_MODEL
SRC_RQ = 0
SRC_FF = 4 * RET_W + 3 * FOX_W
SRC_RW = SRC_FF + FOX_HEADS
SRC_GATE = SRC_RW + RWKV_PROJ

VMEM_LIMIT = 56 * 1024 * 1024
NEG_BIG = -1e30


def _cparams(*sem):
    return pltpu.CompilerParams(dimension_semantics=sem, vmem_limit_bytes=VMEM_LIMIT)


def _pick(n, prefs):
    for p in prefs:
        if n % p == 0:
            return p
    return n


def _rms(x, g):
    return x * lax.rsqrt(jnp.mean(x * x, axis=-1, keepdims=True) + RMS_EPS) * g


def _rms_body(x_ref, g_ref, o_ref):
    o_ref[...] = _rms(x_ref[...], g_ref[...]).astype(o_ref.dtype)


def _rmsnorm(x, g, out_dtype):
    n, d = x.shape
    tm = _pick(n, (256, 128, 8))
    return pl.pallas_call(
        _rms_body,
        grid=(n // tm,),
        in_specs=[pl.BlockSpec((tm, d), lambda i: (i, 0)), pl.BlockSpec((1, d), lambda i: (0, 0))],
        out_specs=pl.BlockSpec((tm, d), lambda i: (i, 0)),
        out_shape=jax.ShapeDtypeStruct((n, d), out_dtype),
        compiler_params=_cparams("parallel"),
        name="rmsnorm",
    )(x, g.reshape(1, d))


def _rms_router_body(x_ref, g_ref, wr_ref, h_ref, lg_ref):
    h = _rms(x_ref[...], g_ref[...])
    h_ref[...] = h
    lg_ref[...] = jnp.dot(h.astype(BF16), wr_ref[...].astype(BF16), preferred_element_type=F32)


def _rmsnorm_router(x, g, wr):
    n, d = x.shape
    tm = _pick(n, (256, 128, 8))
    return pl.pallas_call(
        _rms_router_body,
        grid=(n // tm,),
        in_specs=[pl.BlockSpec((tm, d), lambda i: (i, 0)), pl.BlockSpec((1, d), lambda i: (0, 0)),
                  pl.BlockSpec((d, ROUTER_COLS), lambda i: (0, 0))],
        out_specs=[pl.BlockSpec((tm, d), lambda i: (i, 0)), pl.BlockSpec((tm, ROUTER_COLS), lambda i: (i, 0))],
        out_shape=[jax.ShapeDtypeStruct((n, d), F32), jax.ShapeDtypeStruct((n, ROUTER_COLS), F32)],
        compiler_params=_cparams("parallel"),
        name="rmsnorm_router",
    )(x, g.reshape(1, d), wr)


def _final_norm_prompt(x, g, batch, tp):
    d = x.shape[1]
    nb = tp // CHUNK
    return pl.pallas_call(
        _rms_body,
        grid=(batch, nb - 1),
        in_specs=[pl.BlockSpec((CHUNK, d), lambda b, j: (b * nb + 1 + j, 0)), pl.BlockSpec((1, d), lambda b, j: (0, 0))],
        out_specs=pl.BlockSpec((CHUNK, d), lambda b, j: (b * (nb - 1) + j, 0)),
        out_shape=jax.ShapeDtypeStruct((batch * (tp - CHUNK), d), F32),
        compiler_params=_cparams("parallel", "parallel"),
        name="final_norm",
    )(x, g.reshape(1, d)).reshape(batch, tp - CHUNK, d)


def _mm_body(a_ref, w_ref, o_ref):
    o_ref[...] = jnp.dot(a_ref[...], w_ref[...], preferred_element_type=F32).astype(o_ref.dtype)


def _matmul(a, w, out_dtype=F32, tm_prefs=(1408, 768, 512, 384, 256, 128), tn_prefs=(512, 256, 128)):
    m, k = a.shape
    n = w.shape[1]
    tm = _pick(m, tm_prefs)
    tn = _pick(n, tn_prefs)
    return pl.pallas_call(
        _mm_body,
        grid=(m // tm, n // tn),
        in_specs=[pl.BlockSpec((tm, k), lambda i, j: (i, 0)), pl.BlockSpec((k, tn), lambda i, j: (0, j))],
        out_specs=pl.BlockSpec((tm, tn), lambda i, j: (i, j)),
        out_shape=jax.ShapeDtypeStruct((m, n), out_dtype),
        compiler_params=_cparams("parallel", "arbitrary"),
        name="matmul",
    )(a, w)


def _mm_res_body(a_ref, w_ref, r_ref, o_ref, *, blocks_per_seq, pad):
    y = r_ref[...] + jnp.dot(a_ref[...], w_ref[...], preferred_element_type=F32)
    if pad:
        first = (pl.program_id(0) % blocks_per_seq) == 0
        row = lax.broadcasted_iota(jnp.int32, y.shape, 0)
        y = jnp.where(jnp.logical_and(first, row < pad), 0.0, y)
    o_ref[...] = y


def _matmul_residual(a, w, res, tp=None):
    m, k = a.shape
    n = w.shape[1]
    tm = _pick(tp, (768, 384, 128)) if tp else m
    assert m % tm == 0
    tn = _pick(n, (512, 256, 128))
    body = functools.partial(_mm_res_body, blocks_per_seq=(tp // tm if tp else 1), pad=(PAD_FRONT if tp else 0))
    return pl.pallas_call(
        body,
        grid=(m // tm, n // tn),
        in_specs=[pl.BlockSpec((tm, k), lambda i, j: (i, 0)), pl.BlockSpec((k, tn), lambda i, j: (0, j)),
                  pl.BlockSpec((tm, tn), lambda i, j: (i, j))],
        out_specs=pl.BlockSpec((tm, tn), lambda i, j: (i, j)),
        out_shape=jax.ShapeDtypeStruct((m, n), F32),
        compiler_params=_cparams("parallel", "arbitrary"),
        name="matmul_residual",
    )(a, w, res)


def _rope_halves(x, cos, sin):
    half = x.shape[-1] // 2
    x1, x2 = x[:, :half], x[:, half:]
    return jnp.concatenate([x1 * cos - x2 * sin, x1 * sin + x2 * cos], axis=-1)


def _head_norm(y, eps):
    mu = jnp.mean(y, axis=-1, keepdims=True)
    yc = y - mu
    return yc * lax.rsqrt(jnp.mean(yc * yc, axis=-1, keepdims=True) + eps)


def _silu(x):
    return x / (1.0 + jnp.exp(-x))


def _sigmoid(x):
    return 1.0 / (1.0 + jnp.exp(-x))


def _ret_body(q_ref, k_ref, v_ref, g_ref, cos_ref, sin_ref, dm_ref, cd_ref, kd_ref, sd_ref, o_ref, s_ref):
    c = pl.program_id(2)

    @pl.when(c == 0)
    def _():
        s_ref[...] = jnp.zeros(s_ref.shape, F32)

    cos = cos_ref[...]
    sin = sin_ref[...]
    q = _rope_halves(q_ref[...], cos, sin)
    k = _rope_halves(k_ref[...], cos, sin) * (RET_DK ** -0.5)
    qb = q.astype(BF16)
    kb = k.astype(BF16)
    vb = v_ref[...].astype(BF16)
    s_old = s_ref[0, 0]
    scores = lax.dot_general(qb, kb, (((1,), (1,)), ((), ())), preferred_element_type=F32) * dm_ref[0]
    intra = jnp.dot(scores.astype(BF16), vb, preferred_element_type=F32)
    cross = jnp.dot(qb, s_old.astype(BF16), preferred_element_type=F32) * cd_ref[0]
    kdec = (k * kd_ref[0]).astype(BF16)
    s_ref[0, 0] = sd_ref[0] * s_old + lax.dot_general(kdec, vb, (((0,), (0,)), ((), ())), preferred_element_type=F32)
    o = _head_norm(intra + cross, GN_EPS) * _silu(g_ref[...])
    o_ref[...] = o.astype(o_ref.dtype)


def _ret_tables(length):
    lg = jnp.log1p(-jnp.power(2.0, -5.0 - jnp.arange(RET_HEADS, dtype=F32)))
    i = jnp.arange(length, dtype=F32)
    diff = i[:, None] - i[None, :]
    dmask = jnp.where(diff >= 0, jnp.exp(lg[:, None, None] * jnp.maximum(diff, 0.0)), 0.0)
    cdec = jnp.exp(lg[:, None] * (i + 1.0)[None, :])
    kdec = jnp.exp(lg[:, None] * (length - 1.0 - i)[None, :])
    sdec = jnp.exp(lg * length)
    return lg, dmask, cdec, kdec, sdec


def _rope_tables(pos):
    half = RET_DK // 2
    inv = ROPE_BASE ** (-jnp.arange(half, dtype=F32) / half)
    ang = pos.astype(F32)[:, None] * inv[None, :]
    return jnp.cos(ang), jnp.sin(ang)


def _retention_prompt(proj, batch, tp):
    n = proj.shape[0]
    nc = tp // CHUNK
    _, dmask, cdec, kdec, sdec = _ret_tables(CHUNK)
    cdec = jnp.broadcast_to(cdec[:, :, None], (RET_HEADS, CHUNK, RET_DV))
    kdec = jnp.broadcast_to(kdec[:, :, None], (RET_HEADS, CHUNK, RET_DK))
    sdec = jnp.broadcast_to(sdec[:, None, None], (RET_HEADS, 1, RET_DV))
    cos, sin = _rope_tables(jnp.arange(tp, dtype=jnp.int32) - PAD_FRONT)
    blk = lambda col: pl.BlockSpec((CHUNK, RET_DK), lambda b, h, c, col=col: (b * nc + c, col // RET_DK + h))
    tab = lambda shape: pl.BlockSpec((1,) + shape, lambda b, h, c: (h, 0, 0))
    rope = pl.BlockSpec((CHUNK, RET_DK // 2), lambda b, h, c: (c, 0))
    return pl.pallas_call(
        _ret_body,
        grid=(batch, RET_HEADS, nc),
        in_specs=[blk(COL_RQ), blk(COL_RK), blk(COL_RV), blk(COL_RG), rope, rope,
                  tab((CHUNK, CHUNK)), tab((CHUNK, RET_DV)), tab((CHUNK, RET_DK)), tab((1, RET_DV))],
        out_specs=[pl.BlockSpec((CHUNK, RET_DV), lambda b, h, c: (b * nc + c, h)),
                   pl.BlockSpec((1, 1, RET_DK, RET_DV), lambda b, h, c: (b, h, 0, 0))],
        out_shape=[jax.ShapeDtypeStruct((n, RET_W), BF16), jax.ShapeDtypeStruct((batch, RET_HEADS, RET_DK, RET_DV), F32)],
        compiler_params=_cparams("parallel", "parallel", "arbitrary"),
        name="retention_prompt",
    )(proj, proj, proj, proj, cos, sin, dmask, cdec, kdec, sdec)


def _fox_body(qi_ref, ki_ref, q_ref, k_ref, v_ref, cq_ref, ck_ref, o_ref, m_sc, l_sc, acc_sc, *, tq, tk):
    step = pl.program_id(2)
    qi = qi_ref[step]
    ki = ki_ref[step]

    @pl.when(ki == 0)
    def _():
        m_sc[...] = jnp.full(m_sc.shape, NEG_BIG, F32)
        l_sc[...] = jnp.zeros(l_sc.shape, F32)
        acc_sc[...] = jnp.zeros(acc_sc.shape, F32)

    q = q_ref[...].astype(BF16)
    k = k_ref[...].astype(BF16)
    v = v_ref[...].astype(BF16)
    s = lax.dot_general(q, k, (((1,), (1,)), ((), ())), preferred_element_type=F32) * (FOX_HD ** -0.5)
    cq = cq_ref[0, 0]
    s = s + jnp.concatenate([cq] * (tk // 128), axis=1) - ck_ref[0, 0]
    qpos = qi * tq + lax.broadcasted_iota(jnp.int32, (tq, tk), 0)
    kpos = ki * tk + lax.broadcasted_iota(jnp.int32, (tq, tk), 1)
    s = jnp.where(jnp.logical_and(kpos <= qpos, kpos >= PAD_FRONT), s, NEG_BIG)
    m_old = m_sc[...]
    m_new = jnp.maximum(m_old, jnp.max(s, axis=1, keepdims=True))
    alpha = jnp.exp(m_old - m_new)
    p = jnp.exp(s - m_new)
    l_sc[...] = alpha * l_sc[...] + jnp.sum(p, axis=1, keepdims=True)
    acc_sc[...] = alpha * acc_sc[...] + jnp.dot(p.astype(BF16), v, preferred_element_type=F32)
    m_sc[...] = m_new

    @pl.when(ki == qi)
    def _():
        o_ref[...] = (acc_sc[...] / l_sc[...]).astype(o_ref.dtype)


def _fox_prompt(proj, c, batch, tp):
    n = proj.shape[0]
    tq = _pick(tp, (384, 256, 128))
    nq = tp // tq
    pairs = [(i, j) for i in range(nq) for j in range(i + 1)]
    qi_tab = jnp.asarray(np.array([p[0] for p in pairs], np.int32))
    ki_tab = jnp.asarray(np.array([p[1] for p in pairs], np.int32))
    cq = jnp.broadcast_to(c[..., None], (batch, FOX_HEADS, tp, 128))
    ck = c.reshape(batch, FOX_HEADS, 1, tp)
    cb = lambda col: col // FOX_HD
    grid_spec = pltpu.PrefetchScalarGridSpec(
        num_scalar_prefetch=2,
        grid=(batch, FOX_HEADS, len(pairs)),
        in_specs=[
            pl.BlockSpec((tq, FOX_HD), lambda b, h, s, qi, ki: (b * nq + qi[s], cb(COL_FQ) + h)),
            pl.BlockSpec((tq, FOX_HD), lambda b, h, s, qi, ki: (b * nq + ki[s], cb(COL_FK) + h)),
            pl.BlockSpec((tq, FOX_HD), lambda b, h, s, qi, ki: (b * nq + ki[s], cb(COL_FV) + h)),
            pl.BlockSpec((1, 1, tq, 128), lambda b, h, s, qi, ki: (b, h, qi[s], 0)),
            pl.BlockSpec((1, 1, 1, tq), lambda b, h, s, qi, ki: (b, h, 0, ki[s])),
        ],
        out_specs=pl.BlockSpec((tq, FOX_HD), lambda b, h, s, qi, ki: (b * nq + qi[s], h)),
        scratch_shapes=[pltpu.VMEM((tq, 1), F32), pltpu.VMEM((tq, 1), F32), pltpu.VMEM((tq, FOX_HD), F32)],
    )
    return pl.pallas_call(
        functools.partial(_fox_body, tq=tq, tk=tq),
        grid_spec=grid_spec,
        out_shape=jax.ShapeDtypeStruct((n, FOX_W), BF16),
        compiler_params=_cparams("parallel", "parallel", "arbitrary"),
        name="fox_prompt",
    )(qi_tab, ki_tab, proj, proj, proj, cq, ck)


def _softplus(z):
    return jnp.maximum(z, 0.0) + jnp.log1p(jnp.exp(-jnp.abs(z)))


def _rwkv_prep_math(c, prev, mu, w0, w2, a0, a2, g2, kkp, ka, exact):
    w = RWKV_W
    xm = c + mu * (prev - c)
    r, k, v = xm[:, 0:w], xm[:, w:2 * w], xm[:, 2 * w:3 * w]
    wd = xm[:, 3 * w:3 * w + RWKV_W_RANK]
    ad = xm[:, 3 * w + RWKV_W_RANK:3 * w + RWKV_W_RANK + RWKV_A_RANK]
    gd = xm[:, 3 * w + RWKV_W_RANK + RWKV_A_RANK:]
    if exact:
        mm = lambda x, m: jnp.dot(x, m, precision=HIGHEST, preferred_element_type=F32)
    else:
        mm = lambda x, m: jnp.dot(x.astype(BF16), m.astype(BF16), preferred_element_type=F32)
    w_log = -_softplus(-(w0 + mm(jnp.tanh(wd), w2))) - 0.5
    lw = -jnp.exp(w_log)
    a = _sigmoid(a0 + mm(ad, a2))
    g = mm(_sigmoid(gd), g2)
    kk0 = k * kkp
    kmod = k * (1.0 + (a - 1.0) * ka)
    return r, kmod, v, lw, kk0, a, g


def _rwkv_prep_body(c_ref, mu_ref, w0_ref, w2_ref, a0_ref, a2_ref, g2_ref, kkp_ref, ka_ref,
                    r_ref, k_ref, v_ref, lw_ref, kk_ref, a_ref, g_ref, carry):
    t = pl.program_id(1)

    @pl.when(t == 0)
    def _():
        carry[...] = jnp.zeros(carry.shape, F32)

    c = c_ref[...]
    rows = c.shape[0]
    prev = pltpu.roll(c, 1, axis=0)
    row = lax.broadcasted_iota(jnp.int32, c.shape, 0)
    prev = jnp.where(row == 0, carry[...], prev)
    carry[...] = c[rows - 1:rows, :]
    outs = _rwkv_prep_math(c, prev, mu_ref[...], w0_ref[...], w2_ref[...], a0_ref[...], a2_ref[...], g2_ref[...],
                           kkp_ref[...], ka_ref[...], exact=False)
    for ref, val in zip((r_ref, k_ref, v_ref, lw_ref, kk_ref, a_ref, g_ref), outs):
        ref[...] = val


def _rwkv_prep_prompt(proj, lp, batch, tp):
    n = proj.shape[0]
    tb = CHUNK
    nt = tp // tb
    row = lambda x: x.reshape(1, -1)
    full = lambda shape: pl.BlockSpec(shape, lambda b, t: (0, 0))
    out_spec = pl.BlockSpec((tb, RWKV_W), lambda b, t: (b * nt + t, 0))
    return pl.pallas_call(
        _rwkv_prep_body,
        grid=(batch, nt),
        in_specs=[pl.BlockSpec((tb, RWKV_PROJ), lambda b, t: (b * nt + t, COL_RW // RWKV_PROJ)),
                  full((1, RWKV_PROJ)), full((1, RWKV_W)), full((RWKV_W_RANK, RWKV_W)), full((1, RWKV_W)),
                  full((RWKV_A_RANK, RWKV_W)), full((RWKV_G_RANK, RWKV_W)), full((1, RWKV_W)), full((1, RWKV_W))],
        out_specs=[out_spec] * 7,
        out_shape=[jax.ShapeDtypeStruct((n, RWKV_W), F32)] * 7,
        scratch_shapes=[pltpu.VMEM((1, RWKV_PROJ), F32)],
        compiler_params=_cparams("parallel", "arbitrary"),
        name="rwkv_prep",
    )(proj, row(lp['mu']), row(lp['w0']), lp['w2'], row(lp['a0']), lp['a2'], lp['g2'], row(lp['kk']), row(lp['ka']))


def _rwkv_chunk_body(r_ref, k_ref, v_ref, lw_ref, kk_ref, a_ref, g_ref, rk_ref, lnw_ref, lnb_ref, o_ref, s_ref):
    cn = pl.program_id(1)

    @pl.when(cn == 0)
    def _():
        s_ref[...] = jnp.zeros(s_ref.shape, F32)

    cs = RWKV_CHUNK
    hd = RWKV_HD
    row = lax.broadcasted_iota(jnp.int32, (cs, cs), 0)
    col = lax.broadcasted_iota(jnp.int32, (cs, cs), 1)
    strict = col < row
    incl = col <= row
    eye = (row == col).astype(F32)
    cum_all = jnp.dot(incl.astype(F32), lw_ref[...], precision=HIGHEST, preferred_element_type=F32)
    nt = (((1,), (1,)), ((), ()))
    tn = (((0,), (0,)), ((), ()))
    dot = functools.partial(jnp.dot, preferred_element_type=F32)

    def split(x):
        hi = x.astype(BF16)
        return hi, (x - hi.astype(F32)).astype(BF16)

    def mm_acc(x, y):
        xh, xl = split(x)
        yh, yl = split(y)
        return dot(xh, yh) + (dot(xh, yl) + dot(xl, yh))

    pair_masks = []
    for bit in range(int(np.log2(cs))):
        same_pair = (row >> (bit + 1)) == (col >> (bit + 1))
        pair_masks.append(same_pair & ((row & (1 << bit)) != 0) & ((col & (1 << bit)) == 0))
    for h in range(RWKV_HEADS):
        sl = slice(h * hd, (h + 1) * hd)
        lw = lw_ref[:, sl]
        cum = cum_all[:, sl]
        kk0 = kk_ref[:, sl]
        kk = kk0 * lax.rsqrt(jnp.sum(kk0 * kk0, axis=-1, keepdims=True) + 1e-12)
        b = kk * a_ref[:, sl]
        r = r_ref[:, sl]
        k = k_ref[:, sl]
        v = v_ref[:, sl]
        e_in = jnp.exp(cum)
        e_neg = jnp.exp(-cum)
        a_t = -kk * jnp.exp(cum - lw)
        ar = jnp.concatenate([a_t, r * e_in], axis=0).astype(BF16)
        bk = jnp.concatenate([b * e_neg, k * e_neg], axis=0).astype(BF16)
        gram = lax.dot_general(ar, bk, nt, preferred_element_type=F32)
        l_ab = jnp.where(strict, gram[:cs, :cs], 0.0)
        l_ak = jnp.where(strict, gram[:cs, cs:], 0.0)
        m_rb = jnp.where(incl, gram[cs:, :cs], 0.0)
        m_rk = jnp.where(incl, gram[cs:, cs:], 0.0)
        tinv = eye + jnp.where(pair_masks[0], l_ab, 0.0)
        for mask in pair_masks[1:]:
            tinv = tinv + mm_acc(mm_acc(tinv, jnp.where(mask, l_ab, 0.0)), tinv)
        s_old = s_ref[0, h]
        ars = lax.dot_general(ar, s_old.astype(BF16), nt, preferred_element_type=F32)
        vb = v.astype(BF16)
        rhs = ars[:cs] + dot(l_ak.astype(BF16), vb)
        u = dot(tinv.astype(BF16), rhs.astype(BF16))
        ub = u.astype(BF16)
        y = ars[cs:] + dot(m_rb.astype(BF16), ub) + dot(m_rk.astype(BF16), vb)
        tail = jnp.exp(cum[cs - 1:cs, :] - cum)
        s_ref[0, h] = (s_old * e_in[cs - 1:cs, :]
                       + lax.dot_general(ub, (b * tail).astype(BF16), tn, preferred_element_type=F32)
                       + lax.dot_general(vb, (k * tail).astype(BF16), tn, preferred_element_type=F32))
        yn = _head_norm(y, RWKV_GN_EPS) * lnw_ref[:, sl] + lnb_ref[:, sl]
        bonus = jnp.sum(r * k * rk_ref[:, sl], axis=-1, keepdims=True) * v
        o_ref[:, sl] = ((yn + bonus) * g_ref[:, sl]).astype(o_ref.dtype)


def _rwkv_chunk_prompt(prep, lp, batch, tp):
    n = prep[0].shape[0]
    cs = RWKV_CHUNK
    ncn = tp // cs
    row = lambda x: x.reshape(1, -1)
    blk = pl.BlockSpec((cs, RWKV_W), lambda b, c: (b * ncn + c, 0))
    full = pl.BlockSpec((1, RWKV_W), lambda b, c: (0, 0))
    return pl.pallas_call(
        _rwkv_chunk_body,
        grid=(batch, ncn),
        in_specs=[blk] * 7 + [full] * 3,
        out_specs=[blk, pl.BlockSpec((1, RWKV_HEADS, RWKV_HD, RWKV_HD), lambda b, c: (b, 0, 0, 0))],
        out_shape=[jax.ShapeDtypeStruct((n, RWKV_W), BF16),
                   jax.ShapeDtypeStruct((batch, RWKV_HEADS, RWKV_HD, RWKV_HD), F32)],
        compiler_params=_cparams("parallel", "arbitrary"),
        name="rwkv_chunk",
    )(*prep, row(lp['rk']), row(lp['ln_w']), row(lp['ln_b']))


def _merge_body(ro_ref, fo_ref, wo_ref, wr_ref, wf_ref, ww_ref, g0_ref, g1_ref, g2_ref, o_ref):
    dot = functools.partial(jnp.dot, preferred_element_type=F32)
    m = (_sigmoid(g0_ref[...]) * dot(ro_ref[...], wr_ref[...])
         + _sigmoid(g1_ref[...]) * dot(fo_ref[...], wf_ref[...])
         + _sigmoid(g2_ref[...]) * dot(wo_ref[...], ww_ref[...]))
    o_ref[...] = m.astype(o_ref.dtype)


def _merge(ret_o, fox_o, rw_o, wb_ret, wb_fox, wb_rwkv, gates, gate_col, out_dtype):
    m = ret_o.shape[0]
    tm = _pick(m, (768, 384, 128))
    tn = 512
    nj = D_MODEL // tn
    act = lambda width: pl.BlockSpec((tm, width), lambda i, j: (i, 0))
    wgt = lambda width: pl.BlockSpec((width, tn), lambda i, j: (0, j))
    gate = lambda br: pl.BlockSpec((tm, tn), lambda i, j, br=br: (i, gate_col // tn + br * nj + j))
    return pl.pallas_call(
        _merge_body,
        grid=(m // tm, nj),
        in_specs=[act(RET_W), act(FOX_W), act(RWKV_W), wgt(RET_W), wgt(FOX_W), wgt(RWKV_W), gate(0), gate(1), gate(2)],
        out_specs=pl.BlockSpec((tm, tn), lambda i, j: (i, j)),
        out_shape=jax.ShapeDtypeStruct((m, D_MODEL), out_dtype),
        compiler_params=_cparams("parallel", "arbitrary"),
        name="merge",
    )(ret_o, fox_o, rw_o, wb_ret, wb_fox, wb_rwkv, gates, gates, gates)


def _route(logits, bg, be):
    n = logits.shape[0]
    gp = jax.nn.softmax(logits[:, :N_GROUPS] + bg.astype(F32), axis=-1)
    gidx = jnp.argmax(gp, axis=-1)
    pg = jnp.take_along_axis(gp, gidx[:, None], axis=-1)
    el = logits[:, N_GROUPS:N_GROUPS + N_EXPERTS].reshape(n, N_GROUPS, EXPERTS_PER_GROUP) + be.astype(F32)[None]
    el = jnp.take_along_axis(el, gidx[:, None, None], axis=1)[:, 0]
    topv, topi = lax.top_k(jax.nn.softmax(el, axis=-1), TOP_K)
    gate = pg * topv / jnp.sum(topv, axis=-1, keepdims=True)
    eid = (gidx[:, None] * EXPERTS_PER_GROUP + topi).astype(jnp.int32)
    return eid, gate


def _router_weights(wg, we):
    d = wg.shape[0]
    wr = jnp.concatenate([wg, jnp.transpose(we, (1, 0, 2)).reshape(d, N_EXPERTS)], axis=1)
    return jnp.pad(wr, ((0, 0), (0, ROUTER_COLS - wr.shape[1])))


def _moe_body(be_ref, nused_ref, tok_ref, dst_ref, h_hbm, w1_ref, w3_ref, w2_ref, y_in, y_hbm, xbuf, ybuf, sem_in, sem_out):
    del y_in
    i = pl.program_id(0)

    @pl.when(i < nused_ref[0])
    def _():
        base = i * MOE_BLOCK

        def gather(r):
            return pltpu.make_async_copy(h_hbm.at[pl.ds(tok_ref[base + r], 1)], xbuf.at[pl.ds(r, 1)], sem_in)

        def scatter(r):
            return pltpu.make_async_copy(ybuf.at[pl.ds(r, 1)], y_hbm.at[pl.ds(dst_ref[base + r], 1)], sem_out)

        def run(make, wait):
            def step(r, carry):
                if wait:
                    make(r).wait()
                else:
                    make(r).start()
                return carry
            lax.fori_loop(0, MOE_BLOCK, step, 0)

        run(gather, False)
        run(gather, True)
        x = xbuf[...].astype(BF16)
        a = jnp.dot(x, w1_ref[0], preferred_element_type=F32)
        b = jnp.dot(x, w3_ref[0], preferred_element_type=F32)
        ybuf[...] = jnp.dot((_silu(a) * b).astype(BF16), w2_ref[0], preferred_element_type=F32)
        run(scatter, False)
        run(scatter, True)


def _moe_prompt(h, eid, gate, valid, w1, w3, w2):
    n, d = h.shape
    a_tot = n * TOP_K
    n_real = int(np.sum(valid)) * TOP_K
    n_blk = (n_real + N_EXPERTS * (MOE_BLOCK - 1) + MOE_BLOCK - 1) // MOE_BLOCK
    cap = n_blk * MOE_BLOCK
    validf = jnp.repeat(jnp.asarray(valid), TOP_K)
    eflat = eid.reshape(-1)
    onehot = jnp.logical_and(eflat[:, None] == jnp.arange(N_EXPERTS, dtype=jnp.int32)[None, :], validf[:, None]).astype(jnp.int32)
    rank = jnp.sum((jnp.cumsum(onehot, axis=0) - onehot) * onehot, axis=1)
    counts = jnp.sum(onehot, axis=0)
    padded = (counts + MOE_BLOCK - 1) // MOE_BLOCK * MOE_BLOCK
    pad_end = jnp.cumsum(padded)
    pad_start = pad_end - padded
    dest = jnp.where(validf, pad_start[eflat] + rank, cap)
    slot_row = jnp.arange(a_tot, dtype=jnp.int32)
    zero_row = int(np.argmin(valid))
    tok_buf = jnp.full((cap,), zero_row, jnp.int32).at[dest].set(slot_row // TOP_K, mode='drop')
    dump = a_tot + (jnp.arange(cap, dtype=jnp.int32) % MOE_BLOCK)
    dst_buf = dump.at[dest].set(slot_row, mode='drop')
    n_used = (pad_end[-1] // MOE_BLOCK).astype(jnp.int32)
    blk = jnp.arange(n_blk, dtype=jnp.int32)
    blk_e = jnp.minimum(jnp.searchsorted(pad_end, blk * MOE_BLOCK, side='right'), N_EXPERTS - 1).astype(jnp.int32)
    blk_e = jnp.where(blk < n_used, blk_e, blk_e[jnp.maximum(n_used - 1, 0)])
    y_rows = a_tot + MOE_BLOCK
    y0 = jnp.zeros((y_rows, d), F32)
    ff = w1.shape[2]
    grid_spec = pltpu.PrefetchScalarGridSpec(
        num_scalar_prefetch=4,
        grid=(n_blk,),
        in_specs=[pl.BlockSpec(memory_space=pl.ANY),
                  pl.BlockSpec((1, d, ff), lambda i, be, nu, tk, ds: (be[i], 0, 0)),
                  pl.BlockSpec((1, d, ff), lambda i, be, nu, tk, ds: (be[i], 0, 0)),
                  pl.BlockSpec((1, ff, d), lambda i, be, nu, tk, ds: (be[i], 0, 0)),
                  pl.BlockSpec(memory_space=pl.ANY)],
        out_specs=pl.BlockSpec(memory_space=pl.ANY),
        scratch_shapes=[pltpu.VMEM((MOE_BLOCK, d), F32), pltpu.VMEM((MOE_BLOCK, d), F32),
                        pltpu.SemaphoreType.DMA(()), pltpu.SemaphoreType.DMA(())],
    )
    y2 = pl.pallas_call(
        _moe_body,
        grid_spec=grid_spec,
        out_shape=jax.ShapeDtypeStruct((y_rows, d), F32),
        input_output_aliases={8: 0},
        compiler_params=_cparams("arbitrary"),
        name="moe_experts",
    )(blk_e, n_used.reshape(1), tok_buf, dst_buf, h, w1, w3, w2, y0)
    return y2.reshape(y_rows // TOP_K, TOP_K * d)


def _combine_body(x_ref, y_ref, g_ref, o_ref):
    d = x_ref.shape[1]
    g = g_ref[...]
    o_ref[...] = x_ref[...] + (y_ref[:, :d] * g[:, 0:1] + y_ref[:, d:] * g[:, 1:2])


def _moe_combine(x, y2, gate):
    n, d = x.shape
    tm = _pick(n, (256, 128, 8))
    gpad = jnp.pad(gate, ((0, 0), (0, 128 - TOP_K)))
    return pl.pallas_call(
        _combine_body,
        grid=(n // tm,),
        in_specs=[pl.BlockSpec((tm, d), lambda i: (i, 0)), pl.BlockSpec((tm, TOP_K * d), lambda i: (i, 0)),
                  pl.BlockSpec((tm, 128), lambda i: (i, 0))],
        out_specs=pl.BlockSpec((tm, d), lambda i: (i, 0)),
        out_shape=jax.ShapeDtypeStruct((n, d), F32),
        compiler_params=_cparams("parallel"),
        name="moe_combine",
    )(x, y2, gpad)


def _pack_w_in(w_in):
    ff = jnp.pad(w_in[:, SRC_FF:SRC_FF + FOX_HEADS], ((0, 0), (0, FF_PAD - FOX_HEADS)))
    return jnp.concatenate([w_in[:, SRC_RW:SRC_RW + RWKV_PROJ], w_in[:, :SRC_FF], ff, w_in[:, SRC_GATE:]],
                           axis=1).astype(BF16)


def _prompt_layer(x, lp, moe, batch, tp, valid):
    h = _rmsnorm(x, lp['norm_mix'], BF16)
    proj = _matmul(h, lp['w_in'])
    ret_o, ret_s = _retention_prompt(proj, batch, tp)
    ff = proj[:, COL_FF:COL_FF + FOX_HEADS].reshape(batch, tp, FOX_HEADS)
    logf = jax.nn.log_sigmoid(ff + lp['fox_b'].astype(F32))
    c = jnp.cumsum(logf, axis=1).transpose(0, 2, 1)
    fox_o = _fox_prompt(proj, c, batch, tp)
    prep = _rwkv_prep_prompt(proj, lp, batch, tp)
    rw_o, rw_s = _rwkv_chunk_prompt(prep, lp, batch, tp)
    merged = _merge(ret_o, fox_o, rw_o, lp['wb_ret'], lp['wb_fox'], lp['wb_rwkv'], proj, COL_GATE, BF16)
    x = _matmul_residual(merged, lp['w_out'], x, tp=tp)
    h2, logits = _rmsnorm_router(x, lp['norm_ffn'], moe['wr'])
    eid, gate = _route(logits, moe['bg'], moe['be'])
    y2 = _moe_prompt(h2, eid, gate, valid, moe['w1'], moe['w3'], moe['w2'])
    x = _moe_combine(x, y2, gate)
    p3 = proj.reshape(batch, tp, PROJ_PACKED)
    heads = lambda col: p3[:, PAD_FRONT:, col:col + FOX_W].reshape(batch, tp - PAD_FRONT, FOX_HEADS, FOX_HD)
    state = (heads(COL_FK), heads(COL_FV), logf[:, PAD_FRONT:], ret_s, rw_s, p3[:, tp - 1, COL_RW:COL_RW + RWKV_PROJ])
    return x, state


_NT = (((1,), (1,)), ((), ()))
_TN = (((0,), (0,)), ((), ()))


def _split_hi_lo(x):
    hi = x.astype(BF16)
    return jnp.concatenate([hi, (x - hi.astype(F32)).astype(BF16)], axis=0)


def _mm_sample(x, w):
    m = x.shape[0]
    y = _matmul(_split_hi_lo(x), w, tn_prefs=(1536, 1024, 512, 256, 128))
    return y[:m] + y[m:]


def _pad8(x):
    first = lax.broadcasted_iota(jnp.int32, (8, x.shape[1]), 0) == 0
    return jnp.where(first, jnp.broadcast_to(x, (8, x.shape[1])), 0.0)


def _ret_sample_body(q_ref, k_ref, v_ref, g_ref, cos_ref, sin_ref, dec_ref, s0_ref, o_ref, s_ref):
    b = pl.program_id(1)
    row1 = lambda ref: ref[pl.ds(b, 1), :]
    cos = cos_ref[...]
    sin = sin_ref[...]
    q = _rope_halves(row1(q_ref), cos, sin)
    k = _rope_halves(row1(k_ref), cos, sin) * (RET_DK ** -0.5)
    v = row1(v_ref)
    dec = dec_ref[0]
    s0 = s0_ref[0, 0]
    cross = jnp.dot(_pad8(q), s0, precision=HIGHEST, preferred_element_type=F32)[0:1] * dec
    intra = jnp.sum(q * k, axis=-1, keepdims=True) * v
    s_ref[0, 0] = dec * s0 + lax.dot_general(_pad8(k), _pad8(v), _TN, precision=HIGHEST, preferred_element_type=F32)
    o_ref[pl.ds(b, 1), :] = _head_norm(intra + cross, GN_EPS) * _silu(row1(g_ref))


def _retention_sample(proj, s0, pos):
    nb = proj.shape[0]
    lg = _ret_tables(1)[0]
    dec = jnp.broadcast_to(jnp.exp(lg)[:, None, None], (RET_HEADS, 1, RET_DV))
    cos, sin = _rope_tables(pos)
    blk = lambda col: pl.BlockSpec((nb, RET_DK), lambda h, b, col=col: (0, col // RET_DK + h))
    rope = pl.BlockSpec((1, RET_DK // 2), lambda h, b: (0, 0))
    st = pl.BlockSpec((1, 1, RET_DK, RET_DV), lambda h, b: (b, h, 0, 0))
    return pl.pallas_call(
        _ret_sample_body,
        grid=(RET_HEADS, nb),
        in_specs=[blk(COL_RQ), blk(COL_RK), blk(COL_RV), blk(COL_RG), rope, rope,
                  pl.BlockSpec((1, 1, RET_DV), lambda h, b: (h, 0, 0)), st],
        out_specs=[pl.BlockSpec((nb, RET_DV), lambda h, b: (0, h)), st],
        out_shape=[jax.ShapeDtypeStruct((nb, RET_W), F32), jax.ShapeDtypeStruct(s0.shape, F32)],
        compiler_params=_cparams("parallel", "arbitrary"),
        name="retention_sample",
    )(proj, proj, proj, proj, cos, sin, dec, s0)


def _rwkv_prep_sample_body(c_ref, prev_ref, mu_ref, w0_ref, w2_ref, a0_ref, a2_ref, g2_ref, kkp_ref, ka_ref, *out_refs):
    outs = _rwkv_prep_math(c_ref[...], prev_ref[...], mu_ref[...], w0_ref[...], w2_ref[...], a0_ref[...], a2_ref[...],
                           g2_ref[...], kkp_ref[...], ka_ref[...], exact=True)
    for ref, val in zip(out_refs, outs):
        ref[...] = val


def _rwkv_step_body(r_ref, k_ref, v_ref, lw_ref, kk_ref, a_ref, g_ref, rk_ref, lnw_ref, lnb_ref, s0_ref, o_ref, s_ref):
    b = pl.program_id(0)
    dg = functools.partial(lax.dot_general, precision=HIGHEST, preferred_element_type=F32)
    r_all, k_all, v_all, lw_all, kk_all, a_all, g_all = (
        ref[pl.ds(b, 1), :] for ref in (r_ref, k_ref, v_ref, lw_ref, kk_ref, a_ref, g_ref))
    rk_all, lnw_all, lnb_all = rk_ref[...], lnw_ref[...], lnb_ref[...]
    outs = []
    for h in range(RWKV_HEADS):
        sl = slice(h * RWKV_HD, (h + 1) * RWKV_HD)
        kk0 = kk_all[:, sl]
        kk = kk0 * lax.rsqrt(jnp.sum(kk0 * kk0, axis=-1, keepdims=True) + 1e-12)
        bb = kk * a_all[:, sl]
        w = jnp.exp(lw_all[:, sl])
        r, k, v = r_all[:, sl], k_all[:, sl], v_all[:, sl]
        s0 = s0_ref[0, h]
        sa = dg(s0, _pad8(-kk), _NT)[:, 0:1]
        s_new = s0 * w + sa * bb + dg(_pad8(v), _pad8(k), _TN)
        s_ref[0, h] = s_new
        y = dg(_pad8(r), s_new, _NT)[0:1]
        yn = _head_norm(y, RWKV_GN_EPS) * lnw_all[:, sl] + lnb_all[:, sl]
        bonus = jnp.sum(r * k * rk_all[:, sl], axis=-1, keepdims=True) * v
        outs.append((yn + bonus) * g_all[:, sl])
    o_ref[pl.ds(b, 1), :] = jnp.concatenate(outs, axis=1)


def _rwkv_sample(proj, lp, s0, shift0):
    nb = proj.shape[0]
    row = lambda x: x.reshape(1, -1)
    prep = pl.pallas_call(
        _rwkv_prep_sample_body,
        grid=(1,),
        in_specs=[pl.BlockSpec((nb, RWKV_PROJ), lambda i: (0, COL_RW // RWKV_PROJ)), pl.BlockSpec((nb, RWKV_PROJ), lambda i: (0, 0)),
                  pl.BlockSpec((1, RWKV_PROJ), lambda i: (0, 0)), pl.BlockSpec((1, RWKV_W), lambda i: (0, 0)),
                  pl.BlockSpec((RWKV_W_RANK, RWKV_W), lambda i: (0, 0)), pl.BlockSpec((1, RWKV_W), lambda i: (0, 0)),
                  pl.BlockSpec((RWKV_A_RANK, RWKV_W), lambda i: (0, 0)), pl.BlockSpec((RWKV_G_RANK, RWKV_W), lambda i: (0, 0)),
                  pl.BlockSpec((1, RWKV_W), lambda i: (0, 0)), pl.BlockSpec((1, RWKV_W), lambda i: (0, 0))],
        out_specs=[pl.BlockSpec((nb, RWKV_W), lambda i: (0, 0))] * 7,
        out_shape=[jax.ShapeDtypeStruct((nb, RWKV_W), F32)] * 7,
        compiler_params=_cparams("arbitrary"),
        name="rwkv_prep_sample",
    )(proj, shift0, row(lp['mu']), row(lp['w0']), lp['w2'], row(lp['a0']), lp['a2'], lp['g2'], row(lp['kk']), row(lp['ka']))
    act = pl.BlockSpec((nb, RWKV_W), lambda b: (0, 0))
    par = pl.BlockSpec((1, RWKV_W), lambda b: (0, 0))
    st = pl.BlockSpec((1, RWKV_HEADS, RWKV_HD, RWKV_HD), lambda b: (b, 0, 0, 0))
    return pl.pallas_call(
        _rwkv_step_body,
        grid=(nb,),
        in_specs=[act] * 7 + [par] * 3 + [st],
        out_specs=[act, st],
        out_shape=[jax.ShapeDtypeStruct((nb, RWKV_W), F32), jax.ShapeDtypeStruct(s0.shape, F32)],
        compiler_params=_cparams("arbitrary"),
        name="rwkv_step",
    )(*prep, row(lp['rk']), row(lp['ln_w']), row(lp['ln_b']), s0)


def _fox_decode_body(pt_ref, q_ref, kn_ref, vn_ref, bias_ref, *refs, npg):
    del pt_ref
    k_refs, v_refs = refs[:npg], refs[npg:2 * npg]
    o_ref, m_sc, l_sc, acc_sc = refs[2 * npg:]
    j = pl.program_id(1)
    scale = FOX_HD ** -0.5
    q = q_ref[0]

    @pl.when(j == 0)
    def _():
        m_sc[...] = jnp.sum(q * kn_ref[0], axis=-1, keepdims=True) * scale
        l_sc[...] = jnp.ones(l_sc.shape, F32)
        acc_sc[...] = vn_ref[0]

    qb = q.astype(BF16)
    for g in range(npg):
        rows = k_refs[g].shape[2] * FOX_HEADS
        kf = k_refs[g][0, 0].reshape(rows, FOX_HD).astype(BF16)
        vf = v_refs[g][0, 0].reshape(rows, FOX_HD).astype(BF16)
        s = lax.dot_general(qb, kf, _NT, preferred_element_type=F32) * scale + bias_ref[0, g]
        m_old = m_sc[...]
        m_new = jnp.maximum(m_old, jnp.max(s, axis=-1, keepdims=True))
        alpha = jnp.exp(m_old - m_new)
        p = jnp.exp(s - m_new)
        l_sc[...] = alpha * l_sc[...] + jnp.sum(p, axis=-1, keepdims=True)
        acc_sc[...] = alpha * acc_sc[...] + jnp.dot(p.astype(BF16), vf, preferred_element_type=F32)
        m_sc[...] = m_new

    @pl.when(j == pl.num_programs(1) - 1)
    def _():
        o_ref[0] = acc_sc[...] / l_sc[...]


def _fox_decode(q, k_new, v_new, logf_new, cache_k, cache_v, cache_logf, page_table, layer):
    nb, n_pages = page_table.shape
    page = cache_k.shape[2]
    npg = _pick(n_pages, (4, 2, 1))
    plogf = cache_logf[layer][page_table].astype(F32).reshape(nb, n_pages * page, FOX_HEADS)
    dsuf = lax.cumsum(plogf, axis=1, reverse=True) - plogf
    bias = (dsuf + logf_new[:, None, :]).reshape(nb, n_pages, page, FOX_HEADS)
    own = jnp.eye(FOX_HEADS, dtype=bool)[None, None, :, None, :]
    bias = jnp.where(own, bias[:, :, None, :, :], NEG_BIG).reshape(nb, n_pages, FOX_HEADS, page * FOX_HEADS)
    tok = pl.BlockSpec((1, FOX_HEADS, FOX_HD), lambda b, j, pt: (b, 0, 0))
    kv = lambda g: pl.BlockSpec((1, 1, page, FOX_HEADS, FOX_HD), lambda b, j, pt, g=g: (layer, pt[b, j * npg + g], 0, 0, 0))
    grid_spec = pltpu.PrefetchScalarGridSpec(
        num_scalar_prefetch=1,
        grid=(nb, n_pages // npg),
        in_specs=[tok, tok, tok, pl.BlockSpec((1, npg, FOX_HEADS, page * FOX_HEADS), lambda b, j, pt: (b, j, 0, 0))]
                 + [kv(g) for g in range(npg)] * 2,
        out_specs=tok,
        scratch_shapes=[pltpu.VMEM((FOX_HEADS, 1), F32), pltpu.VMEM((FOX_HEADS, 1), F32), pltpu.VMEM((FOX_HEADS, FOX_HD), F32)],
    )
    r3 = lambda x: x.reshape(nb, FOX_HEADS, FOX_HD)
    o = pl.pallas_call(
        functools.partial(_fox_decode_body, npg=npg),
        grid_spec=grid_spec,
        out_shape=jax.ShapeDtypeStruct((nb, FOX_HEADS, FOX_HD), F32),
        compiler_params=_cparams("parallel", "arbitrary"),
        name="fox_decode",
    )(page_table, r3(q), r3(k_new), r3(v_new), bias, *([cache_k] * npg), *([cache_v] * npg))
    return o.reshape(nb, FOX_W)


def _moe_sample_body(e_ref, h_ref, w1_ref, w3_ref, w2_ref, wv_ref, o_ref):
    del e_ref
    m = o_ref.shape[0]

    @pl.when(pl.program_id(0) == 0)
    def _():
        o_ref[...] = jnp.zeros(o_ref.shape, F32)

    fold = lambda y: y[:m] + y[m:]
    h2 = h_ref[...]
    a = fold(jnp.dot(h2, w1_ref[0], preferred_element_type=F32))
    b = fold(jnp.dot(h2, w3_ref[0], preferred_element_type=F32))
    mid = _silu(a) * b
    hi = mid.astype(BF16).astype(F32)
    mid2 = jnp.concatenate([hi, mid - hi], axis=0).astype(BF16)
    y = fold(jnp.dot(mid2, w2_ref[0], preferred_element_type=F32))
    o_ref[...] += wv_ref[0][:, 0:1] * y


def _moe_sample(h, eid, gate, w1, w3, w2):
    m, d = h.shape
    na = m * TOP_K
    order = jnp.argsort(eid.reshape(-1))
    e_sorted = eid.reshape(-1)[order].astype(jnp.int32)
    wv = jnp.zeros((na, m), F32).at[jnp.arange(na), order // TOP_K].set(gate.reshape(-1)[order])
    wv = jnp.broadcast_to(wv[:, :, None], (na, m, 128))
    ff = w1.shape[2]
    grid_spec = pltpu.PrefetchScalarGridSpec(
        num_scalar_prefetch=1,
        grid=(na,),
        in_specs=[pl.BlockSpec((2 * m, d), lambda s, e: (0, 0)),
                  pl.BlockSpec((1, d, ff), lambda s, e: (e[s], 0, 0)),
                  pl.BlockSpec((1, d, ff), lambda s, e: (e[s], 0, 0)),
                  pl.BlockSpec((1, ff, d), lambda s, e: (e[s], 0, 0)),
                  pl.BlockSpec((1, m, 128), lambda s, e: (s, 0, 0))],
        out_specs=pl.BlockSpec((m, d), lambda s, e: (0, 0)),
    )
    return pl.pallas_call(
        _moe_sample_body,
        grid_spec=grid_spec,
        out_shape=jax.ShapeDtypeStruct((m, d), F32),
        compiler_params=_cparams("arbitrary"),
        name="moe_sample",
    )(e_sorted, _split_hi_lo(h), w1, w3, w2, wv)


def _sample_layer(x, lp, moe, layer, cache_k, cache_v, cache_logf, page_table, s_ret, s_rwkv, s_shift, pos):
    nb = x.shape[0]
    proj = _mm_sample(_rmsnorm(x, lp['norm_mix'], F32), lp['w_in'])
    ret_o, ret_s = _retention_sample(proj, s_ret, pos)
    logf = jax.nn.log_sigmoid(proj[:, COL_FF:COL_FF + FOX_HEADS] + lp['fox_b'].astype(F32))
    fk, fv = proj[:, COL_FK:COL_FK + FOX_W], proj[:, COL_FV:COL_FV + FOX_W]
    fox_o = _fox_decode(proj[:, COL_FQ:COL_FQ + FOX_W], fk, fv, logf, cache_k, cache_v, cache_logf, page_table, layer)
    rw_o, rw_s = _rwkv_sample(proj, lp, s_rwkv, s_shift)
    g2 = jnp.concatenate([proj[:, COL_GATE:]] * 2, axis=0)
    m2 = _merge(_split_hi_lo(ret_o), _split_hi_lo(fox_o), _split_hi_lo(rw_o), lp['wb_ret'], lp['wb_fox'], lp['wb_rwkv'], g2, 0, F32)
    x = x + _mm_sample(m2[:nb] + m2[nb:], lp['w_out'])
    h2, logits = _rmsnorm_router(x, lp['norm_ffn'], moe['wr'])
    eid, gate = _route(logits, moe['bg'], moe['be'])
    x = x + _moe_sample(h2, eid, gate, moe['w1'], moe['w3'], moe['w2'])
    heads = lambda a: a.reshape(nb, 1, FOX_HEADS, FOX_HD)
    state = (heads(fk), heads(fv), logf.reshape(nb, 1, FOX_HEADS), ret_s, rw_s, proj[:, COL_RW:COL_RW + RWKV_PROJ])
    return x, state


def kernel(x_prompt, x_sample, cache_k, cache_v, cache_logf, page_table, state_ret, state_rwkv, state_shift,
           meta_tokens, norm_mix, norm_ffn, norm_final, w_in, fox_forget_bias,
           rwkv_mu, rwkv_w0, rwkv_w2, rwkv_a0, rwkv_a2, rwkv_g2, rwkv_kk, rwkv_ka, rwkv_rk, rwkv_ln_w, rwkv_ln_b,
           w_branch_ret, w_branch_fox, w_branch_rwkv, w_out,
           router_group_w, router_group_b, router_expert_w, router_expert_b, expert_w1, expert_w3, expert_w2):
    batch, s_len, d = x_prompt.shape
    nb, n_new, _ = x_sample.shape
    assert n_new == 1 and d == D_MODEL and s_len % CHUNK == 0
    depth = w_in.shape[0]
    tp = PAD_FRONT + N_META + s_len
    past_len = page_table.shape[1] * cache_k.shape[2]
    valid = np.tile(np.arange(tp) >= PAD_FRONT, batch)
    xp = jnp.concatenate([jnp.zeros((batch, PAD_FRONT, d), F32),
                          jnp.broadcast_to(meta_tokens[None].astype(F32), (batch, N_META, d)), x_prompt], axis=1)
    xp = xp.reshape(batch * tp, d)
    xs = x_sample.reshape(nb, d)
    pos_s = jnp.full((1,), past_len, jnp.int32)
    outs_p = [[] for _ in range(6)]
    outs_s = [[] for _ in range(6)]
    for l in range(depth):
        lp = dict(norm_mix=norm_mix[l], norm_ffn=norm_ffn[l], w_in=_pack_w_in(w_in[l]), fox_b=fox_forget_bias[l],
                  mu=rwkv_mu[l], w0=rwkv_w0[l], w2=rwkv_w2[l], a0=rwkv_a0[l], a2=rwkv_a2[l], g2=rwkv_g2[l],
                  kk=rwkv_kk[l], ka=rwkv_ka[l], rk=rwkv_rk[l], ln_w=rwkv_ln_w[l], ln_b=rwkv_ln_b[l],
                  wb_ret=w_branch_ret[l].astype(BF16), wb_fox=w_branch_fox[l].astype(BF16),
                  wb_rwkv=w_branch_rwkv[l].astype(BF16), w_out=w_out[l].astype(BF16))
        moe = dict(wr=_router_weights(router_group_w[l], router_expert_w[l]), bg=router_group_b[l], be=router_expert_b[l],
                   w1=expert_w1[l].astype(BF16), w3=expert_w3[l].astype(BF16), w2=expert_w2[l].astype(BF16))
        xp, st = _prompt_layer(xp, lp, moe, batch, tp, valid)
        for j in range(6):
            outs_p[j].append(st[j])
        xs, st = _sample_layer(xs, lp, moe, l, cache_k, cache_v, cache_logf, page_table,
                               state_ret[l], state_rwkv[l], state_shift[l], pos_s)
        for j in range(6):
            outs_s[j].append(st[j])
    y_prompt = _final_norm_prompt(xp, norm_final, batch, tp)
    y_sample = _rmsnorm(xs, norm_final, F32).reshape(nb, 1, d)
    return (y_prompt, y_sample, *[jnp.stack(o, axis=0) for o in outs_p], *[jnp.stack(o, axis=0) for o in outs_s])
```

```python
import functools

import numpy as np
import jax
import jax.numpy as jnp
from jax import lax
from jax.experimental import pallas as pl
from jax.experimental.pallas import tpu as pltpu

F32 = jnp.float32
BF16 = jnp.bfloat16
HIGHEST = lax.Precision.HIGHEST

D_MODEL = 2048
N_META = 16
CHUNK = 128
PAD_FRONT = CHUNK - N_META
RMS_EPS = 1e-6
GN_EPS = 1e-5
RET_HEADS = 4
RET_DK = 256
RET_DV = 256
RET_W = RET_HEADS * RET_DK
ROPE_BASE = 10000.0
FOX_HEADS = 8
FOX_HD = 128
FOX_W = FOX_HEADS * FOX_HD
FOX_PAIR = 2
LOG2E = 1.4426950408889634
RWKV_HEADS = 16
RWKV_HD = 64
RWKV_W = RWKV_HEADS * RWKV_HD
RWKV_W_RANK = 64
RWKV_A_RANK = 64
RWKV_G_RANK = 128
RWKV_GN_EPS = 64e-5
RWKV_PROJ = 3 * RWKV_W + RWKV_W_RANK + RWKV_A_RANK + RWKV_G_RANK
RWKV_CHUNK = 64
RWKV_GROUP = 4
N_BRANCH = 3
N_GROUPS = 4
EXPERTS_PER_GROUP = 8
N_EXPERTS = N_GROUPS * EXPERTS_PER_GROUP
TOP_K = 2
EXPERT_FF = 1024
MOE_BLOCK = 128
ROUTER_COLS = 128

FF_PAD = 256
COL_RW = 0
COL_RQ = COL_RW + RWKV_PROJ
COL_RK = COL_RQ + RET_W
COL_RV = COL_RK + RET_W
COL_RG = COL_RV + RET_W
COL_FQ = COL_RG + RET_W
COL_FK = COL_FQ + FOX_W
COL_FV = COL_FK + FOX_W
COL_FF = COL_FV + FOX_W
COL_GATE = COL_FF + FF_PAD
PROJ_PACKED = COL_GATE + N_BRANCH * D_MODEL
SRC_RQ = 0
SRC_FF = 4 * RET_W + 3 * FOX_W
SRC_RW = SRC_FF + FOX_HEADS
SRC_GATE = SRC_RW + RWKV_PROJ

VMEM_LIMIT = 56 * 1024 * 1024
NEG_BIG = -1e30
_NT = (((1,), (1,)), ((), ()))
_TN = (((0,), (0,)), ((), ()))


def _cparams(*sem):
    return pltpu.CompilerParams(dimension_semantics=sem, vmem_limit_bytes=VMEM_LIMIT)


def _pick(n, prefs):
    for p in prefs:
        if n % p == 0:
            return p
    return n


def _rms(x, g):
    return x * lax.rsqrt(jnp.mean(x * x, axis=-1, keepdims=True) + RMS_EPS) * g


def _rms_body(x_ref, g_ref, o_ref):
    o_ref[...] = _rms(x_ref[...], g_ref[...]).astype(o_ref.dtype)


def _rmsnorm(x, g, out_dtype):
    n, d = x.shape
    tm = _pick(n, (256, 128, 8))
    return pl.pallas_call(
        _rms_body,
        grid=(n // tm,),
        in_specs=[pl.BlockSpec((tm, d), lambda i: (i, 0)), pl.BlockSpec((1, d), lambda i: (0, 0))],
        out_specs=pl.BlockSpec((tm, d), lambda i: (i, 0)),
        out_shape=jax.ShapeDtypeStruct((n, d), out_dtype),
        compiler_params=_cparams("parallel"),
        name="rmsnorm",
    )(x, g.reshape(1, d))


def _rms_router_body(x_ref, g_ref, wr_ref, h_ref, lg_ref):
    h = _rms(x_ref[...], g_ref[...])
    h_ref[...] = h
    lg_ref[...] = jnp.dot(h.astype(BF16), wr_ref[...].astype(BF16), preferred_element_type=F32)


def _rmsnorm_router(x, g, wr):
    n, d = x.shape
    tm = _pick(n, (256, 128, 8))
    return pl.pallas_call(
        _rms_router_body,
        grid=(n // tm,),
        in_specs=[pl.BlockSpec((tm, d), lambda i: (i, 0)), pl.BlockSpec((1, d), lambda i: (0, 0)),
                  pl.BlockSpec((d, ROUTER_COLS), lambda i: (0, 0))],
        out_specs=[pl.BlockSpec((tm, d), lambda i: (i, 0)), pl.BlockSpec((tm, ROUTER_COLS), lambda i: (i, 0))],
        out_shape=[jax.ShapeDtypeStruct((n, d), F32), jax.ShapeDtypeStruct((n, ROUTER_COLS), F32)],
        compiler_params=_cparams("parallel"),
        name="rmsnorm_router",
    )(x, g.reshape(1, d), wr)


def _final_norm_prompt(x, g, batch, tp):
    d = x.shape[1]
    nb = tp // CHUNK
    return pl.pallas_call(
        _rms_body,
        grid=(batch, nb - 1),
        in_specs=[pl.BlockSpec((CHUNK, d), lambda b, j: (b * nb + 1 + j, 0)), pl.BlockSpec((1, d), lambda b, j: (0, 0))],
        out_specs=pl.BlockSpec((CHUNK, d), lambda b, j: (b * (nb - 1) + j, 0)),
        out_shape=jax.ShapeDtypeStruct((batch * (tp - CHUNK), d), F32),
        compiler_params=_cparams("parallel", "parallel"),
        name="final_norm",
    )(x, g.reshape(1, d)).reshape(batch, tp - CHUNK, d)


def _mm_body(a_ref, w_ref, o_ref):
    o_ref[...] = jnp.dot(a_ref[...], w_ref[...], preferred_element_type=F32).astype(o_ref.dtype)


def _matmul(a, w, out_dtype=F32, tm_prefs=(1408, 768, 512, 384, 256, 128), tn_prefs=(512, 256, 128)):
    m, k = a.shape
    n = w.shape[1]
    tm = _pick(m, tm_prefs)
    tn = _pick(n, tn_prefs)
    return pl.pallas_call(
        _mm_body,
        grid=(m // tm, n // tn),
        in_specs=[pl.BlockSpec((tm, k), lambda i, j: (i, 0)), pl.BlockSpec((k, tn), lambda i, j: (0, j))],
        out_specs=pl.BlockSpec((tm, tn), lambda i, j: (i, j)),
        out_shape=jax.ShapeDtypeStruct((m, n), out_dtype),
        compiler_params=_cparams("parallel", "arbitrary"),
        name="matmul",
    )(a, w)


def _mm_res_body(a_ref, w_ref, r_ref, o_ref, *, blocks_per_seq, pad):
    y = r_ref[...] + jnp.dot(a_ref[...], w_ref[...], preferred_element_type=F32)
    if pad:
        first = (pl.program_id(0) % blocks_per_seq) == 0
        row = lax.broadcasted_iota(jnp.int32, y.shape, 0)
        y = jnp.where(jnp.logical_and(first, row < pad), 0.0, y)
    o_ref[...] = y


def _matmul_residual(a, w, res, tp=None):
    m, k = a.shape
    n = w.shape[1]
    tm = _pick(tp, (768, 384, 128)) if tp else m
    assert m % tm == 0
    tn = _pick(n, (512, 256, 128))
    body = functools.partial(_mm_res_body, blocks_per_seq=(tp // tm if tp else 1), pad=(PAD_FRONT if tp else 0))
    return pl.pallas_call(
        body,
        grid=(m // tm, n // tn),
        in_specs=[pl.BlockSpec((tm, k), lambda i, j: (i, 0)), pl.BlockSpec((k, tn), lambda i, j: (0, j)),
                  pl.BlockSpec((tm, tn), lambda i, j: (i, j))],
        out_specs=pl.BlockSpec((tm, tn), lambda i, j: (i, j)),
        out_shape=jax.ShapeDtypeStruct((m, n), F32),
        compiler_params=_cparams("parallel", "arbitrary"),
        name="matmul_residual",
    )(a, w, res)


def _rope_halves(x, cos, sin):
    half = x.shape[-1] // 2
    x1, x2 = x[:, :half], x[:, half:]
    return jnp.concatenate([x1 * cos - x2 * sin, x1 * sin + x2 * cos], axis=-1)


def _head_norm(y, eps):
    mu = jnp.mean(y, axis=-1, keepdims=True)
    yc = y - mu
    return yc * lax.rsqrt(jnp.mean(yc * yc, axis=-1, keepdims=True) + eps)


def _silu(x):
    return x / (1.0 + jnp.exp(-x))


def _sigmoid(x):
    return 1.0 / (1.0 + jnp.exp(-x))


def _ret_body(q_ref, k_ref, v_ref, g_ref, cos_ref, sin_ref, dm_ref, cd_ref, kd_ref, sd_ref, o_ref, s_ref):
    c = pl.program_id(2)

    @pl.when(c == 0)
    def _():
        s_ref[...] = jnp.zeros(s_ref.shape, F32)

    cos = cos_ref[...]
    sin = sin_ref[...]
    q = _rope_halves(q_ref[...], cos, sin)
    k = _rope_halves(k_ref[...], cos, sin) * (RET_DK ** -0.5)
    qb = q.astype(BF16)
    kb = k.astype(BF16)
    vb = v_ref[...].astype(BF16)
    s_old = s_ref[0, 0]
    scores = lax.dot_general(qb, kb, (((1,), (1,)), ((), ())), preferred_element_type=F32) * dm_ref[0]
    intra = jnp.dot(scores.astype(BF16), vb, preferred_element_type=F32)
    cross = jnp.dot(qb, s_old.astype(BF16), preferred_element_type=F32) * cd_ref[0]
    kdec = (k * kd_ref[0]).astype(BF16)
    s_ref[0, 0] = sd_ref[0] * s_old + lax.dot_general(kdec, vb, (((0,), (0,)), ((), ())), preferred_element_type=F32)
    o = _head_norm(intra + cross, GN_EPS) * _silu(g_ref[...])
    o_ref[...] = o.astype(o_ref.dtype)


def _ret_tables(length):
    lg = jnp.log1p(-jnp.power(2.0, -5.0 - jnp.arange(RET_HEADS, dtype=F32)))
    i = jnp.arange(length, dtype=F32)
    diff = i[:, None] - i[None, :]
    dmask = jnp.where(diff >= 0, jnp.exp(lg[:, None, None] * jnp.maximum(diff, 0.0)), 0.0)
    cdec = jnp.exp(lg[:, None] * (i + 1.0)[None, :])
    kdec = jnp.exp(lg[:, None] * (length - 1.0 - i)[None, :])
    sdec = jnp.exp(lg * length)
    return lg, dmask, cdec, kdec, sdec


def _rope_tables(pos):
    half = RET_DK // 2
    inv = ROPE_BASE ** (-jnp.arange(half, dtype=F32) / half)
    ang = pos.astype(F32)[:, None] * inv[None, :]
    return jnp.cos(ang), jnp.sin(ang)


def _retention_prompt(proj, batch, tp):
    n = proj.shape[0]
    nc = tp // CHUNK
    _, dmask, cdec, kdec, sdec = _ret_tables(CHUNK)
    cdec = jnp.broadcast_to(cdec[:, :, None], (RET_HEADS, CHUNK, RET_DV))
    kdec = jnp.broadcast_to(kdec[:, :, None], (RET_HEADS, CHUNK, RET_DK))
    sdec = jnp.broadcast_to(sdec[:, None, None], (RET_HEADS, 1, RET_DV))
    cos, sin = _rope_tables(jnp.arange(tp, dtype=jnp.int32) - PAD_FRONT)
    blk = lambda col: pl.BlockSpec((CHUNK, RET_DK), lambda b, h, c, col=col: (b * nc + c, col // RET_DK + h))
    tab = lambda shape: pl.BlockSpec((1,) + shape, lambda b, h, c: (h, 0, 0))
    rope = pl.BlockSpec((CHUNK, RET_DK // 2), lambda b, h, c: (c, 0))
    return pl.pallas_call(
        _ret_body,
        grid=(batch, RET_HEADS, nc),
        in_specs=[blk(COL_RQ), blk(COL_RK), blk(COL_RV), blk(COL_RG), rope, rope,
                  tab((CHUNK, CHUNK)), tab((CHUNK, RET_DV)), tab((CHUNK, RET_DK)), tab((1, RET_DV))],
        out_specs=[pl.BlockSpec((CHUNK, RET_DV), lambda b, h, c: (b * nc + c, h)),
                   pl.BlockSpec((1, 1, RET_DK, RET_DV), lambda b, h, c: (b, h, 0, 0))],
        out_shape=[jax.ShapeDtypeStruct((n, RET_W), BF16), jax.ShapeDtypeStruct((batch, RET_HEADS, RET_DK, RET_DV), F32)],
        compiler_params=_cparams("parallel", "parallel", "arbitrary"),
        name="retention_prompt",
    )(proj, proj, proj, proj, cos, sin, dmask, cdec, kdec, sdec)


def _fox_body(qi_ref, ki_ref, q_ref, k_ref, v_ref, qx_ref, kx_ref, o_ref, m_sc, l_sc, acc_sc, *, tq, tk, hp):
    step = pl.program_id(2)
    qi = qi_ref[step]
    ki = ki_ref[step]

    @pl.when(ki == 0)
    def _():
        m_sc[...] = jnp.full(m_sc.shape, NEG_BIG, F32)
        l_sc[...] = jnp.zeros(l_sc.shape, F32)
        acc_sc[...] = jnp.zeros(acc_sc.shape, F32)

    def update(masked):
        if masked:
            qpos = qi * tq + lax.broadcasted_iota(jnp.int32, (tq, tk), 0)
            kpos = ki * tk + lax.broadcasted_iota(jnp.int32, (tq, tk), 1)
            valid = jnp.logical_and(kpos <= qpos, kpos >= PAD_FRONT)
        m_old, l_old, acc_old = m_sc[...], l_sc[...], acc_sc[...]
        m_out, l_out, acc_out = [], [], []
        for h in range(hp):
            sl = slice(h * FOX_HD, (h + 1) * FOX_HD)
            qa = jnp.concatenate([q_ref[:, sl].astype(BF16), qx_ref[0, h]], axis=1)
            ka = jnp.concatenate([k_ref[:, sl].astype(BF16), kx_ref[0, h]], axis=1)
            s = lax.dot_general(qa, ka, _NT, preferred_element_type=F32) * (FOX_HD ** -0.5 * LOG2E)
            if masked:
                s = jnp.where(valid, s, NEG_BIG)
            m_new = jnp.maximum(m_old[h], jnp.max(s, axis=1, keepdims=True))
            alpha = jnp.exp2(m_old[h] - m_new)
            p = jnp.exp2(s - m_new)
            m_out.append(m_new)
            l_out.append(alpha * l_old[h] + jnp.sum(p, axis=1, keepdims=True))
            acc_out.append(alpha * acc_old[:, sl] + jnp.dot(p.astype(BF16), v_ref[:, sl].astype(BF16),
                                                            preferred_element_type=F32))
        for h in range(hp):
            m_sc[h] = m_out[h]
            l_sc[h] = l_out[h]
            acc_sc[:, h * FOX_HD:(h + 1) * FOX_HD] = acc_out[h]

    edge = jnp.logical_or(ki == qi, ki == 0)
    pl.when(edge)(functools.partial(update, True))
    pl.when(jnp.logical_not(edge))(functools.partial(update, False))

    @pl.when(ki == qi)
    def _():
        for h in range(hp):
            sl = slice(h * FOX_HD, (h + 1) * FOX_HD)
            o_ref[:, sl] = (acc_sc[:, sl] / l_sc[h]).astype(o_ref.dtype)


def _split3(x):
    hi = x.astype(BF16)
    r1 = x - hi.astype(F32)
    mid = r1.astype(BF16)
    return hi, mid, (r1 - mid.astype(F32)).astype(BF16)


def _fox_prompt(proj, c, batch, tp):
    n = proj.shape[0]
    tq = _pick(tp, (384, 256, 128))
    nq = tp // tq
    hp = FOX_PAIR
    pairs = [(i, j) for i in range(nq) for j in range(i + 1)]
    qi_tab = jnp.asarray(np.array([p[0] for p in pairs], np.int32))
    ki_tab = jnp.asarray(np.array([p[1] for p in pairs], np.int32))
    hi, mid, lo = _split3(c * (FOX_HD ** 0.5))
    one = jnp.ones_like(hi)
    fill = jnp.zeros(c.shape + (FOX_HD - 6,), BF16)
    qx = jnp.concatenate([jnp.stack([hi, mid, lo, one, one, one], axis=-1), fill], axis=-1)
    kx = jnp.concatenate([jnp.stack([one, one, one, -hi, -mid, -lo], axis=-1), fill], axis=-1)
    cb = lambda col: col // (hp * FOX_HD)
    wide = hp * FOX_HD
    grid_spec = pltpu.PrefetchScalarGridSpec(
        num_scalar_prefetch=2,
        grid=(batch, FOX_HEADS // hp, len(pairs)),
        in_specs=[
            pl.BlockSpec((tq, wide), lambda b, h, s, qi, ki: (b * nq + qi[s], cb(COL_FQ) + h)),
            pl.BlockSpec((tq, wide), lambda b, h, s, qi, ki: (b * nq + ki[s], cb(COL_FK) + h)),
            pl.BlockSpec((tq, wide), lambda b, h, s, qi, ki: (b * nq + ki[s], cb(COL_FV) + h)),
            pl.BlockSpec((1, hp, tq, FOX_HD), lambda b, h, s, qi, ki: (b, h, qi[s], 0)),
            pl.BlockSpec((1, hp, tq, FOX_HD), lambda b, h, s, qi, ki: (b, h, ki[s], 0)),
        ],
        out_specs=pl.BlockSpec((tq, wide), lambda b, h, s, qi, ki: (b * nq + qi[s], h)),
        scratch_shapes=[pltpu.VMEM((hp, tq, 1), F32), pltpu.VMEM((hp, tq, 1), F32), pltpu.VMEM((tq, wide), F32)],
    )
    return pl.pallas_call(
        functools.partial(_fox_body, tq=tq, tk=tq, hp=hp),
        grid_spec=grid_spec,
        out_shape=jax.ShapeDtypeStruct((n, FOX_W), BF16),
        compiler_params=_cparams("parallel", "parallel", "arbitrary"),
        name="fox_prompt",
    )(qi_tab, ki_tab, proj, proj, proj, qx, kx)


def _softplus(z):
    return jnp.maximum(z, 0.0) + jnp.log1p(jnp.exp(-jnp.abs(z)))


def _rwkv_prep_math(c, prev, mu, w0, w2, a0, a2, g2, kkp, ka, exact):
    w = RWKV_W
    xm = c + mu * (prev - c)
    r, k, v = xm[:, 0:w], xm[:, w:2 * w], xm[:, 2 * w:3 * w]
    wd = xm[:, 3 * w:3 * w + RWKV_W_RANK]
    ad = xm[:, 3 * w + RWKV_W_RANK:3 * w + RWKV_W_RANK + RWKV_A_RANK]
    gd = xm[:, 3 * w + RWKV_W_RANK + RWKV_A_RANK:]
    if exact:
        mm = lambda x, m: jnp.dot(x, m, precision=HIGHEST, preferred_element_type=F32)
    else:
        mm = lambda x, m: jnp.dot(x.astype(BF16), m.astype(BF16), preferred_element_type=F32)
    w_log = -_softplus(-(w0 + mm(jnp.tanh(wd), w2))) - 0.5
    lw = -jnp.exp(w_log)
    a = _sigmoid(a0 + mm(ad, a2))
    g = mm(_sigmoid(gd), g2)
    kk0 = k * kkp
    kmod = k * (1.0 + (a - 1.0) * ka)
    return r, kmod, v, lw, kk0, a, g


def _rwkv_prep_body(c_ref, mu_ref, w0_ref, w2_ref, a0_ref, a2_ref, g2_ref, kkp_ref, ka_ref,
                    r_ref, k_ref, v_ref, lw_ref, kk_ref, a_ref, g_ref, carry):
    t = pl.program_id(1)

    @pl.when(t == 0)
    def _():
        carry[...] = jnp.zeros(carry.shape, F32)

    c = c_ref[...]
    rows = c.shape[0]
    prev = pltpu.roll(c, 1, axis=0)
    row = lax.broadcasted_iota(jnp.int32, c.shape, 0)
    prev = jnp.where(row == 0, carry[...], prev)
    carry[...] = c[rows - 1:rows, :]
    outs = _rwkv_prep_math(c, prev, mu_ref[...], w0_ref[...], w2_ref[...], a0_ref[...], a2_ref[...], g2_ref[...],
                           kkp_ref[...], ka_ref[...], exact=False)
    for ref, val in zip((r_ref, k_ref, v_ref, lw_ref, kk_ref, a_ref, g_ref), outs):
        ref[...] = val


def _rwkv_prep_prompt(proj, lp, batch, tp):
    n = proj.shape[0]
    tb = CHUNK
    nt = tp // tb
    row = lambda x: x.reshape(1, -1)
    full = lambda shape: pl.BlockSpec(shape, lambda b, t: (0, 0))
    out_spec = pl.BlockSpec((tb, RWKV_W), lambda b, t: (b * nt + t, 0))
    return pl.pallas_call(
        _rwkv_prep_body,
        grid=(batch, nt),
        in_specs=[pl.BlockSpec((tb, RWKV_PROJ), lambda b, t: (b * nt + t, COL_RW // RWKV_PROJ)),
                  full((1, RWKV_PROJ)), full((1, RWKV_W)), full((RWKV_W_RANK, RWKV_W)), full((1, RWKV_W)),
                  full((RWKV_A_RANK, RWKV_W)), full((RWKV_G_RANK, RWKV_W)), full((1, RWKV_W)), full((1, RWKV_W))],
        out_specs=[out_spec] * 7,
        out_shape=[jax.ShapeDtypeStruct((n, RWKV_W), F32)] * 7,
        scratch_shapes=[pltpu.VMEM((1, RWKV_PROJ), F32)],
        compiler_params=_cparams("parallel", "arbitrary"),
        name="rwkv_prep",
    )(proj, row(lp['mu']), row(lp['w0']), lp['w2'], row(lp['a0']), lp['a2'], lp['g2'], row(lp['kk']), row(lp['ka']))


def _rwkv_chunk_body(r_ref, k_ref, v_ref, lw_ref, kk_ref, a_ref, g_ref, rk_ref, lnw_ref, lnb_ref, o_ref, s_ref):
    cn = pl.program_id(1)

    @pl.when(cn == 0)
    def _():
        s_ref[...] = jnp.zeros(s_ref.shape, F32)

    cs = RWKV_CHUNK
    hd = RWKV_HD
    gh = RWKV_GROUP
    n = gh * cs
    gw = gh * hd
    bits = int(np.log2(cs))
    row = lax.broadcasted_iota(jnp.int32, (n, n), 0)
    col = lax.broadcasted_iota(jnp.int32, (n, n), 1)
    same_head = (row >> bits) == (col >> bits)
    strict = same_head & (col < row)
    incl = same_head & (col <= row)
    eye = (row == col).astype(F32)
    pair_masks = [((row >> (bit + 1)) == (col >> (bit + 1))) & ((row & (1 << bit)) != 0) & ((col & (1 << bit)) == 0)
                  for bit in range(bits)]
    trow = lax.broadcasted_iota(jnp.int32, (cs, cs), 0)
    tcol = lax.broadcasted_iota(jnp.int32, (cs, cs), 1)
    cum_all = jnp.dot((tcol <= trow).astype(F32), lw_ref[...], precision=HIGHEST, preferred_element_type=F32)
    decay_all = jnp.exp(cum_all[cs - 1:cs, :])
    tail_all = jnp.exp(cum_all[cs - 1:cs, :] - cum_all)
    dot = lambda x, y: jnp.dot(x.astype(BF16), y.astype(BF16), preferred_element_type=F32)
    dot_nt = lambda x, y: lax.dot_general(x.astype(BF16), y.astype(BF16), _NT, preferred_element_type=F32)
    dot_tn = lambda x, y: lax.dot_general(x.astype(BF16), y.astype(BF16), _TN, preferred_element_type=F32)
    for g in range(RWKV_HEADS // gh):
        lanes = slice(g * gw, (g + 1) * gw)
        stack = lambda x: jnp.concatenate([x[:, g * gw + h * hd:g * gw + (h + 1) * hd] for h in range(gh)], axis=0)
        rows = lambda x: jnp.concatenate([jnp.broadcast_to(x[:, g * gw + h * hd:g * gw + (h + 1) * hd], (cs, hd))
                                          for h in range(gh)], axis=0)
        lw, cum, tail = stack(lw_ref[...]), stack(cum_all), stack(tail_all)
        kk0 = stack(kk_ref[...])
        kk = kk0 * lax.rsqrt(jnp.sum(kk0 * kk0, axis=-1, keepdims=True) + 1e-12)
        b = kk * stack(a_ref[...])
        r, k, v = stack(r_ref[...]), stack(k_ref[...]), stack(v_ref[...])
        e_in = jnp.exp(cum)
        e_neg = jnp.exp(-cum)
        ar = jnp.concatenate([-kk * jnp.exp(cum - lw), r * e_in], axis=0)
        bk = jnp.concatenate([b * e_neg, k * e_neg], axis=0)
        gram = dot_nt(ar, bk)
        l_ab = jnp.where(strict, gram[:n, :n], 0.0)
        l_ak = jnp.where(strict, gram[:n, n:], 0.0)
        m_rbk = jnp.concatenate([jnp.where(incl, gram[n:, :n], 0.0), jnp.where(incl, gram[n:, n:], 0.0)], axis=1)
        tinv = eye + jnp.where(pair_masks[0], l_ab, 0.0)
        for mask in pair_masks[1:]:
            tinv = tinv + dot(dot(tinv, jnp.where(mask, l_ab, 0.0)), tinv)
        s_old = s_ref[0, g * n:(g + 1) * n, :]
        ars = dot_nt(ar, s_old)
        v_bd = jnp.where(same_head, jnp.concatenate([v_ref[:, lanes]] * gh, axis=0), 0.0)
        u = dot(tinv, jnp.where(same_head, ars[:n], 0.0) + dot(l_ak, v_bd))
        uv = jnp.concatenate([u, v_bd], axis=0)
        y_bd = jnp.where(same_head, ars[n:], 0.0) + dot(m_rbk, uv)
        s_ref[0, g * n:(g + 1) * n, :] = s_old * rows(decay_all) + dot_tn(uv, jnp.concatenate([b * tail, k * tail], axis=0))
        y = sum(y_bd[:, h * hd:(h + 1) * hd] for h in range(gh))
        yn = _head_norm(y, RWKV_GN_EPS) * rows(lnw_ref[...]) + rows(lnb_ref[...])
        bonus = jnp.sum(r * k * rows(rk_ref[...]), axis=-1, keepdims=True) * v
        out = (yn + bonus) * stack(g_ref[...])
        o_ref[:, lanes] = jnp.concatenate([out[h * cs:(h + 1) * cs] for h in range(gh)], axis=1).astype(o_ref.dtype)


def _rwkv_chunk_prompt(prep, lp, batch, tp):
    n = prep[0].shape[0]
    cs = RWKV_CHUNK
    ncn = tp // cs
    row = lambda x: x.reshape(1, -1)
    blk = pl.BlockSpec((cs, RWKV_W), lambda b, c: (b * ncn + c, 0))
    full = pl.BlockSpec((1, RWKV_W), lambda b, c: (0, 0))
    out, state = pl.pallas_call(
        _rwkv_chunk_body,
        grid=(batch, ncn),
        in_specs=[blk] * 7 + [full] * 3,
        out_specs=[blk, pl.BlockSpec((1, RWKV_W, RWKV_HD), lambda b, c: (b, 0, 0))],
        out_shape=[jax.ShapeDtypeStruct((n, RWKV_W), BF16), jax.ShapeDtypeStruct((batch, RWKV_W, RWKV_HD), F32)],
        compiler_params=_cparams("parallel", "arbitrary"),
        name="rwkv_chunk",
    )(*prep, row(lp['rk']), row(lp['ln_w']), row(lp['ln_b']))
    return out, state.reshape(batch, RWKV_HEADS, RWKV_HD, RWKV_HD)


def _merge_body(ro_ref, fo_ref, wo_ref, wr_ref, wf_ref, ww_ref, g0_ref, g1_ref, g2_ref, o_ref):
    dot = functools.partial(jnp.dot, preferred_element_type=F32)
    m = (_sigmoid(g0_ref[...]) * dot(ro_ref[...], wr_ref[...])
         + _sigmoid(g1_ref[...]) * dot(fo_ref[...], wf_ref[...])
         + _sigmoid(g2_ref[...]) * dot(wo_ref[...], ww_ref[...]))
    o_ref[...] = m.astype(o_ref.dtype)


def _merge(ret_o, fox_o, rw_o, wb_ret, wb_fox, wb_rwkv, gates, gate_col, out_dtype):
    m = ret_o.shape[0]
    tm = _pick(m, (768, 384, 128))
    tn = 512
    nj = D_MODEL // tn
    act = lambda width: pl.BlockSpec((tm, width), lambda i, j: (i, 0))
    wgt = lambda width: pl.BlockSpec((width, tn), lambda i, j: (0, j))
    gate = lambda br: pl.BlockSpec((tm, tn), lambda i, j, br=br: (i, gate_col // tn + br * nj + j))
    return pl.pallas_call(
        _merge_body,
        grid=(m // tm, nj),
        in_specs=[act(RET_W), act(FOX_W), act(RWKV_W), wgt(RET_W), wgt(FOX_W), wgt(RWKV_W), gate(0), gate(1), gate(2)],
        out_specs=pl.BlockSpec((tm, tn), lambda i, j: (i, j)),
        out_shape=jax.ShapeDtypeStruct((m, D_MODEL), out_dtype),
        compiler_params=_cparams("parallel", "arbitrary"),
        name="merge",
    )(ret_o, fox_o, rw_o, wb_ret, wb_fox, wb_rwkv, gates, gates, gates)


def _route(logits, bg, be):
    n = logits.shape[0]
    gp = jax.nn.softmax(logits[:, :N_GROUPS] + bg.astype(F32), axis=-1)
    gidx = jnp.argmax(gp, axis=-1)
    pg = jnp.take_along_axis(gp, gidx[:, None], axis=-1)
    el = logits[:, N_GROUPS:N_GROUPS + N_EXPERTS].reshape(n, N_GROUPS, EXPERTS_PER_GROUP) + be.astype(F32)[None]
    el = jnp.take_along_axis(el, gidx[:, None, None], axis=1)[:, 0]
    topv, topi = lax.top_k(jax.nn.softmax(el, axis=-1), TOP_K)
    gate = pg * topv / jnp.sum(topv, axis=-1, keepdims=True)
    eid = (gidx[:, None] * EXPERTS_PER_GROUP + topi).astype(jnp.int32)
    return eid, gate


def _router_weights(wg, we):
    d = wg.shape[0]
    wr = jnp.concatenate([wg, jnp.transpose(we, (1, 0, 2)).reshape(d, N_EXPERTS)], axis=1)
    return jnp.pad(wr, ((0, 0), (0, ROUTER_COLS - wr.shape[1])))


def _moe_body(be_ref, nused_ref, tok_ref, dst_ref, h_hbm, w1_ref, w3_ref, w2_ref, y_in, y_hbm, xbuf, ybuf, sem_in, sem_out):
    del y_in
    i = pl.program_id(0)

    @pl.when(i < nused_ref[0])
    def _():
        base = i * MOE_BLOCK

        def gather(r):
            return pltpu.make_async_copy(h_hbm.at[pl.ds(tok_ref[base + r], 1)], xbuf.at[pl.ds(r, 1)], sem_in)

        def scatter(r):
            return pltpu.make_async_copy(ybuf.at[pl.ds(r, 1)], y_hbm.at[pl.ds(dst_ref[base + r], 1)], sem_out)

        def run(make, wait):
            def step(r, carry):
                if wait:
                    make(r).wait()
                else:
                    make(r).start()
                return carry
            lax.fori_loop(0, MOE_BLOCK, step, 0)

        run(gather, False)
        run(gather, True)
        x = xbuf[...].astype(BF16)
        a = jnp.dot(x, w1_ref[0], preferred_element_type=F32)
        b = jnp.dot(x, w3_ref[0], preferred_element_type=F32)
        ybuf[...] = jnp.dot((_silu(a) * b).astype(BF16), w2_ref[0], preferred_element_type=F32)
        run(scatter, False)
        run(scatter, True)


def _moe_prompt(h, eid, gate, valid, w1, w3, w2):
    n, d = h.shape
    a_tot = n * TOP_K
    n_real = int(np.sum(valid)) * TOP_K
    n_blk = (n_real + N_EXPERTS * (MOE_BLOCK - 1) + MOE_BLOCK - 1) // MOE_BLOCK
    cap = n_blk * MOE_BLOCK
    validf = jnp.repeat(jnp.asarray(valid), TOP_K)
    eflat = eid.reshape(-1)
    onehot = jnp.logical_and(eflat[:, None] == jnp.arange(N_EXPERTS, dtype=jnp.int32)[None, :], validf[:, None]).astype(jnp.int32)
    rank = jnp.sum((jnp.cumsum(onehot, axis=0) - onehot) * onehot, axis=1)
    counts = jnp.sum(onehot, axis=0)
    padded = (counts + MOE_BLOCK - 1) // MOE_BLOCK * MOE_BLOCK
    pad_end = jnp.cumsum(padded)
    pad_start = pad_end - padded
    dest = jnp.where(validf, pad_start[eflat] + rank, cap)
    assign = jnp.arange(a_tot, dtype=jnp.int32)
    zero_row = int(np.argmin(valid))
    tok_buf = jnp.full((cap,), zero_row, jnp.int32).at[dest].set(assign // TOP_K, mode='drop')
    dump = a_tot + (jnp.arange(cap, dtype=jnp.int32) % MOE_BLOCK)
    dst_buf = dump.at[dest].set((assign % TOP_K) * n + assign // TOP_K, mode='drop')
    n_used = (pad_end[-1] // MOE_BLOCK).astype(jnp.int32)
    blk = jnp.arange(n_blk, dtype=jnp.int32)
    blk_e = jnp.minimum(jnp.searchsorted(pad_end, blk * MOE_BLOCK, side='right'), N_EXPERTS - 1).astype(jnp.int32)
    blk_e = jnp.where(blk < n_used, blk_e, blk_e[jnp.maximum(n_used - 1, 0)])
    y_rows = a_tot + MOE_BLOCK
    y0 = jnp.zeros((y_rows, d), F32)
    ff = w1.shape[2]
    grid_spec = pltpu.PrefetchScalarGridSpec(
        num_scalar_prefetch=4,
        grid=(n_blk,),
        in_specs=[pl.BlockSpec(memory_space=pl.ANY),
                  pl.BlockSpec((1, d, ff), lambda i, be, nu, tk, ds: (be[i], 0, 0)),
                  pl.BlockSpec((1, d, ff), lambda i, be, nu, tk, ds: (be[i], 0, 0)),
                  pl.BlockSpec((1, ff, d), lambda i, be, nu, tk, ds: (be[i], 0, 0)),
                  pl.BlockSpec(memory_space=pl.ANY)],
        out_specs=pl.BlockSpec(memory_space=pl.ANY),
        scratch_shapes=[pltpu.VMEM((MOE_BLOCK, d), F32), pltpu.VMEM((MOE_BLOCK, d), F32),
                        pltpu.SemaphoreType.DMA(()), pltpu.SemaphoreType.DMA(())],
    )
    y2 = pl.pallas_call(
        _moe_body,
        grid_spec=grid_spec,
        out_shape=jax.ShapeDtypeStruct((y_rows, d), F32),
        input_output_aliases={8: 0},
        compiler_params=_cparams("arbitrary"),
        name="moe_experts",
    )(blk_e, n_used.reshape(1), tok_buf, dst_buf, h, w1, w3, w2, y0)
    return y2


def _combine_body(x_ref, y0_ref, y1_ref, g_ref, o_ref):
    g = g_ref[...]
    o_ref[...] = x_ref[...] + (y0_ref[...] * g[:, 0:1] + y1_ref[...] * g[:, 1:2])


def _moe_combine(x, y2, gate):
    n, d = x.shape
    tm = _pick(n, (256, 128, 8))
    gpad = jnp.pad(gate, ((0, 0), (0, 128 - TOP_K)))
    return pl.pallas_call(
        _combine_body,
        grid=(n // tm,),
        in_specs=[pl.BlockSpec((tm, d), lambda i: (i, 0)), pl.BlockSpec((tm, d), lambda i: (i, 0)),
                  pl.BlockSpec((tm, d), lambda i: (n // tm + i, 0)), pl.BlockSpec((tm, 128), lambda i: (i, 0))],
        out_specs=pl.BlockSpec((tm, d), lambda i: (i, 0)),
        out_shape=jax.ShapeDtypeStruct((n, d), F32),
        compiler_params=_cparams("parallel"),
        name="moe_combine",
    )(x, y2, y2, gpad)


def _pack_w_in(w_in):
    ff = jnp.pad(w_in[:, SRC_FF:SRC_FF + FOX_HEADS], ((0, 0), (0, FF_PAD - FOX_HEADS)))
    return jnp.concatenate([w_in[:, SRC_RW:SRC_RW + RWKV_PROJ], w_in[:, :SRC_FF], ff, w_in[:, SRC_GATE:]],
                           axis=1).astype(BF16)


def _prompt_layer(x, lp, moe, batch, tp, valid):
    h = _rmsnorm(x, lp['norm_mix'], BF16)
    proj = _matmul(h, lp['w_in'])
    ret_o, ret_s = _retention_prompt(proj, batch, tp)
    ff = proj[:, COL_FF:COL_FF + FOX_HEADS].reshape(batch, tp, FOX_HEADS)
    logf = jax.nn.log_sigmoid(ff + lp['fox_b'].astype(F32))
    c = jnp.cumsum(logf, axis=1).transpose(0, 2, 1)
    fox_o = _fox_prompt(proj, c, batch, tp)
    prep = _rwkv_prep_prompt(proj, lp, batch, tp)
    rw_o, rw_s = _rwkv_chunk_prompt(prep, lp, batch, tp)
    merged = _merge(ret_o, fox_o, rw_o, lp['wb_ret'], lp['wb_fox'], lp['wb_rwkv'], proj, COL_GATE, BF16)
    x = _matmul_residual(merged, lp['w_out'], x, tp=tp)
    h2, logits = _rmsnorm_router(x, lp['norm_ffn'], moe['wr'])
    eid, gate = _route(logits, moe['bg'], moe['be'])
    y2 = _moe_prompt(h2, eid, gate, valid, moe['w1'], moe['w3'], moe['w2'])
    x = _moe_combine(x, y2, gate)
    p3 = proj.reshape(batch, tp, PROJ_PACKED)
    heads = lambda col: p3[:, PAD_FRONT:, col:col + FOX_W].reshape(batch, tp - PAD_FRONT, FOX_HEADS, FOX_HD)
    state = (heads(COL_FK), heads(COL_FV), logf[:, PAD_FRONT:], ret_s, rw_s, p3[:, tp - 1, COL_RW:COL_RW + RWKV_PROJ])
    return x, state


def _rows16(x):
    return jnp.concatenate([x.astype(BF16), jnp.zeros(x.shape, BF16)], axis=0)


def _mm_sample(x, w):
    return _matmul(_rows16(x), w, tn_prefs=(1536, 1024, 512, 256, 128))[:x.shape[0]]


def _pad8(x):
    first = lax.broadcasted_iota(jnp.int32, (8, x.shape[1]), 0) == 0
    return jnp.where(first, jnp.broadcast_to(x, (8, x.shape[1])), 0.0)


def _ret_sample_body(q_ref, k_ref, v_ref, g_ref, cos_ref, sin_ref, dec_ref, s0_ref, o_ref, s_ref):
    b = pl.program_id(1)
    row1 = lambda ref: ref[pl.ds(b, 1), :]
    cos = cos_ref[...]
    sin = sin_ref[...]
    q = _rope_halves(row1(q_ref), cos, sin)
    k = _rope_halves(row1(k_ref), cos, sin) * (RET_DK ** -0.5)
    v = row1(v_ref)
    dec = dec_ref[0]
    s0 = s0_ref[0, 0]
    rnd = lambda x: x.astype(BF16).astype(F32)
    cross = jnp.dot(_pad8(q).astype(BF16), s0.astype(BF16), preferred_element_type=F32)[0:1] * dec
    intra = jnp.sum(rnd(q) * rnd(k), axis=-1, keepdims=True) * v
    s_ref[0, 0] = dec * s0 + lax.dot_general(_pad8(k), _pad8(v), _TN, precision=HIGHEST, preferred_element_type=F32)
    o_ref[pl.ds(b, 1), :] = _head_norm(intra + cross, GN_EPS) * _silu(row1(g_ref))


def _retention_sample(proj, s0, pos):
    nb = proj.shape[0]
    lg = _ret_tables(1)[0]
    dec = jnp.broadcast_to(jnp.exp(lg)[:, None, None], (RET_HEADS, 1, RET_DV))
    cos, sin = _rope_tables(pos)
    blk = lambda col: pl.BlockSpec((nb, RET_DK), lambda h, b, col=col: (0, col // RET_DK + h))
    rope = pl.BlockSpec((1, RET_DK // 2), lambda h, b: (0, 0))
    st = pl.BlockSpec((1, 1, RET_DK, RET_DV), lambda h, b: (b, h, 0, 0))
    return pl.pallas_call(
        _ret_sample_body,
        grid=(RET_HEADS, nb),
        in_specs=[blk(COL_RQ), blk(COL_RK), blk(COL_RV), blk(COL_RG), rope, rope,
                  pl.BlockSpec((1, 1, RET_DV), lambda h, b: (h, 0, 0)), st],
        out_specs=[pl.BlockSpec((nb, RET_DV), lambda h, b: (0, h)), st],
        out_shape=[jax.ShapeDtypeStruct((nb, RET_W), F32), jax.ShapeDtypeStruct(s0.shape, F32)],
        compiler_params=_cparams("parallel", "arbitrary"),
        name="retention_sample",
    )(proj, proj, proj, proj, cos, sin, dec, s0)


def _rwkv_prep_sample_body(c_ref, prev_ref, mu_ref, w0_ref, w2_ref, a0_ref, a2_ref, g2_ref, kkp_ref, ka_ref, *out_refs):
    outs = _rwkv_prep_math(c_ref[...], prev_ref[...], mu_ref[...], w0_ref[...], w2_ref[...], a0_ref[...], a2_ref[...],
                           g2_ref[...], kkp_ref[...], ka_ref[...], exact=False)
    for ref, val in zip(out_refs, outs):
        ref[...] = val


def _rwkv_step_body(r_ref, k_ref, v_ref, lw_ref, kk_ref, a_ref, g_ref, rk_ref, lnw_ref, lnb_ref, s0_ref, o_ref, s_ref):
    b = pl.program_id(0)
    dg = functools.partial(lax.dot_general, precision=HIGHEST, preferred_element_type=F32)
    r_all, k_all, v_all, lw_all, kk_all, a_all, g_all = (
        ref[pl.ds(b, 1), :] for ref in (r_ref, k_ref, v_ref, lw_ref, kk_ref, a_ref, g_ref))
    rk_all, lnw_all, lnb_all = rk_ref[...], lnw_ref[...], lnb_ref[...]
    outs = []
    for h in range(RWKV_HEADS):
        sl = slice(h * RWKV_HD, (h + 1) * RWKV_HD)
        kk0 = kk_all[:, sl]
        kk = kk0 * lax.rsqrt(jnp.sum(kk0 * kk0, axis=-1, keepdims=True) + 1e-12)
        bb = kk * a_all[:, sl]
        w = jnp.exp(lw_all[:, sl])
        r, k, v = r_all[:, sl], k_all[:, sl], v_all[:, sl]
        s0 = s0_ref[0, h]
        sa = lax.dot_general(s0.astype(BF16), _pad8(-kk).astype(BF16), _NT, preferred_element_type=F32)[:, 0:1]
        s_new = s0 * w + sa * bb + dg(_pad8(v), _pad8(k), _TN)
        s_ref[0, h] = s_new
        y = lax.dot_general(_pad8(r).astype(BF16), s_new.astype(BF16), _NT, preferred_element_type=F32)[0:1]
        yn = _head_norm(y, RWKV_GN_EPS) * lnw_all[:, sl] + lnb_all[:, sl]
        bonus = jnp.sum(r * k * rk_all[:, sl], axis=-1, keepdims=True) * v
        outs.append((yn + bonus) * g_all[:, sl])
    o_ref[pl.ds(b, 1), :] = jnp.concatenate(outs, axis=1)


def _rwkv_sample(proj, lp, s0, shift0):
    nb = proj.shape[0]
    row = lambda x: x.reshape(1, -1)
    prep = pl.pallas_call(
        _rwkv_prep_sample_body,
        grid=(1,),
        in_specs=[pl.BlockSpec((nb, RWKV_PROJ), lambda i: (0, COL_RW // RWKV_PROJ)), pl.BlockSpec((nb, RWKV_PROJ), lambda i: (0, 0)),
                  pl.BlockSpec((1, RWKV_PROJ), lambda i: (0, 0)), pl.BlockSpec((1, RWKV_W), lambda i: (0, 0)),
                  pl.BlockSpec((RWKV_W_RANK, RWKV_W), lambda i: (0, 0)), pl.BlockSpec((1, RWKV_W), lambda i: (0, 0)),
                  pl.BlockSpec((RWKV_A_RANK, RWKV_W), lambda i: (0, 0)), pl.BlockSpec((RWKV_G_RANK, RWKV_W), lambda i: (0, 0)),
                  pl.BlockSpec((1, RWKV_W), lambda i: (0, 0)), pl.BlockSpec((1, RWKV_W), lambda i: (0, 0))],
        out_specs=[pl.BlockSpec((nb, RWKV_W), lambda i: (0, 0))] * 7,
        out_shape=[jax.ShapeDtypeStruct((nb, RWKV_W), F32)] * 7,
        compiler_params=_cparams("arbitrary"),
        name="rwkv_prep_sample",
    )(proj, shift0, row(lp['mu']), row(lp['w0']), lp['w2'], row(lp['a0']), lp['a2'], lp['g2'], row(lp['kk']), row(lp['ka']))
    act = pl.BlockSpec((nb, RWKV_W), lambda b: (0, 0))
    par = pl.BlockSpec((1, RWKV_W), lambda b: (0, 0))
    st = pl.BlockSpec((1, RWKV_HEADS, RWKV_HD, RWKV_HD), lambda b: (b, 0, 0, 0))
    return pl.pallas_call(
        _rwkv_step_body,
        grid=(nb,),
        in_specs=[act] * 7 + [par] * 3 + [st],
        out_specs=[act, st],
        out_shape=[jax.ShapeDtypeStruct((nb, RWKV_W), F32), jax.ShapeDtypeStruct(s0.shape, F32)],
        compiler_params=_cparams("arbitrary"),
        name="rwkv_step",
    )(*prep, row(lp['rk']), row(lp['ln_w']), row(lp['ln_b']), s0)


def _fox_decode_body(pt_ref, q_ref, kn_ref, vn_ref, bias_ref, *refs, npg):
    del pt_ref
    k_refs, v_refs = refs[:npg], refs[npg:2 * npg]
    o_ref, s_sc, m_sc, l_sc, acc_sc = refs[2 * npg:]
    phase = pl.program_id(1)
    j = pl.program_id(2)
    last = pl.num_programs(2) - 1
    scale = FOX_HD ** -0.5
    rnd = lambda x: x.astype(BF16).astype(F32)

    @pl.when(jnp.logical_and(phase == 0, j == 0))
    def _():
        m_sc[...] = jnp.sum(rnd(q_ref[0]) * rnd(kn_ref[0]), axis=-1, keepdims=True) * scale
        l_sc[...] = jnp.ones(l_sc.shape, F32)

    @pl.when(phase == 0)
    def _():
        qb = q_ref[0].astype(BF16)
        for g in range(npg):
            rows = k_refs[g].shape[2] * FOX_HEADS
            kf = k_refs[g][0, 0].reshape(rows, FOX_HD).astype(BF16)
            s = lax.dot_general(qb, kf, _NT, preferred_element_type=F32) * scale + bias_ref[0, g]
            s_sc[j * npg + g] = s
            m_old = m_sc[...]
            m_new = jnp.maximum(m_old, jnp.max(s, axis=-1, keepdims=True))
            l_sc[...] = jnp.exp(m_old - m_new) * l_sc[...] + jnp.sum(jnp.exp(s - m_new), axis=-1, keepdims=True)
            m_sc[...] = m_new

    @pl.when(phase == 1)
    def _():
        @pl.when(j == 0)
        def _():
            self_score = jnp.sum(rnd(q_ref[0]) * rnd(kn_ref[0]), axis=-1, keepdims=True) * scale
            acc_sc[...] = rnd(jnp.exp(self_score - m_sc[...]) / l_sc[...]) * rnd(vn_ref[0])

        for g in range(npg):
            rows = v_refs[g].shape[2] * FOX_HEADS
            vf = v_refs[g][0, 0].reshape(rows, FOX_HD).astype(BF16)
            p = jnp.exp(s_sc[j * npg + g] - m_sc[...]) / l_sc[...]
            acc_sc[...] += jnp.dot(p.astype(BF16), vf, preferred_element_type=F32)

        @pl.when(j == last)
        def _():
            o_ref[0] = acc_sc[...]


def _fox_decode(q, k_new, v_new, logf_new, cache_k, cache_v, cache_logf, page_table, layer):
    nb, n_pages = page_table.shape
    page = cache_k.shape[2]
    npg = _pick(n_pages, (8, 4, 2, 1))
    plogf = cache_logf[layer][page_table].astype(F32).reshape(nb, n_pages * page, FOX_HEADS)
    dsuf = lax.cumsum(plogf, axis=1, reverse=True) - plogf
    bias = (dsuf + logf_new[:, None, :]).reshape(nb, n_pages, page, FOX_HEADS)
    own = jnp.eye(FOX_HEADS, dtype=bool)[None, None, :, None, :]
    bias = jnp.where(own, bias[:, :, None, :, :], NEG_BIG).reshape(nb, n_pages, FOX_HEADS, page * FOX_HEADS)
    nst = n_pages // npg
    tok = pl.BlockSpec((1, FOX_HEADS, FOX_HD), lambda b, ph, j, pt: (b, 0, 0))
    k_step = lambda ph, j: j * (1 - ph) + (nst - 1) * ph
    v_step = lambda ph, j: j * ph
    page_spec = lambda step, g: pl.BlockSpec(
        (1, 1, page, FOX_HEADS, FOX_HD), lambda b, ph, j, pt: (layer, pt[b, step(ph, j) * npg + g], 0, 0, 0))
    grid_spec = pltpu.PrefetchScalarGridSpec(
        num_scalar_prefetch=1,
        grid=(nb, 2, nst),
        in_specs=[tok, tok, tok,
                  pl.BlockSpec((1, npg, FOX_HEADS, page * FOX_HEADS), lambda b, ph, j, pt: (b, k_step(ph, j), 0, 0))]
                 + [page_spec(k_step, g) for g in range(npg)] + [page_spec(v_step, g) for g in range(npg)],
        out_specs=tok,
        scratch_shapes=[pltpu.VMEM((n_pages, FOX_HEADS, page * FOX_HEADS), F32), pltpu.VMEM((FOX_HEADS, 1), F32),
                        pltpu.VMEM((FOX_HEADS, 1), F32), pltpu.VMEM((FOX_HEADS, FOX_HD), F32)],
    )
    r3 = lambda x: x.reshape(nb, FOX_HEADS, FOX_HD)
    o = pl.pallas_call(
        functools.partial(_fox_decode_body, npg=npg),
        grid_spec=grid_spec,
        out_shape=jax.ShapeDtypeStruct((nb, FOX_HEADS, FOX_HD), F32),
        compiler_params=_cparams("parallel", "arbitrary", "arbitrary"),
        name="fox_decode",
    )(page_table, r3(q), r3(k_new), r3(v_new), bias, *([cache_k] * npg), *([cache_v] * npg))
    return o.reshape(nb, FOX_W)


def _moe_sample_body(e_ref, h_ref, w1_ref, w3_ref, w2_ref, wv_ref, o_ref):
    del e_ref
    m = o_ref.shape[0]

    @pl.when(pl.program_id(0) == 0)
    def _():
        o_ref[...] = jnp.zeros(o_ref.shape, F32)

    h2 = h_ref[...]
    a = jnp.dot(h2, w1_ref[0], preferred_element_type=F32)
    b = jnp.dot(h2, w3_ref[0], preferred_element_type=F32)
    y = jnp.dot((_silu(a) * b).astype(BF16), w2_ref[0], preferred_element_type=F32)
    o_ref[...] += wv_ref[0][:, 0:1] * y[:m]


def _moe_sample(h, eid, gate, w1, w3, w2):
    m, d = h.shape
    na = m * TOP_K
    order = jnp.argsort(eid.reshape(-1))
    e_sorted = eid.reshape(-1)[order].astype(jnp.int32)
    wv = jnp.zeros((na, m), F32).at[jnp.arange(na), order // TOP_K].set(gate.reshape(-1)[order])
    wv = jnp.broadcast_to(wv[:, :, None], (na, m, 128))
    ff = w1.shape[2]
    grid_spec = pltpu.PrefetchScalarGridSpec(
        num_scalar_prefetch=1,
        grid=(na,),
        in_specs=[pl.BlockSpec((2 * m, d), lambda s, e: (0, 0)),
                  pl.BlockSpec((1, d, ff), lambda s, e: (e[s], 0, 0)),
                  pl.BlockSpec((1, d, ff), lambda s, e: (e[s], 0, 0)),
                  pl.BlockSpec((1, ff, d), lambda s, e: (e[s], 0, 0)),
                  pl.BlockSpec((1, m, 128), lambda s, e: (s, 0, 0))],
        out_specs=pl.BlockSpec((m, d), lambda s, e: (0, 0)),
    )
    return pl.pallas_call(
        _moe_sample_body,
        grid_spec=grid_spec,
        out_shape=jax.ShapeDtypeStruct((m, d), F32),
        compiler_params=_cparams("arbitrary"),
        name="moe_sample",
    )(e_sorted, _rows16(h), w1, w3, w2, wv)


def _sample_layer(x, lp, moe, layer, cache_k, cache_v, cache_logf, page_table, s_ret, s_rwkv, s_shift, pos):
    nb = x.shape[0]
    proj = _mm_sample(_rmsnorm(x, lp['norm_mix'], F32), lp['w_in'])
    ret_o, ret_s = _retention_sample(proj, s_ret, pos)
    logf = jax.nn.log_sigmoid(proj[:, COL_FF:COL_FF + FOX_HEADS] + lp['fox_b'].astype(F32))
    fk, fv = proj[:, COL_FK:COL_FK + FOX_W], proj[:, COL_FV:COL_FV + FOX_W]
    fox_o = _fox_decode(proj[:, COL_FQ:COL_FQ + FOX_W], fk, fv, logf, cache_k, cache_v, cache_logf, page_table, layer)
    rw_o, rw_s = _rwkv_sample(proj, lp, s_rwkv, s_shift)
    g2 = jnp.concatenate([proj[:, COL_GATE:]] * 2, axis=0)
    m2 = _merge(_rows16(ret_o), _rows16(fox_o), _rows16(rw_o), lp['wb_ret'], lp['wb_fox'], lp['wb_rwkv'], g2, 0, F32)
    x = x + _mm_sample(m2[:nb], lp['w_out'])
    h2, logits = _rmsnorm_router(x, lp['norm_ffn'], moe['wr'])
    eid, gate = _route(logits, moe['bg'], moe['be'])
    x = x + _moe_sample(h2, eid, gate, moe['w1'], moe['w3'], moe['w2'])
    heads = lambda a: a.reshape(nb, 1, FOX_HEADS, FOX_HD)
    state = (heads(fk), heads(fv), logf.reshape(nb, 1, FOX_HEADS), ret_s, rw_s, proj[:, COL_RW:COL_RW + RWKV_PROJ])
    return x, state


def kernel(x_prompt, x_sample, cache_k, cache_v, cache_logf, page_table, state_ret, state_rwkv, state_shift,
           meta_tokens, norm_mix, norm_ffn, norm_final, w_in, fox_forget_bias,
           rwkv_mu, rwkv_w0, rwkv_w2, rwkv_a0, rwkv_a2, rwkv_g2, rwkv_kk, rwkv_ka, rwkv_rk, rwkv_ln_w, rwkv_ln_b,
           w_branch_ret, w_branch_fox, w_branch_rwkv, w_out,
           router_group_w, router_group_b, router_expert_w, router_expert_b, expert_w1, expert_w3, expert_w2):
    batch, s_len, d = x_prompt.shape
    nb, n_new, _ = x_sample.shape
    assert n_new == 1 and d == D_MODEL and s_len % CHUNK == 0
    depth = w_in.shape[0]
    tp = PAD_FRONT + N_META + s_len
    past_len = page_table.shape[1] * cache_k.shape[2]
    valid = np.tile(np.arange(tp) >= PAD_FRONT, batch)
    xp = jnp.concatenate([jnp.zeros((batch, PAD_FRONT, d), F32),
                          jnp.broadcast_to(meta_tokens[None].astype(F32), (batch, N_META, d)), x_prompt], axis=1)
    xp = xp.reshape(batch * tp, d)
    xs = x_sample.reshape(nb, d)
    pos_s = jnp.full((1,), past_len, jnp.int32)
    outs_p = [[] for _ in range(6)]
    outs_s = [[] for _ in range(6)]
    for l in range(depth):
        lp = dict(norm_mix=norm_mix[l], norm_ffn=norm_ffn[l], w_in=_pack_w_in(w_in[l]), fox_b=fox_forget_bias[l],
                  mu=rwkv_mu[l], w0=rwkv_w0[l], w2=rwkv_w2[l], a0=rwkv_a0[l], a2=rwkv_a2[l], g2=rwkv_g2[l],
                  kk=rwkv_kk[l], ka=rwkv_ka[l], rk=rwkv_rk[l], ln_w=rwkv_ln_w[l], ln_b=rwkv_ln_b[l],
                  wb_ret=w_branch_ret[l].astype(BF16), wb_fox=w_branch_fox[l].astype(BF16),
                  wb_rwkv=w_branch_rwkv[l].astype(BF16), w_out=w_out[l].astype(BF16))
        moe = dict(wr=_router_weights(router_group_w[l], router_expert_w[l]), bg=router_group_b[l], be=router_expert_b[l],
                   w1=expert_w1[l].astype(BF16), w3=expert_w3[l].astype(BF16), w2=expert_w2[l].astype(BF16))
        xp, st = _prompt_layer(xp, lp, moe, batch, tp, valid)
        for j in range(6):
            outs_p[j].append(st[j])
        xs, st = _sample_layer(xs, lp, moe, l, cache_k, cache_v, cache_logf, page_table,
                               state_ret[l], state_rwkv[l], state_shift[l], pos_s)
        for j in range(6):
            outs_s[j].append(st[j])
    y_prompt = _final_norm_prompt(xp, norm_final, batch, tp)
    y_sample = _rmsnorm(xs, norm_final, F32).reshape(nb, 1, d)
    return (y_prompt, y_sample, *[jnp.stack(o, axis=0) for o in outs_p], *[jnp.stack(o, axis=0) for o in outs_s])
```

```python
import functools

import numpy as np
import jax
import jax.numpy as jnp
from jax import lax
from jax.experimental import pallas as pl
from jax.experimental.pallas import tpu as pltpu

F32 = jnp.float32
BF16 = jnp.bfloat16
HIGHEST = lax.Precision.HIGHEST

D_MODEL = 2048
N_META = 16
CHUNK = 128
PAD_FRONT = CHUNK - N_META
RMS_EPS = 1e-6
GN_EPS = 1e-5
RET_HEADS = 4
RET_DK = 256
RET_DV = 256
RET_W = RET_HEADS * RET_DK
ROPE_BASE = 10000.0
FOX_HEADS = 8
FOX_HD = 128
FOX_W = FOX_HEADS * FOX_HD
FOX_PAIR = 2
LOG2E = 1.4426950408889634
RWKV_HEADS = 16
RWKV_HD = 64
RWKV_W = RWKV_HEADS * RWKV_HD
RWKV_W_RANK = 64
RWKV_A_RANK = 64
RWKV_G_RANK = 128
RWKV_GN_EPS = 64e-5
RWKV_PROJ = 3 * RWKV_W + RWKV_W_RANK + RWKV_A_RANK + RWKV_G_RANK
RWKV_CHUNK = 64
RWKV_GROUP = 4
N_BRANCH = 3
N_GROUPS = 4
EXPERTS_PER_GROUP = 8
N_EXPERTS = N_GROUPS * EXPERTS_PER_GROUP
TOP_K = 2
EXPERT_FF = 1024
MOE_BLOCK = 128
ROUTER_COLS = 128

FF_PAD = 256
COL_RW = 0
COL_RQ = COL_RW + RWKV_PROJ
COL_RK = COL_RQ + RET_W
COL_RV = COL_RK + RET_W
COL_RG = COL_RV + RET_W
COL_FQ = COL_RG + RET_W
COL_FK = COL_FQ + FOX_W
COL_FV = COL_FK + FOX_W
COL_FF = COL_FV + FOX_W
COL_GATE = COL_FF + FF_PAD
PROJ_PACKED = COL_GATE + N_BRANCH * D_MODEL
SRC_RQ = 0
SRC_FF = 4 * RET_W + 3 * FOX_W
SRC_RW = SRC_FF + FOX_HEADS
SRC_GATE = SRC_RW + RWKV_PROJ

VMEM_LIMIT = 56 * 1024 * 1024
NEG_BIG = -1e30
_NT = (((1,), (1,)), ((), ()))
_TN = (((0,), (0,)), ((), ()))


def _cparams(*sem):
    return pltpu.CompilerParams(dimension_semantics=sem, vmem_limit_bytes=VMEM_LIMIT)


def _pick(n, prefs):
    for p in prefs:
        if n % p == 0:
            return p
    return n


def _rms(x, g):
    return x * lax.rsqrt(jnp.mean(x * x, axis=-1, keepdims=True) + RMS_EPS) * g


def _rms_body(x_ref, g_ref, o_ref):
    o_ref[...] = _rms(x_ref[...], g_ref[...]).astype(o_ref.dtype)


def _rmsnorm(x, g, out_dtype):
    n, d = x.shape
    tm = _pick(n, (256, 128, 8))
    return pl.pallas_call(
        _rms_body,
        grid=(n // tm,),
        in_specs=[pl.BlockSpec((tm, d), lambda i: (i, 0)), pl.BlockSpec((1, d), lambda i: (0, 0))],
        out_specs=pl.BlockSpec((tm, d), lambda i: (i, 0)),
        out_shape=jax.ShapeDtypeStruct((n, d), out_dtype),
        compiler_params=_cparams("parallel"),
        name="rmsnorm",
    )(x, g.reshape(1, d))


def _rms_router_body(x_ref, g_ref, wr_ref, h_ref, lg_ref):
    h = _rms(x_ref[...], g_ref[...])
    h_ref[...] = h
    lg_ref[...] = jnp.dot(h.astype(BF16), wr_ref[...].astype(BF16), preferred_element_type=F32)


def _rmsnorm_router(x, g, wr):
    n, d = x.shape
    tm = _pick(n, (256, 128, 8))
    return pl.pallas_call(
        _rms_router_body,
        grid=(n // tm,),
        in_specs=[pl.BlockSpec((tm, d), lambda i: (i, 0)), pl.BlockSpec((1, d), lambda i: (0, 0)),
                  pl.BlockSpec((d, ROUTER_COLS), lambda i: (0, 0))],
        out_specs=[pl.BlockSpec((tm, d), lambda i: (i, 0)), pl.BlockSpec((tm, ROUTER_COLS), lambda i: (i, 0))],
        out_shape=[jax.ShapeDtypeStruct((n, d), F32), jax.ShapeDtypeStruct((n, ROUTER_COLS), F32)],
        compiler_params=_cparams("parallel"),
        name="rmsnorm_router",
    )(x, g.reshape(1, d), wr)


def _final_norm_prompt(x, g, batch, tp):
    d = x.shape[1]
    nb = tp // CHUNK
    return pl.pallas_call(
        _rms_body,
        grid=(batch, nb - 1),
        in_specs=[pl.BlockSpec((CHUNK, d), lambda b, j: (b * nb + 1 + j, 0)), pl.BlockSpec((1, d), lambda b, j: (0, 0))],
        out_specs=pl.BlockSpec((CHUNK, d), lambda b, j: (b * (nb - 1) + j, 0)),
        out_shape=jax.ShapeDtypeStruct((batch * (tp - CHUNK), d), F32),
        compiler_params=_cparams("parallel", "parallel"),
        name="final_norm",
    )(x, g.reshape(1, d)).reshape(batch, tp - CHUNK, d)


def _mm_body(a_ref, w_ref, o_ref):
    o_ref[...] = jnp.dot(a_ref[...], w_ref[...], preferred_element_type=F32).astype(o_ref.dtype)


def _matmul(a, w, out_dtype=F32, tm_prefs=(1408, 768, 512, 384, 256, 128), tn_prefs=(512, 256, 128)):
    m, k = a.shape
    n = w.shape[1]
    tm = _pick(m, tm_prefs)
    tn = _pick(n, tn_prefs)
    return pl.pallas_call(
        _mm_body,
        grid=(m // tm, n // tn),
        in_specs=[pl.BlockSpec((tm, k), lambda i, j: (i, 0)), pl.BlockSpec((k, tn), lambda i, j: (0, j))],
        out_specs=pl.BlockSpec((tm, tn), lambda i, j: (i, j)),
        out_shape=jax.ShapeDtypeStruct((m, n), out_dtype),
        compiler_params=_cparams("parallel", "arbitrary"),
        name="matmul",
    )(a, w)


def _mm_res_body(a_ref, w_ref, r_ref, o_ref, *, blocks_per_seq, pad):
    y = r_ref[...] + jnp.dot(a_ref[...], w_ref[...], preferred_element_type=F32)
    if pad:
        first = (pl.program_id(0) % blocks_per_seq) == 0
        row = lax.broadcasted_iota(jnp.int32, y.shape, 0)
        y = jnp.where(jnp.logical_and(first, row < pad), 0.0, y)
    o_ref[...] = y


def _matmul_residual(a, w, res, tp=None):
    m, k = a.shape
    n = w.shape[1]
    tm = _pick(tp, (768, 384, 128)) if tp else m
    assert m % tm == 0
    tn = _pick(n, (512, 256, 128))
    body = functools.partial(_mm_res_body, blocks_per_seq=(tp // tm if tp else 1), pad=(PAD_FRONT if tp else 0))
    return pl.pallas_call(
        body,
        grid=(m // tm, n // tn),
        in_specs=[pl.BlockSpec((tm, k), lambda i, j: (i, 0)), pl.BlockSpec((k, tn), lambda i, j: (0, j)),
                  pl.BlockSpec((tm, tn), lambda i, j: (i, j))],
        out_specs=pl.BlockSpec((tm, tn), lambda i, j: (i, j)),
        out_shape=jax.ShapeDtypeStruct((m, n), F32),
        compiler_params=_cparams("parallel", "arbitrary"),
        name="matmul_residual",
    )(a, w, res)


def _rope_halves(x, cos, sin):
    half = x.shape[-1] // 2
    x1, x2 = x[:, :half], x[:, half:]
    return jnp.concatenate([x1 * cos - x2 * sin, x1 * sin + x2 * cos], axis=-1)


def _head_norm(y, eps):
    mu = jnp.mean(y, axis=-1, keepdims=True)
    yc = y - mu
    return yc * lax.rsqrt(jnp.mean(yc * yc, axis=-1, keepdims=True) + eps)


def _silu(x):
    return x / (1.0 + jnp.exp(-x))


def _sigmoid(x):
    return 1.0 / (1.0 + jnp.exp(-x))


def _ret_body(q_ref, k_ref, v_ref, g_ref, cos_ref, sin_ref, dm_ref, cd_ref, kd_ref, sd_ref, o_ref, s_ref):
    c = pl.program_id(2)

    @pl.when(c == 0)
    def _():
        s_ref[...] = jnp.zeros(s_ref.shape, F32)

    cos = cos_ref[...]
    sin = sin_ref[...]
    q = _rope_halves(q_ref[...], cos, sin)
    k = _rope_halves(k_ref[...], cos, sin) * (RET_DK ** -0.5)
    qb = q.astype(BF16)
    kb = k.astype(BF16)
    vb = v_ref[...].astype(BF16)
    s_old = s_ref[0, 0]
    scores = lax.dot_general(qb, kb, (((1,), (1,)), ((), ())), preferred_element_type=F32) * dm_ref[0]
    intra = jnp.dot(scores.astype(BF16), vb, preferred_element_type=F32)
    cross = jnp.dot(qb, s_old.astype(BF16), preferred_element_type=F32) * cd_ref[0]
    kdec = (k * kd_ref[0]).astype(BF16)
    s_ref[0, 0] = sd_ref[0] * s_old + lax.dot_general(kdec, vb, (((0,), (0,)), ((), ())), preferred_element_type=F32)
    o = _head_norm(intra + cross, GN_EPS) * _silu(g_ref[...])
    o_ref[...] = o.astype(o_ref.dtype)


def _ret_tables(length):
    lg = jnp.log1p(-jnp.power(2.0, -5.0 - jnp.arange(RET_HEADS, dtype=F32)))
    i = jnp.arange(length, dtype=F32)
    diff = i[:, None] - i[None, :]
    dmask = jnp.where(diff >= 0, jnp.exp(lg[:, None, None] * jnp.maximum(diff, 0.0)), 0.0)
    cdec = jnp.exp(lg[:, None] * (i + 1.0)[None, :])
    kdec = jnp.exp(lg[:, None] * (length - 1.0 - i)[None, :])
    sdec = jnp.exp(lg * length)
    return lg, dmask, cdec, kdec, sdec


def _rope_tables(pos):
    half = RET_DK // 2
    inv = ROPE_BASE ** (-jnp.arange(half, dtype=F32) / half)
    ang = pos.astype(F32)[:, None] * inv[None, :]
    return jnp.cos(ang), jnp.sin(ang)


def _retention_prompt(proj, batch, tp):
    n = proj.shape[0]
    nc = tp // CHUNK
    _, dmask, cdec, kdec, sdec = _ret_tables(CHUNK)
    cdec = jnp.broadcast_to(cdec[:, :, None], (RET_HEADS, CHUNK, RET_DV))
    kdec = jnp.broadcast_to(kdec[:, :, None], (RET_HEADS, CHUNK, RET_DK))
    sdec = jnp.broadcast_to(sdec[:, None, None], (RET_HEADS, 1, RET_DV))
    cos, sin = _rope_tables(jnp.arange(tp, dtype=jnp.int32) - PAD_FRONT)
    blk = lambda col: pl.BlockSpec((CHUNK, RET_DK), lambda b, h, c, col=col: (b * nc + c, col // RET_DK + h))
    tab = lambda shape: pl.BlockSpec((1,) + shape, lambda b, h, c: (h, 0, 0))
    rope = pl.BlockSpec((CHUNK, RET_DK // 2), lambda b, h, c: (c, 0))
    return pl.pallas_call(
        _ret_body,
        grid=(batch, RET_HEADS, nc),
        in_specs=[blk(COL_RQ), blk(COL_RK), blk(COL_RV), blk(COL_RG), rope, rope,
                  tab((CHUNK, CHUNK)), tab((CHUNK, RET_DV)), tab((CHUNK, RET_DK)), tab((1, RET_DV))],
        out_specs=[pl.BlockSpec((CHUNK, RET_DV), lambda b, h, c: (b * nc + c, h)),
                   pl.BlockSpec((1, 1, RET_DK, RET_DV), lambda b, h, c: (b, h, 0, 0))],
        out_shape=[jax.ShapeDtypeStruct((n, RET_W), BF16), jax.ShapeDtypeStruct((batch, RET_HEADS, RET_DK, RET_DV), F32)],
        compiler_params=_cparams("parallel", "parallel", "arbitrary"),
        name="retention_prompt",
    )(proj, proj, proj, proj, cos, sin, dmask, cdec, kdec, sdec)


def _fox_body(qi_ref, ki_ref, q_ref, k_ref, v_ref, qx_ref, kx_ref, o_ref, m_sc, l_sc, acc_sc, *, tq, tk, hp):
    step = pl.program_id(2)
    qi = qi_ref[step]
    ki = ki_ref[step]

    @pl.when(ki == 0)
    def _():
        m_sc[...] = jnp.full(m_sc.shape, NEG_BIG, F32)
        l_sc[...] = jnp.zeros(l_sc.shape, F32)
        acc_sc[...] = jnp.zeros(acc_sc.shape, F32)

    def update(masked):
        if masked:
            qpos = qi * tq + lax.broadcasted_iota(jnp.int32, (tq, tk), 0)
            kpos = ki * tk + lax.broadcasted_iota(jnp.int32, (tq, tk), 1)
            valid = jnp.logical_and(kpos <= qpos, kpos >= PAD_FRONT)
        m_old, l_old, acc_old = m_sc[...], l_sc[...], acc_sc[...]
        m_out, l_out, acc_out = [], [], []
        for h in range(hp):
            sl = slice(h * FOX_HD, (h + 1) * FOX_HD)
            qa = jnp.concatenate([q_ref[:, sl].astype(BF16), qx_ref[0, h]], axis=1)
            ka = jnp.concatenate([k_ref[:, sl].astype(BF16), kx_ref[0, h]], axis=1)
            s = lax.dot_general(qa, ka, _NT, preferred_element_type=F32) * (FOX_HD ** -0.5 * LOG2E)
            if masked:
                s = jnp.where(valid, s, NEG_BIG)
            m_new = jnp.maximum(m_old[h], jnp.max(s, axis=1, keepdims=True))
            alpha = jnp.exp2(m_old[h] - m_new)
            p = jnp.exp2(s - m_new)
            m_out.append(m_new)
            l_out.append(alpha * l_old[h] + jnp.sum(p, axis=1, keepdims=True))
            acc_out.append(alpha * acc_old[:, sl] + jnp.dot(p.astype(BF16), v_ref[:, sl].astype(BF16),
                                                            preferred_element_type=F32))
        for h in range(hp):
            m_sc[h] = m_out[h]
            l_sc[h] = l_out[h]
            acc_sc[:, h * FOX_HD:(h + 1) * FOX_HD] = acc_out[h]

    edge = jnp.logical_or(ki == qi, ki == 0)
    pl.when(edge)(functools.partial(update, True))
    pl.when(jnp.logical_not(edge))(functools.partial(update, False))

    @pl.when(ki == qi)
    def _():
        for h in range(hp):
            sl = slice(h * FOX_HD, (h + 1) * FOX_HD)
            o_ref[:, sl] = (acc_sc[:, sl] / l_sc[h]).astype(o_ref.dtype)


def _split3(x):
    hi = x.astype(BF16)
    r1 = x - hi.astype(F32)
    mid = r1.astype(BF16)
    return hi, mid, (r1 - mid.astype(F32)).astype(BF16)


def _fox_prompt(proj, c, batch, tp):
    n = proj.shape[0]
    tq = _pick(tp, (384, 256, 128))
    nq = tp // tq
    hp = FOX_PAIR
    pairs = [(i, j) for i in range(nq) for j in range(i + 1)]
    qi_tab = jnp.asarray(np.array([p[0] for p in pairs], np.int32))
    ki_tab = jnp.asarray(np.array([p[1] for p in pairs], np.int32))
    hi, mid, lo = _split3(c * (FOX_HD ** 0.5))
    one = jnp.ones_like(hi)
    fill = jnp.zeros(c.shape + (FOX_HD - 6,), BF16)
    qx = jnp.concatenate([jnp.stack([hi, mid, lo, one, one, one], axis=-1), fill], axis=-1)
    kx = jnp.concatenate([jnp.stack([one, one, one, -hi, -mid, -lo], axis=-1), fill], axis=-1)
    cb = lambda col: col // (hp * FOX_HD)
    wide = hp * FOX_HD
    grid_spec = pltpu.PrefetchScalarGridSpec(
        num_scalar_prefetch=2,
        grid=(batch, FOX_HEADS // hp, len(pairs)),
        in_specs=[
            pl.BlockSpec((tq, wide), lambda b, h, s, qi, ki: (b * nq + qi[s], cb(COL_FQ) + h)),
            pl.BlockSpec((tq, wide), lambda b, h, s, qi, ki: (b * nq + ki[s], cb(COL_FK) + h)),
            pl.BlockSpec((tq, wide), lambda b, h, s, qi, ki: (b * nq + ki[s], cb(COL_FV) + h)),
            pl.BlockSpec((1, hp, tq, FOX_HD), lambda b, h, s, qi, ki: (b, h, qi[s], 0)),
            pl.BlockSpec((1, hp, tq, FOX_HD), lambda b, h, s, qi, ki: (b, h, ki[s], 0)),
        ],
        out_specs=pl.BlockSpec((tq, wide), lambda b, h, s, qi, ki: (b * nq + qi[s], h)),
        scratch_shapes=[pltpu.VMEM((hp, tq, 1), F32), pltpu.VMEM((hp, tq, 1), F32), pltpu.VMEM((tq, wide), F32)],
    )
    return pl.pallas_call(
        functools.partial(_fox_body, tq=tq, tk=tq, hp=hp),
        grid_spec=grid_spec,
        out_shape=jax.ShapeDtypeStruct((n, FOX_W), BF16),
        compiler_params=_cparams("parallel", "parallel", "arbitrary"),
        name="fox_prompt",
    )(qi_tab, ki_tab, proj, proj, proj, qx, kx)


def _softplus(z):
    return jnp.maximum(z, 0.0) + jnp.log1p(jnp.exp(-jnp.abs(z)))


def _rwkv_prep_math(c, prev, mu, w0, w2, a0, a2, g2, kkp, ka, exact):
    w = RWKV_W
    xm = c + mu * (prev - c)
    r, k, v = xm[:, 0:w], xm[:, w:2 * w], xm[:, 2 * w:3 * w]
    wd = xm[:, 3 * w:3 * w + RWKV_W_RANK]
    ad = xm[:, 3 * w + RWKV_W_RANK:3 * w + RWKV_W_RANK + RWKV_A_RANK]
    gd = xm[:, 3 * w + RWKV_W_RANK + RWKV_A_RANK:]
    if exact:
        mm = lambda x, m: jnp.dot(x, m, precision=HIGHEST, preferred_element_type=F32)
    else:
        mm = lambda x, m: jnp.dot(x.astype(BF16), m.astype(BF16), preferred_element_type=F32)
    w_log = -_softplus(-(w0 + mm(jnp.tanh(wd), w2))) - 0.5
    lw = -jnp.exp(w_log)
    a = _sigmoid(a0 + mm(ad, a2))
    g = mm(_sigmoid(gd), g2)
    kk0 = k * kkp
    kmod = k * (1.0 + (a - 1.0) * ka)
    return r, kmod, v, lw, kk0, a, g


def _rwkv_prep_body(c_ref, mu_ref, w0_ref, w2_ref, a0_ref, a2_ref, g2_ref, kkp_ref, ka_ref,
                    r_ref, k_ref, v_ref, lw_ref, kk_ref, a_ref, g_ref, carry):
    t = pl.program_id(1)

    @pl.when(t == 0)
    def _():
        carry[...] = jnp.zeros(carry.shape, F32)

    c = c_ref[...]
    rows = c.shape[0]
    prev = pltpu.roll(c, 1, axis=0)
    row = lax.broadcasted_iota(jnp.int32, c.shape, 0)
    prev = jnp.where(row == 0, carry[...], prev)
    carry[...] = c[rows - 1:rows, :]
    outs = _rwkv_prep_math(c, prev, mu_ref[...], w0_ref[...], w2_ref[...], a0_ref[...], a2_ref[...], g2_ref[...],
                           kkp_ref[...], ka_ref[...], exact=False)
    for ref, val in zip((r_ref, k_ref, v_ref, lw_ref, kk_ref, a_ref, g_ref), outs):
        ref[...] = val


def _rwkv_prep_prompt(proj, lp, batch, tp):
    n = proj.shape[0]
    tb = CHUNK
    nt = tp // tb
    row = lambda x: x.reshape(1, -1)
    full = lambda shape: pl.BlockSpec(shape, lambda b, t: (0, 0))
    out_spec = pl.BlockSpec((tb, RWKV_W), lambda b, t: (b * nt + t, 0))
    return pl.pallas_call(
        _rwkv_prep_body,
        grid=(batch, nt),
        in_specs=[pl.BlockSpec((tb, RWKV_PROJ), lambda b, t: (b * nt + t, COL_RW // RWKV_PROJ)),
                  full((1, RWKV_PROJ)), full((1, RWKV_W)), full((RWKV_W_RANK, RWKV_W)), full((1, RWKV_W)),
                  full((RWKV_A_RANK, RWKV_W)), full((RWKV_G_RANK, RWKV_W)), full((1, RWKV_W)), full((1, RWKV_W))],
        out_specs=[out_spec] * 7,
        out_shape=[jax.ShapeDtypeStruct((n, RWKV_W), F32)] * 7,
        scratch_shapes=[pltpu.VMEM((1, RWKV_PROJ), F32)],
        compiler_params=_cparams("parallel", "arbitrary"),
        name="rwkv_prep",
    )(proj, row(lp['mu']), row(lp['w0']), lp['w2'], row(lp['a0']), lp['a2'], lp['g2'], row(lp['kk']), row(lp['ka']))


def _rwkv_chunk_body(r_ref, k_ref, v_ref, lw_ref, kk_ref, a_ref, g_ref, rk_ref, lnw_ref, lnb_ref, o_ref, s_ref):
    cn = pl.program_id(1)

    @pl.when(cn == 0)
    def _():
        s_ref[...] = jnp.zeros(s_ref.shape, F32)

    cs = RWKV_CHUNK
    hd = RWKV_HD
    gh = RWKV_GROUP
    n = gh * cs
    gw = gh * hd
    bits = int(np.log2(cs))
    row = lax.broadcasted_iota(jnp.int32, (n, n), 0)
    col = lax.broadcasted_iota(jnp.int32, (n, n), 1)
    same_head = (row >> bits) == (col >> bits)
    strict = same_head & (col < row)
    incl = same_head & (col <= row)
    eye = (row == col).astype(F32)
    pair_masks = [((row >> (bit + 1)) == (col >> (bit + 1))) & ((row & (1 << bit)) != 0) & ((col & (1 << bit)) == 0)
                  for bit in range(bits)]
    trow = lax.broadcasted_iota(jnp.int32, (cs, cs), 0)
    tcol = lax.broadcasted_iota(jnp.int32, (cs, cs), 1)
    cum_all = jnp.dot((tcol <= trow).astype(F32), lw_ref[...], precision=HIGHEST, preferred_element_type=F32)
    decay_all = jnp.exp(cum_all[cs - 1:cs, :])
    tail_all = jnp.exp(cum_all[cs - 1:cs, :] - cum_all)
    dot = lambda x, y: jnp.dot(x.astype(BF16), y.astype(BF16), preferred_element_type=F32)
    dot_nt = lambda x, y: lax.dot_general(x.astype(BF16), y.astype(BF16), _NT, preferred_element_type=F32)
    dot_tn = lambda x, y: lax.dot_general(x.astype(BF16), y.astype(BF16), _TN, preferred_element_type=F32)
    for g in range(RWKV_HEADS // gh):
        lanes = slice(g * gw, (g + 1) * gw)
        stack = lambda x: jnp.concatenate([x[:, g * gw + h * hd:g * gw + (h + 1) * hd] for h in range(gh)], axis=0)
        rows = lambda x: jnp.concatenate([jnp.broadcast_to(x[:, g * gw + h * hd:g * gw + (h + 1) * hd], (cs, hd))
                                          for h in range(gh)], axis=0)
        lw, cum, tail = stack(lw_ref[...]), stack(cum_all), stack(tail_all)
        kk0 = stack(kk_ref[...])
        kk = kk0 * lax.rsqrt(jnp.sum(kk0 * kk0, axis=-1, keepdims=True) + 1e-12)
        b = kk * stack(a_ref[...])
        r, k, v = stack(r_ref[...]), stack(k_ref[...]), stack(v_ref[...])
        e_in = jnp.exp(cum)
        e_neg = jnp.exp(-cum)
        ar = jnp.concatenate([-kk * jnp.exp(cum - lw), r * e_in], axis=0)
        bk = jnp.concatenate([b * e_neg, k * e_neg], axis=0)
        gram = dot_nt(ar, bk)
        l_ab = jnp.where(strict, gram[:n, :n], 0.0)
        l_ak = jnp.where(strict, gram[:n, n:], 0.0)
        m_rbk = jnp.concatenate([jnp.where(incl, gram[n:, :n], 0.0), jnp.where(incl, gram[n:, n:], 0.0)], axis=1)
        tinv = eye + jnp.where(pair_masks[0], l_ab, 0.0)
        for mask in pair_masks[1:]:
            tinv = tinv + dot(dot(tinv, jnp.where(mask, l_ab, 0.0)), tinv)
        s_old = s_ref[0, g * n:(g + 1) * n, :]
        ars = dot_nt(ar, s_old)
        v_bd = jnp.where(same_head, jnp.concatenate([v_ref[:, lanes]] * gh, axis=0), 0.0)
        u = dot(tinv, jnp.where(same_head, ars[:n], 0.0) + dot(l_ak, v_bd))
        uv = jnp.concatenate([u, v_bd], axis=0)
        y_bd = jnp.where(same_head, ars[n:], 0.0) + dot(m_rbk, uv)
        s_ref[0, g * n:(g + 1) * n, :] = s_old * rows(decay_all) + dot_tn(uv, jnp.concatenate([b * tail, k * tail], axis=0))
        y = sum(y_bd[:, h * hd:(h + 1) * hd] for h in range(gh))
        yn = _head_norm(y, RWKV_GN_EPS) * rows(lnw_ref[...]) + rows(lnb_ref[...])
        bonus = jnp.sum(r * k * rows(rk_ref[...]), axis=-1, keepdims=True) * v
        out = (yn + bonus) * stack(g_ref[...])
        o_ref[:, lanes] = jnp.concatenate([out[h * cs:(h + 1) * cs] for h in range(gh)], axis=1).astype(o_ref.dtype)


def _rwkv_chunk_prompt(prep, lp, batch, tp):
    n = prep[0].shape[0]
    cs = RWKV_CHUNK
    ncn = tp // cs
    row = lambda x: x.reshape(1, -1)
    blk = pl.BlockSpec((cs, RWKV_W), lambda b, c: (b * ncn + c, 0))
    full = pl.BlockSpec((1, RWKV_W), lambda b, c: (0, 0))
    out, state = pl.pallas_call(
        _rwkv_chunk_body,
        grid=(batch, ncn),
        in_specs=[blk] * 7 + [full] * 3,
        out_specs=[blk, pl.BlockSpec((1, RWKV_W, RWKV_HD), lambda b, c: (b, 0, 0))],
        out_shape=[jax.ShapeDtypeStruct((n, RWKV_W), BF16), jax.ShapeDtypeStruct((batch, RWKV_W, RWKV_HD), F32)],
        compiler_params=_cparams("parallel", "arbitrary"),
        name="rwkv_chunk",
    )(*prep, row(lp['rk']), row(lp['ln_w']), row(lp['ln_b']))
    return out, state.reshape(batch, RWKV_HEADS, RWKV_HD, RWKV_HD)


def _merge_body(ro_ref, fo_ref, wo_ref, wr_ref, wf_ref, ww_ref, g0_ref, g1_ref, g2_ref, o_ref):
    dot = functools.partial(jnp.dot, preferred_element_type=F32)
    m = (_sigmoid(g0_ref[...]) * dot(ro_ref[...], wr_ref[...])
         + _sigmoid(g1_ref[...]) * dot(fo_ref[...], wf_ref[...])
         + _sigmoid(g2_ref[...]) * dot(wo_ref[...], ww_ref[...]))
    o_ref[...] = m.astype(o_ref.dtype)


def _merge(ret_o, fox_o, rw_o, wb_ret, wb_fox, wb_rwkv, gates, gate_col, out_dtype):
    m = ret_o.shape[0]
    tm = _pick(m, (768, 384, 128))
    tn = 512
    nj = D_MODEL // tn
    act = lambda width: pl.BlockSpec((tm, width), lambda i, j: (i, 0))
    wgt = lambda width: pl.BlockSpec((width, tn), lambda i, j: (0, j))
    gate = lambda br: pl.BlockSpec((tm, tn), lambda i, j, br=br: (i, gate_col // tn + br * nj + j))
    return pl.pallas_call(
        _merge_body,
        grid=(m // tm, nj),
        in_specs=[act(RET_W), act(FOX_W), act(RWKV_W), wgt(RET_W), wgt(FOX_W), wgt(RWKV_W), gate(0), gate(1), gate(2)],
        out_specs=pl.BlockSpec((tm, tn), lambda i, j: (i, j)),
        out_shape=jax.ShapeDtypeStruct((m, D_MODEL), out_dtype),
        compiler_params=_cparams("parallel", "arbitrary"),
        name="merge",
    )(ret_o, fox_o, rw_o, wb_ret, wb_fox, wb_rwkv, gates, gates, gates)


def _route(logits, bg, be):
    n = logits.shape[0]
    gp = jax.nn.softmax(logits[:, :N_GROUPS] + bg.astype(F32), axis=-1)
    gidx = jnp.argmax(gp, axis=-1)
    pg = jnp.take_along_axis(gp, gidx[:, None], axis=-1)
    el = logits[:, N_GROUPS:N_GROUPS + N_EXPERTS].reshape(n, N_GROUPS, EXPERTS_PER_GROUP) + be.astype(F32)[None]
    el = jnp.take_along_axis(el, gidx[:, None, None], axis=1)[:, 0]
    topv, topi = lax.top_k(jax.nn.softmax(el, axis=-1), TOP_K)
    gate = pg * topv / jnp.sum(topv, axis=-1, keepdims=True)
    eid = (gidx[:, None] * EXPERTS_PER_GROUP + topi).astype(jnp.int32)
    return eid, gate


def _router_weights(wg, we):
    d = wg.shape[0]
    wr = jnp.concatenate([wg, jnp.transpose(we, (1, 0, 2)).reshape(d, N_EXPERTS)], axis=1)
    return jnp.pad(wr, ((0, 0), (0, ROUTER_COLS - wr.shape[1])))


def _moe_body(be_ref, nused_ref, tok_ref, dst_ref, h_hbm, w1_ref, w3_ref, w2_ref, y_in, y_hbm, xbuf, ybuf, sem_in, sem_out):
    del y_in
    i = pl.program_id(0)
    n_used = nused_ref[0]
    slot = i % 2
    rows = range(MOE_BLOCK)

    def gather(blk, buf, r):
        return pltpu.make_async_copy(h_hbm.at[pl.ds(tok_ref[blk * MOE_BLOCK + r], 1)], xbuf.at[buf, pl.ds(r, 1)],
                                     sem_in.at[buf])

    def scatter(blk, r):
        return pltpu.make_async_copy(ybuf.at[pl.ds(r, 1)], y_hbm.at[pl.ds(dst_ref[blk * MOE_BLOCK + r], 1)], sem_out)

    @pl.when(jnp.logical_and(i == 0, n_used > 0))
    def _():
        for r in rows:
            gather(0, 0, r).start()

    @pl.when(i < n_used)
    def _():
        for r in rows:
            gather(i, slot, r).wait()
        nxt = jnp.minimum(i + 1, n_used - 1)
        for r in rows:
            gather(nxt, 1 - slot, r).start()
        x = xbuf[slot].astype(BF16)
        a = jnp.dot(x, w1_ref[0], preferred_element_type=F32)
        b = jnp.dot(x, w3_ref[0], preferred_element_type=F32)
        y = jnp.dot((_silu(a) * b).astype(BF16), w2_ref[0], preferred_element_type=F32)

        @pl.when(i > 0)
        def _():
            for r in rows:
                scatter(i - 1, r).wait()

        ybuf[...] = y
        for r in rows:
            scatter(i, r).start()

        @pl.when(i == n_used - 1)
        def _():
            for r in rows:
                gather(nxt, 1 - slot, r).wait()
                scatter(i, r).wait()


def _moe_prompt(h, eid, gate, valid, w1, w3, w2):
    n, d = h.shape
    a_tot = n * TOP_K
    n_real = int(np.sum(valid)) * TOP_K
    n_blk = (n_real + N_EXPERTS * (MOE_BLOCK - 1) + MOE_BLOCK - 1) // MOE_BLOCK
    cap = n_blk * MOE_BLOCK
    validf = jnp.repeat(jnp.asarray(valid), TOP_K)
    eflat = eid.reshape(-1)
    onehot = jnp.logical_and(eflat[:, None] == jnp.arange(N_EXPERTS, dtype=jnp.int32)[None, :], validf[:, None]).astype(jnp.int32)
    blocks = onehot.astype(F32).reshape(a_tot // CHUNK, CHUNK, N_EXPERTS)
    below = jnp.tril(jnp.ones((CHUNK, CHUNK), F32), -1)
    inner = jnp.einsum('ij,bjk->bik', below, blocks)
    totals = jnp.sum(blocks, axis=1)
    offset = jnp.cumsum(totals, axis=0) - totals
    before = (inner + offset[:, None, :]).reshape(a_tot, N_EXPERTS).astype(jnp.int32)
    rank = jnp.sum(before * onehot, axis=1)
    counts = jnp.sum(onehot, axis=0)
    padded = (counts + MOE_BLOCK - 1) // MOE_BLOCK * MOE_BLOCK
    pad_end = jnp.cumsum(padded)
    pad_start = pad_end - padded
    dest = jnp.where(validf, pad_start[eflat] + rank, cap)
    assign = jnp.arange(a_tot, dtype=jnp.int32)
    zero_row = int(np.argmin(valid))
    tok_buf = jnp.full((cap,), zero_row, jnp.int32).at[dest].set(assign // TOP_K, mode='drop')
    dump = a_tot + (jnp.arange(cap, dtype=jnp.int32) % MOE_BLOCK)
    dst_buf = dump.at[dest].set((assign % TOP_K) * n + assign // TOP_K, mode='drop')
    n_used = (pad_end[-1] // MOE_BLOCK).astype(jnp.int32)
    blk = jnp.arange(n_blk, dtype=jnp.int32)
    blk_e = jnp.minimum(jnp.searchsorted(pad_end, blk * MOE_BLOCK, side='right'), N_EXPERTS - 1).astype(jnp.int32)
    blk_e = jnp.where(blk < n_used, blk_e, blk_e[jnp.maximum(n_used - 1, 0)])
    y_rows = a_tot + MOE_BLOCK
    y0 = jnp.zeros((y_rows, d), F32)
    ff = w1.shape[2]
    grid_spec = pltpu.PrefetchScalarGridSpec(
        num_scalar_prefetch=4,
        grid=(n_blk,),
        in_specs=[pl.BlockSpec(memory_space=pl.ANY),
                  pl.BlockSpec((1, d, ff), lambda i, be, nu, tk, ds: (be[i], 0, 0)),
                  pl.BlockSpec((1, d, ff), lambda i, be, nu, tk, ds: (be[i], 0, 0)),
                  pl.BlockSpec((1, ff, d), lambda i, be, nu, tk, ds: (be[i], 0, 0)),
                  pl.BlockSpec(memory_space=pl.ANY)],
        out_specs=pl.BlockSpec(memory_space=pl.ANY),
        scratch_shapes=[pltpu.VMEM((2, MOE_BLOCK, d), F32), pltpu.VMEM((MOE_BLOCK, d), F32),
                        pltpu.SemaphoreType.DMA((2,)), pltpu.SemaphoreType.DMA(())],
    )
    y2 = pl.pallas_call(
        _moe_body,
        grid_spec=grid_spec,
        out_shape=jax.ShapeDtypeStruct((y_rows, d), F32),
        input_output_aliases={8: 0},
        compiler_params=_cparams("arbitrary"),
        name="moe_experts",
    )(blk_e, n_used.reshape(1), tok_buf, dst_buf, h, w1, w3, w2, y0)
    return y2


def _combine_body(x_ref, y0_ref, y1_ref, g_ref, o_ref):
    g = g_ref[...]
    o_ref[...] = x_ref[...] + (y0_ref[...] * g[:, 0:1] + y1_ref[...] * g[:, 1:2])


def _moe_combine(x, y2, gate):
    n, d = x.shape
    tm = _pick(n, (256, 128, 8))
    gpad = jnp.pad(gate, ((0, 0), (0, 128 - TOP_K)))
    return pl.pallas_call(
        _combine_body,
        grid=(n // tm,),
        in_specs=[pl.BlockSpec((tm, d), lambda i: (i, 0)), pl.BlockSpec((tm, d), lambda i: (i, 0)),
                  pl.BlockSpec((tm, d), lambda i: (n // tm + i, 0)), pl.BlockSpec((tm, 128), lambda i: (i, 0))],
        out_specs=pl.BlockSpec((tm, d), lambda i: (i, 0)),
        out_shape=jax.ShapeDtypeStruct((n, d), F32),
        compiler_params=_cparams("parallel"),
        name="moe_combine",
    )(x, y2, y2, gpad)


def _pack_w_in(w_in):
    ff = jnp.pad(w_in[:, SRC_FF:SRC_FF + FOX_HEADS], ((0, 0), (0, FF_PAD - FOX_HEADS)))
    return jnp.concatenate([w_in[:, SRC_RW:SRC_RW + RWKV_PROJ], w_in[:, :SRC_FF], ff, w_in[:, SRC_GATE:]],
                           axis=1).astype(BF16)


def _prompt_layer(x, lp, moe, batch, tp, valid):
    h = _rmsnorm(x, lp['norm_mix'], BF16)
    proj = _matmul(h, lp['w_in'])
    ret_o, ret_s = _retention_prompt(proj, batch, tp)
    ff = proj[:, COL_FF:COL_FF + FOX_HEADS].reshape(batch, tp, FOX_HEADS)
    logf = jax.nn.log_sigmoid(ff + lp['fox_b'].astype(F32))
    c = jnp.cumsum(logf, axis=1).transpose(0, 2, 1)
    fox_o = _fox_prompt(proj, c, batch, tp)
    prep = _rwkv_prep_prompt(proj, lp, batch, tp)
    rw_o, rw_s = _rwkv_chunk_prompt(prep, lp, batch, tp)
    merged = _merge(ret_o, fox_o, rw_o, lp['wb_ret'], lp['wb_fox'], lp['wb_rwkv'], proj, COL_GATE, BF16)
    x = _matmul_residual(merged, lp['w_out'], x, tp=tp)
    h2, logits = _rmsnorm_router(x, lp['norm_ffn'], moe['wr'])
    eid, gate = _route(logits, moe['bg'], moe['be'])
    y2 = _moe_prompt(h2, eid, gate, valid, moe['w1'], moe['w3'], moe['w2'])
    x = _moe_combine(x, y2, gate)
    p3 = proj.reshape(batch, tp, PROJ_PACKED)
    heads = lambda col: p3[:, PAD_FRONT:, col:col + FOX_W].reshape(batch, tp - PAD_FRONT, FOX_HEADS, FOX_HD)
    state = (heads(COL_FK), heads(COL_FV), logf[:, PAD_FRONT:], ret_s, rw_s, p3[:, tp - 1, COL_RW:COL_RW + RWKV_PROJ])
    return x, state


def _rows16(x):
    return jnp.concatenate([x.astype(BF16), jnp.zeros(x.shape, BF16)], axis=0)


def _mm_sample(x, w):
    return _matmul(_rows16(x), w, tn_prefs=(1536, 1024, 512, 256, 128))[:x.shape[0]]


def _pad8(x):
    first = lax.broadcasted_iota(jnp.int32, (8, x.shape[1]), 0) == 0
    return jnp.where(first, jnp.broadcast_to(x, (8, x.shape[1])), 0.0)


def _ret_sample_body(q_ref, k_ref, v_ref, g_ref, cos_ref, sin_ref, dec_ref, s0_ref, o_ref, s_ref):
    b = pl.program_id(1)
    row1 = lambda ref: ref[pl.ds(b, 1), :]
    cos = cos_ref[...]
    sin = sin_ref[...]
    q = _rope_halves(row1(q_ref), cos, sin)
    k = _rope_halves(row1(k_ref), cos, sin) * (RET_DK ** -0.5)
    v = row1(v_ref)
    dec = dec_ref[0]
    s0 = s0_ref[0, 0]
    rnd = lambda x: x.astype(BF16).astype(F32)
    cross = jnp.dot(_pad8(q).astype(BF16), s0.astype(BF16), preferred_element_type=F32)[0:1] * dec
    intra = jnp.sum(rnd(q) * rnd(k), axis=-1, keepdims=True) * v
    s_ref[0, 0] = dec * s0 + lax.dot_general(_pad8(k), _pad8(v), _TN, precision=HIGHEST, preferred_element_type=F32)
    o_ref[pl.ds(b, 1), :] = _head_norm(intra + cross, GN_EPS) * _silu(row1(g_ref))


def _retention_sample(proj, s0, pos):
    nb = proj.shape[0]
    lg = _ret_tables(1)[0]
    dec = jnp.broadcast_to(jnp.exp(lg)[:, None, None], (RET_HEADS, 1, RET_DV))
    cos, sin = _rope_tables(pos)
    blk = lambda col: pl.BlockSpec((nb, RET_DK), lambda h, b, col=col: (0, col // RET_DK + h))
    rope = pl.BlockSpec((1, RET_DK // 2), lambda h, b: (0, 0))
    st = pl.BlockSpec((1, 1, RET_DK, RET_DV), lambda h, b: (b, h, 0, 0))
    return pl.pallas_call(
        _ret_sample_body,
        grid=(RET_HEADS, nb),
        in_specs=[blk(COL_RQ), blk(COL_RK), blk(COL_RV), blk(COL_RG), rope, rope,
                  pl.BlockSpec((1, 1, RET_DV), lambda h, b: (h, 0, 0)), st],
        out_specs=[pl.BlockSpec((nb, RET_DV), lambda h, b: (0, h)), st],
        out_shape=[jax.ShapeDtypeStruct((nb, RET_W), F32), jax.ShapeDtypeStruct(s0.shape, F32)],
        compiler_params=_cparams("parallel", "arbitrary"),
        name="retention_sample",
    )(proj, proj, proj, proj, cos, sin, dec, s0)


def _rwkv_prep_sample_body(c_ref, prev_ref, mu_ref, w0_ref, w2_ref, a0_ref, a2_ref, g2_ref, kkp_ref, ka_ref, *out_refs):
    outs = _rwkv_prep_math(c_ref[...], prev_ref[...], mu_ref[...], w0_ref[...], w2_ref[...], a0_ref[...], a2_ref[...],
                           g2_ref[...], kkp_ref[...], ka_ref[...], exact=False)
    for ref, val in zip(out_refs, outs):
        ref[...] = val


def _rwkv_step_body(r_ref, k_ref, v_ref, lw_ref, kk_ref, a_ref, g_ref, rk_ref, lnw_ref, lnb_ref, s0_ref, o_ref, s_ref):
    b = pl.program_id(0)
    dg = functools.partial(lax.dot_general, precision=HIGHEST, preferred_element_type=F32)
    r_all, k_all, v_all, lw_all, kk_all, a_all, g_all = (
        ref[pl.ds(b, 1), :] for ref in (r_ref, k_ref, v_ref, lw_ref, kk_ref, a_ref, g_ref))
    rk_all, lnw_all, lnb_all = rk_ref[...], lnw_ref[...], lnb_ref[...]
    outs = []
    for h in range(RWKV_HEADS):
        sl = slice(h * RWKV_HD, (h + 1) * RWKV_HD)
        kk0 = kk_all[:, sl]
        kk = kk0 * lax.rsqrt(jnp.sum(kk0 * kk0, axis=-1, keepdims=True) + 1e-12)
        bb = kk * a_all[:, sl]
        w = jnp.exp(lw_all[:, sl])
        r, k, v = r_all[:, sl], k_all[:, sl], v_all[:, sl]
        s0 = s0_ref[0, h]
        sa = lax.dot_general(s0.astype(BF16), _pad8(-kk).astype(BF16), _NT, preferred_element_type=F32)[:, 0:1]
        s_new = s0 * w + sa * bb + dg(_pad8(v), _pad8(k), _TN)
        s_ref[0, h] = s_new
        y = lax.dot_general(_pad8(r).astype(BF16), s_new.astype(BF16), _NT, preferred_element_type=F32)[0:1]
        yn = _head_norm(y, RWKV_GN_EPS) * lnw_all[:, sl] + lnb_all[:, sl]
        bonus = jnp.sum(r * k * rk_all[:, sl], axis=-1, keepdims=True) * v
        outs.append((yn + bonus) * g_all[:, sl])
    o_ref[pl.ds(b, 1), :] = jnp.concatenate(outs, axis=1)


def _rwkv_sample(proj, lp, s0, shift0):
    nb = proj.shape[0]
    row = lambda x: x.reshape(1, -1)
    prep = pl.pallas_call(
        _rwkv_prep_sample_body,
        grid=(1,),
        in_specs=[pl.BlockSpec((nb, RWKV_PROJ), lambda i: (0, COL_RW // RWKV_PROJ)), pl.BlockSpec((nb, RWKV_PROJ), lambda i: (0, 0)),
                  pl.BlockSpec((1, RWKV_PROJ), lambda i: (0, 0)), pl.BlockSpec((1, RWKV_W), lambda i: (0, 0)),
                  pl.BlockSpec((RWKV_W_RANK, RWKV_W), lambda i: (0, 0)), pl.BlockSpec((1, RWKV_W), lambda i: (0, 0)),
                  pl.BlockSpec((RWKV_A_RANK, RWKV_W), lambda i: (0, 0)), pl.BlockSpec((RWKV_G_RANK, RWKV_W), lambda i: (0, 0)),
                  pl.BlockSpec((1, RWKV_W), lambda i: (0, 0)), pl.BlockSpec((1, RWKV_W), lambda i: (0, 0))],
        out_specs=[pl.BlockSpec((nb, RWKV_W), lambda i: (0, 0))] * 7,
        out_shape=[jax.ShapeDtypeStruct((nb, RWKV_W), F32)] * 7,
        compiler_params=_cparams("arbitrary"),
        name="rwkv_prep_sample",
    )(proj, shift0, row(lp['mu']), row(lp['w0']), lp['w2'], row(lp['a0']), lp['a2'], lp['g2'], row(lp['kk']), row(lp['ka']))
    act = pl.BlockSpec((nb, RWKV_W), lambda b: (0, 0))
    par = pl.BlockSpec((1, RWKV_W), lambda b: (0, 0))
    st = pl.BlockSpec((1, RWKV_HEADS, RWKV_HD, RWKV_HD), lambda b: (b, 0, 0, 0))
    return pl.pallas_call(
        _rwkv_step_body,
        grid=(nb,),
        in_specs=[act] * 7 + [par] * 3 + [st],
        out_specs=[act, st],
        out_shape=[jax.ShapeDtypeStruct((nb, RWKV_W), F32), jax.ShapeDtypeStruct(s0.shape, F32)],
        compiler_params=_cparams("arbitrary"),
        name="rwkv_step",
    )(*prep, row(lp['rk']), row(lp['ln_w']), row(lp['ln_b']), s0)


def _fox_decode_body(pt_ref, q_ref, kn_ref, vn_ref, bias_ref, *refs, npg):
    del pt_ref
    k_refs, v_refs = refs[:npg], refs[npg:2 * npg]
    o_ref, s_sc, m_sc, l_sc, acc_sc = refs[2 * npg:]
    phase = pl.program_id(1)
    j = pl.program_id(2)
    last = pl.num_programs(2) - 1
    scale = FOX_HD ** -0.5
    rnd = lambda x: x.astype(BF16).astype(F32)

    @pl.when(jnp.logical_and(phase == 0, j == 0))
    def _():
        m_sc[...] = jnp.sum(rnd(q_ref[0]) * rnd(kn_ref[0]), axis=-1, keepdims=True) * scale
        l_sc[...] = jnp.ones(l_sc.shape, F32)

    @pl.when(phase == 0)
    def _():
        qb = q_ref[0].astype(BF16)
        for g in range(npg):
            rows = k_refs[g].shape[2] * FOX_HEADS
            kf = k_refs[g][0, 0].reshape(rows, FOX_HD).astype(BF16)
            s = lax.dot_general(qb, kf, _NT, preferred_element_type=F32) * scale + bias_ref[0, g]
            s_sc[j * npg + g] = s
            m_old = m_sc[...]
            m_new = jnp.maximum(m_old, jnp.max(s, axis=-1, keepdims=True))
            l_sc[...] = jnp.exp(m_old - m_new) * l_sc[...] + jnp.sum(jnp.exp(s - m_new), axis=-1, keepdims=True)
            m_sc[...] = m_new

    @pl.when(phase == 1)
    def _():
        @pl.when(j == 0)
        def _():
            self_score = jnp.sum(rnd(q_ref[0]) * rnd(kn_ref[0]), axis=-1, keepdims=True) * scale
            acc_sc[...] = rnd(jnp.exp(self_score - m_sc[...]) / l_sc[...]) * rnd(vn_ref[0])

        for g in range(npg):
            rows = v_refs[g].shape[2] * FOX_HEADS
            vf = v_refs[g][0, 0].reshape(rows, FOX_HD).astype(BF16)
            p = jnp.exp(s_sc[j * npg + g] - m_sc[...]) / l_sc[...]
            acc_sc[...] += jnp.dot(p.astype(BF16), vf, preferred_element_type=F32)

        @pl.when(j == last)
        def _():
            o_ref[0] = acc_sc[...]


def _fox_decode(q, k_new, v_new, logf_new, cache_k, cache_v, cache_logf, page_table, layer):
    nb, n_pages = page_table.shape
    page = cache_k.shape[2]
    npg = _pick(n_pages, (8, 4, 2, 1))
    plogf = cache_logf[layer][page_table].astype(F32).reshape(nb, n_pages * page, FOX_HEADS)
    dsuf = lax.cumsum(plogf, axis=1, reverse=True) - plogf
    bias = (dsuf + logf_new[:, None, :]).reshape(nb, n_pages, page, FOX_HEADS)
    own = jnp.eye(FOX_HEADS, dtype=bool)[None, None, :, None, :]
    bias = jnp.where(own, bias[:, :, None, :, :], NEG_BIG).reshape(nb, n_pages, FOX_HEADS, page * FOX_HEADS)
    nst = n_pages // npg
    tok = pl.BlockSpec((1, FOX_HEADS, FOX_HD), lambda b, ph, j, pt: (b, 0, 0))
    k_step = lambda ph, j: j * (1 - ph) + (nst - 1) * ph
    v_step = lambda ph, j: j * ph
    page_spec = lambda step, g: pl.BlockSpec(
        (1, 1, page, FOX_HEADS, FOX_HD), lambda b, ph, j, pt: (layer, pt[b, step(ph, j) * npg + g], 0, 0, 0))
    grid_spec = pltpu.PrefetchScalarGridSpec(
        num_scalar_prefetch=1,
        grid=(nb, 2, nst),
        in_specs=[tok, tok, tok,
                  pl.BlockSpec((1, npg, FOX_HEADS, page * FOX_HEADS), lambda b, ph, j, pt: (b, k_step(ph, j), 0, 0))]
                 + [page_spec(k_step, g) for g in range(npg)] + [page_spec(v_step, g) for g in range(npg)],
        out_specs=tok,
        scratch_shapes=[pltpu.VMEM((n_pages, FOX_HEADS, page * FOX_HEADS), F32), pltpu.VMEM((FOX_HEADS, 1), F32),
                        pltpu.VMEM((FOX_HEADS, 1), F32), pltpu.VMEM((FOX_HEADS, FOX_HD), F32)],
    )
    r3 = lambda x: x.reshape(nb, FOX_HEADS, FOX_HD)
    o = pl.pallas_call(
        functools.partial(_fox_decode_body, npg=npg),
        grid_spec=grid_spec,
        out_shape=jax.ShapeDtypeStruct((nb, FOX_HEADS, FOX_HD), F32),
        compiler_params=_cparams("parallel", "arbitrary", "arbitrary"),
        name="fox_decode",
    )(page_table, r3(q), r3(k_new), r3(v_new), bias, *([cache_k] * npg), *([cache_v] * npg))
    return o.reshape(nb, FOX_W)


def _moe_sample_body(e_ref, h_ref, w1_ref, w3_ref, w2_ref, wv_ref, o_ref):
    del e_ref
    m = o_ref.shape[0]

    @pl.when(pl.program_id(0) == 0)
    def _():
        o_ref[...] = jnp.zeros(o_ref.shape, F32)

    h2 = h_ref[...]
    a = jnp.dot(h2, w1_ref[0], preferred_element_type=F32)
    b = jnp.dot(h2, w3_ref[0], preferred_element_type=F32)
    y = jnp.dot((_silu(a) * b).astype(BF16), w2_ref[0], preferred_element_type=F32)
    o_ref[...] += wv_ref[0][:, 0:1] * y[:m]


def _moe_sample(h, eid, gate, w1, w3, w2):
    m, d = h.shape
    na = m * TOP_K
    order = jnp.argsort(eid.reshape(-1))
    e_sorted = eid.reshape(-1)[order].astype(jnp.int32)
    wv = jnp.zeros((na, m), F32).at[jnp.arange(na), order // TOP_K].set(gate.reshape(-1)[order])
    wv = jnp.broadcast_to(wv[:, :, None], (na, m, 128))
    ff = w1.shape[2]
    grid_spec = pltpu.PrefetchScalarGridSpec(
        num_scalar_prefetch=1,
        grid=(na,),
        in_specs=[pl.BlockSpec((2 * m, d), lambda s, e: (0, 0)),
                  pl.BlockSpec((1, d, ff), lambda s, e: (e[s], 0, 0)),
                  pl.BlockSpec((1, d, ff), lambda s, e: (e[s], 0, 0)),
                  pl.BlockSpec((1, ff, d), lambda s, e: (e[s], 0, 0)),
                  pl.BlockSpec((1, m, 128), lambda s, e: (s, 0, 0))],
        out_specs=pl.BlockSpec((m, d), lambda s, e: (0, 0)),
    )
    return pl.pallas_call(
        _moe_sample_body,
        grid_spec=grid_spec,
        out_shape=jax.ShapeDtypeStruct((m, d), F32),
        compiler_params=_cparams("arbitrary"),
        name="moe_sample",
    )(e_sorted, _rows16(h), w1, w3, w2, wv)


def _sample_layer(x, lp, moe, layer, cache_k, cache_v, cache_logf, page_table, s_ret, s_rwkv, s_shift, pos):
    nb = x.shape[0]
    proj = _mm_sample(_rmsnorm(x, lp['norm_mix'], F32), lp['w_in'])
    ret_o, ret_s = _retention_sample(proj, s_ret, pos)
    logf = jax.nn.log_sigmoid(proj[:, COL_FF:COL_FF + FOX_HEADS] + lp['fox_b'].astype(F32))
    fk, fv = proj[:, COL_FK:COL_FK + FOX_W], proj[:, COL_FV:COL_FV + FOX_W]
    fox_o = _fox_decode(proj[:, COL_FQ:COL_FQ + FOX_W], fk, fv, logf, cache_k, cache_v, cache_logf, page_table, layer)
    rw_o, rw_s = _rwkv_sample(proj, lp, s_rwkv, s_shift)
    g2 = jnp.concatenate([proj[:, COL_GATE:]] * 2, axis=0)
    m2 = _merge(_rows16(ret_o), _rows16(fox_o), _rows16(rw_o), lp['wb_ret'], lp['wb_fox'], lp['wb_rwkv'], g2, 0, F32)
    x = x + _mm_sample(m2[:nb], lp['w_out'])
    h2, logits = _rmsnorm_router(x, lp['norm_ffn'], moe['wr'])
    eid, gate = _route(logits, moe['bg'], moe['be'])
    x = x + _moe_sample(h2, eid, gate, moe['w1'], moe['w3'], moe['w2'])
    heads = lambda a: a.reshape(nb, 1, FOX_HEADS, FOX_HD)
    state = (heads(fk), heads(fv), logf.reshape(nb, 1, FOX_HEADS), ret_s, rw_s, proj[:, COL_RW:COL_RW + RWKV_PROJ])
    return x, state


def kernel(x_prompt, x_sample, cache_k, cache_v, cache_logf, page_table, state_ret, state_rwkv, state_shift,
           meta_tokens, norm_mix, norm_ffn, norm_final, w_in, fox_forget_bias,
           rwkv_mu, rwkv_w0, rwkv_w2, rwkv_a0, rwkv_a2, rwkv_g2, rwkv_kk, rwkv_ka, rwkv_rk, rwkv_ln_w, rwkv_ln_b,
           w_branch_ret, w_branch_fox, w_branch_rwkv, w_out,
           router_group_w, router_group_b, router_expert_w, router_expert_b, expert_w1, expert_w3, expert_w2):
    batch, s_len, d = x_prompt.shape
    nb, n_new, _ = x_sample.shape
    assert n_new == 1 and d == D_MODEL and s_len % CHUNK == 0
    depth = w_in.shape[0]
    tp = PAD_FRONT + N_META + s_len
    past_len = page_table.shape[1] * cache_k.shape[2]
    valid = np.tile(np.arange(tp) >= PAD_FRONT, batch)
    xp = jnp.concatenate([jnp.zeros((batch, PAD_FRONT, d), F32),
                          jnp.broadcast_to(meta_tokens[None].astype(F32), (batch, N_META, d)), x_prompt], axis=1)
    xp = xp.reshape(batch * tp, d)
    xs = x_sample.reshape(nb, d)
    pos_s = jnp.full((1,), past_len, jnp.int32)
    outs_p = [[] for _ in range(6)]
    outs_s = [[] for _ in range(6)]
    for l in range(depth):
        lp = dict(norm_mix=norm_mix[l], norm_ffn=norm_ffn[l], w_in=_pack_w_in(w_in[l]), fox_b=fox_forget_bias[l],
                  mu=rwkv_mu[l], w0=rwkv_w0[l], w2=rwkv_w2[l], a0=rwkv_a0[l], a2=rwkv_a2[l], g2=rwkv_g2[l],
                  kk=rwkv_kk[l], ka=rwkv_ka[l], rk=rwkv_rk[l], ln_w=rwkv_ln_w[l], ln_b=rwkv_ln_b[l],
                  wb_ret=w_branch_ret[l].astype(BF16), wb_fox=w_branch_fox[l].astype(BF16),
                  wb_rwkv=w_branch_rwkv[l].astype(BF16), w_out=w_out[l].astype(BF16))
        moe = dict(wr=_router_weights(router_group_w[l], router_expert_w[l]), bg=router_group_b[l], be=router_expert_b[l],
                   w1=expert_w1[l].astype(BF16), w3=expert_w3[l].astype(BF16), w2=expert_w2[l].astype(BF16))
        xp, st = _prompt_layer(xp, lp, moe, batch, tp, valid)
        for j in range(6):
            outs_p[j].append(st[j])
        xs, st = _sample_layer(xs, lp, moe, l, cache_k, cache_v, cache_logf, page_table,
                               state_ret[l], state_rwkv[l], state_shift[l], pos_s)
        for j in range(6):
            outs_s[j].append(st[j])
    y_prompt = _final_norm_prompt(xp, norm_final, batch, tp)
    y_sample = _rmsnorm(xs, norm_final, F32).reshape(nb, 1, d)
    return (y_prompt, y_sample, *[jnp.stack(o, axis=0) for o in outs_p], *[jnp.stack(o, axis=0) for o in outs_s])
```

```python
import functools

import numpy as np
import jax
import jax.numpy as jnp
from jax import lax
from jax.experimental import pallas as pl
from jax.experimental.pallas import tpu as pltpu

F32 = jnp.float32
BF16 = jnp.bfloat16
HIGHEST = lax.Precision.HIGHEST

D_MODEL = 2048
N_META = 16
CHUNK = 128
PAD_FRONT = CHUNK - N_META
RMS_EPS = 1e-6
GN_EPS = 1e-5
RET_HEADS = 4
RET_DK = 256
RET_DV = 256
RET_W = RET_HEADS * RET_DK
ROPE_BASE = 10000.0
FOX_HEADS = 8
FOX_HD = 128
FOX_W = FOX_HEADS * FOX_HD
FOX_PAIR = 2
LOG2E = 1.4426950408889634
RWKV_HEADS = 16
RWKV_HD = 64
RWKV_W = RWKV_HEADS * RWKV_HD
RWKV_W_RANK = 64
RWKV_A_RANK = 64
RWKV_G_RANK = 128
RWKV_GN_EPS = 64e-5
RWKV_PROJ = 3 * RWKV_W + RWKV_W_RANK + RWKV_A_RANK + RWKV_G_RANK
RWKV_CHUNK = 64
RWKV_GROUP = 4
N_BRANCH = 3
N_GROUPS = 4
EXPERTS_PER_GROUP = 8
N_EXPERTS = N_GROUPS * EXPERTS_PER_GROUP
TOP_K = 2
EXPERT_FF = 1024
MOE_BLOCK = 128
ROUTER_COLS = 128

FF_PAD = 256
COL_RW = 0
COL_RQ = COL_RW + RWKV_PROJ
COL_RK = COL_RQ + RET_W
COL_RV = COL_RK + RET_W
COL_RG = COL_RV + RET_W
COL_FQ = COL_RG + RET_W
COL_FK = COL_FQ + FOX_W
COL_FV = COL_FK + FOX_W
COL_FF = COL_FV + FOX_W
COL_GATE = COL_FF + FF_PAD
PROJ_PACKED = COL_GATE + N_BRANCH * D_MODEL
SRC_RQ = 0
SRC_FF = 4 * RET_W + 3 * FOX_W
SRC_RW = SRC_FF + FOX_HEADS
SRC_GATE = SRC_RW + RWKV_PROJ

VMEM_LIMIT = 56 * 1024 * 1024
NEG_BIG = -1e30
_NT = (((1,), (1,)), ((), ()))
_TN = (((0,), (0,)), ((), ()))


def _cparams(*sem):
    return pltpu.CompilerParams(dimension_semantics=sem, vmem_limit_bytes=VMEM_LIMIT)


def _pick(n, prefs):
    for p in prefs:
        if n % p == 0:
            return p
    return n


def _rms(x, g):
    return x * lax.rsqrt(jnp.mean(x * x, axis=-1, keepdims=True) + RMS_EPS) * g


def _rms_body(x_ref, g_ref, o_ref):
    o_ref[...] = _rms(x_ref[...], g_ref[...]).astype(o_ref.dtype)


def _rmsnorm(x, g, out_dtype):
    n, d = x.shape
    tm = _pick(n, (256, 128, 8))
    return pl.pallas_call(
        _rms_body,
        grid=(n // tm,),
        in_specs=[pl.BlockSpec((tm, d), lambda i: (i, 0)), pl.BlockSpec((1, d), lambda i: (0, 0))],
        out_specs=pl.BlockSpec((tm, d), lambda i: (i, 0)),
        out_shape=jax.ShapeDtypeStruct((n, d), out_dtype),
        compiler_params=_cparams("parallel"),
        name="rmsnorm",
    )(x, g.reshape(1, d))


def _rms_router_body(x_ref, g_ref, wr_ref, h_ref, lg_ref):
    h = _rms(x_ref[...], g_ref[...])
    h_ref[...] = h
    lg_ref[...] = jnp.dot(h.astype(BF16), wr_ref[...].astype(BF16), preferred_element_type=F32)


def _rmsnorm_router(x, g, wr):
    n, d = x.shape
    tm = _pick(n, (256, 128, 8))
    return pl.pallas_call(
        _rms_router_body,
        grid=(n // tm,),
        in_specs=[pl.BlockSpec((tm, d), lambda i: (i, 0)), pl.BlockSpec((1, d), lambda i: (0, 0)),
                  pl.BlockSpec((d, ROUTER_COLS), lambda i: (0, 0))],
        out_specs=[pl.BlockSpec((tm, d), lambda i: (i, 0)), pl.BlockSpec((tm, ROUTER_COLS), lambda i: (i, 0))],
        out_shape=[jax.ShapeDtypeStruct((n, d), F32), jax.ShapeDtypeStruct((n, ROUTER_COLS), F32)],
        compiler_params=_cparams("parallel"),
        name="rmsnorm_router",
    )(x, g.reshape(1, d), wr)


def _final_norm_prompt(x, g, batch, tp):
    d = x.shape[1]
    nb = tp // CHUNK
    return pl.pallas_call(
        _rms_body,
        grid=(batch, nb - 1),
        in_specs=[pl.BlockSpec((CHUNK, d), lambda b, j: (b * nb + 1 + j, 0)), pl.BlockSpec((1, d), lambda b, j: (0, 0))],
        out_specs=pl.BlockSpec((CHUNK, d), lambda b, j: (b * (nb - 1) + j, 0)),
        out_shape=jax.ShapeDtypeStruct((batch * (tp - CHUNK), d), F32),
        compiler_params=_cparams("parallel", "parallel"),
        name="final_norm",
    )(x, g.reshape(1, d)).reshape(batch, tp - CHUNK, d)


def _mm_body(a_ref, w_ref, o_ref):
    o_ref[...] = jnp.dot(a_ref[...], w_ref[...], preferred_element_type=F32).astype(o_ref.dtype)


def _matmul(a, w, out_dtype=F32, tm_prefs=(1408, 768, 512, 384, 256, 128), tn_prefs=(512, 256, 128)):
    m, k = a.shape
    n = w.shape[1]
    tm = _pick(m, tm_prefs)
    tn = _pick(n, tn_prefs)
    return pl.pallas_call(
        _mm_body,
        grid=(m // tm, n // tn),
        in_specs=[pl.BlockSpec((tm, k), lambda i, j: (i, 0)), pl.BlockSpec((k, tn), lambda i, j: (0, j))],
        out_specs=pl.BlockSpec((tm, tn), lambda i, j: (i, j)),
        out_shape=jax.ShapeDtypeStruct((m, n), out_dtype),
        compiler_params=_cparams("parallel", "arbitrary"),
        name="matmul",
    )(a, w)


def _mm_res_body(a_ref, w_ref, r_ref, o_ref, *, blocks_per_seq, pad):
    y = r_ref[...] + jnp.dot(a_ref[...], w_ref[...], preferred_element_type=F32)
    if pad:
        first = (pl.program_id(0) % blocks_per_seq) == 0
        row = lax.broadcasted_iota(jnp.int32, y.shape, 0)
        y = jnp.where(jnp.logical_and(first, row < pad), 0.0, y)
    o_ref[...] = y


def _matmul_residual(a, w, res, tp=None):
    m, k = a.shape
    n = w.shape[1]
    tm = _pick(tp, (768, 384, 128)) if tp else m
    assert m % tm == 0
    tn = _pick(n, (512, 256, 128))
    body = functools.partial(_mm_res_body, blocks_per_seq=(tp // tm if tp else 1), pad=(PAD_FRONT if tp else 0))
    return pl.pallas_call(
        body,
        grid=(m // tm, n // tn),
        in_specs=[pl.BlockSpec((tm, k), lambda i, j: (i, 0)), pl.BlockSpec((k, tn), lambda i, j: (0, j)),
                  pl.BlockSpec((tm, tn), lambda i, j: (i, j))],
        out_specs=pl.BlockSpec((tm, tn), lambda i, j: (i, j)),
        out_shape=jax.ShapeDtypeStruct((m, n), F32),
        compiler_params=_cparams("parallel", "arbitrary"),
        name="matmul_residual",
    )(a, w, res)


def _rope_halves(x, cos, sin):
    half = x.shape[-1] // 2
    x1, x2 = x[:, :half], x[:, half:]
    return jnp.concatenate([x1 * cos - x2 * sin, x1 * sin + x2 * cos], axis=-1)


def _head_norm(y, eps):
    mu = jnp.mean(y, axis=-1, keepdims=True)
    yc = y - mu
    return yc * lax.rsqrt(jnp.mean(yc * yc, axis=-1, keepdims=True) + eps)


def _silu(x):
    return x / (1.0 + jnp.exp(-x))


def _sigmoid(x):
    return 1.0 / (1.0 + jnp.exp(-x))


def _ret_body(q_ref, k_ref, v_ref, g_ref, cos_ref, sin_ref, dm_ref, cd_ref, kd_ref, sd_ref, o_ref, s_ref):
    c = pl.program_id(2)

    @pl.when(c == 0)
    def _():
        s_ref[...] = jnp.zeros(s_ref.shape, F32)

    cos = cos_ref[...]
    sin = sin_ref[...]
    q = _rope_halves(q_ref[...], cos, sin)
    k = _rope_halves(k_ref[...], cos, sin) * (RET_DK ** -0.5)
    qb = q.astype(BF16)
    kb = k.astype(BF16)
    vb = v_ref[...].astype(BF16)
    s_old = s_ref[0, 0]
    scores = lax.dot_general(qb, kb, (((1,), (1,)), ((), ())), preferred_element_type=F32) * dm_ref[0]
    intra = jnp.dot(scores.astype(BF16), vb, preferred_element_type=F32)
    cross = jnp.dot(qb, s_old.astype(BF16), preferred_element_type=F32) * cd_ref[0]
    kdec = (k * kd_ref[0]).astype(BF16)
    s_ref[0, 0] = sd_ref[0] * s_old + lax.dot_general(kdec, vb, (((0,), (0,)), ((), ())), preferred_element_type=F32)
    o = _head_norm(intra + cross, GN_EPS) * _silu(g_ref[...])
    o_ref[...] = o.astype(o_ref.dtype)


def _ret_tables(length):
    lg = jnp.log1p(-jnp.power(2.0, -5.0 - jnp.arange(RET_HEADS, dtype=F32)))
    i = jnp.arange(length, dtype=F32)
    diff = i[:, None] - i[None, :]
    dmask = jnp.where(diff >= 0, jnp.exp(lg[:, None, None] * jnp.maximum(diff, 0.0)), 0.0)
    cdec = jnp.exp(lg[:, None] * (i + 1.0)[None, :])
    kdec = jnp.exp(lg[:, None] * (length - 1.0 - i)[None, :])
    sdec = jnp.exp(lg * length)
    return lg, dmask, cdec, kdec, sdec


def _rope_tables(pos):
    half = RET_DK // 2
    inv = ROPE_BASE ** (-jnp.arange(half, dtype=F32) / half)
    ang = pos.astype(F32)[:, None] * inv[None, :]
    return jnp.cos(ang), jnp.sin(ang)


def _retention_prompt(proj, batch, tp):
    n = proj.shape[0]
    nc = tp // CHUNK
    _, dmask, cdec, kdec, sdec = _ret_tables(CHUNK)
    cdec = jnp.broadcast_to(cdec[:, :, None], (RET_HEADS, CHUNK, RET_DV))
    kdec = jnp.broadcast_to(kdec[:, :, None], (RET_HEADS, CHUNK, RET_DK))
    sdec = jnp.broadcast_to(sdec[:, None, None], (RET_HEADS, 1, RET_DV))
    cos, sin = _rope_tables(jnp.arange(tp, dtype=jnp.int32) - PAD_FRONT)
    blk = lambda col: pl.BlockSpec((CHUNK, RET_DK), lambda b, h, c, col=col: (b * nc + c, col // RET_DK + h))
    tab = lambda shape: pl.BlockSpec((1,) + shape, lambda b, h, c: (h, 0, 0))
    rope = pl.BlockSpec((CHUNK, RET_DK // 2), lambda b, h, c: (c, 0))
    return pl.pallas_call(
        _ret_body,
        grid=(batch, RET_HEADS, nc),
        in_specs=[blk(COL_RQ), blk(COL_RK), blk(COL_RV), blk(COL_RG), rope, rope,
                  tab((CHUNK, CHUNK)), tab((CHUNK, RET_DV)), tab((CHUNK, RET_DK)), tab((1, RET_DV))],
        out_specs=[pl.BlockSpec((CHUNK, RET_DV), lambda b, h, c: (b * nc + c, h)),
                   pl.BlockSpec((1, 1, RET_DK, RET_DV), lambda b, h, c: (b, h, 0, 0))],
        out_shape=[jax.ShapeDtypeStruct((n, RET_W), BF16), jax.ShapeDtypeStruct((batch, RET_HEADS, RET_DK, RET_DV), F32)],
        compiler_params=_cparams("parallel", "parallel", "arbitrary"),
        name="retention_prompt",
    )(proj, proj, proj, proj, cos, sin, dmask, cdec, kdec, sdec)


def _fox_body(qi_ref, ki_ref, q_ref, k_ref, v_ref, qx_ref, kx_ref, o_ref, m_sc, l_sc, acc_sc, *, tq, tk, hp):
    step = pl.program_id(2)
    qi = qi_ref[step]
    ki = ki_ref[step]

    @pl.when(ki == 0)
    def _():
        m_sc[...] = jnp.full(m_sc.shape, NEG_BIG, F32)
        l_sc[...] = jnp.zeros(l_sc.shape, F32)
        acc_sc[...] = jnp.zeros(acc_sc.shape, F32)

    def update(masked):
        if masked:
            qpos = qi * tq + lax.broadcasted_iota(jnp.int32, (tq, tk), 0)
            kpos = ki * tk + lax.broadcasted_iota(jnp.int32, (tq, tk), 1)
            valid = jnp.logical_and(kpos <= qpos, kpos >= PAD_FRONT)
        m_old, l_old, acc_old = m_sc[...], l_sc[...], acc_sc[...]
        m_out, l_out, acc_out = [], [], []
        for h in range(hp):
            sl = slice(h * FOX_HD, (h + 1) * FOX_HD)
            qa = jnp.concatenate([q_ref[:, sl].astype(BF16), qx_ref[0, h]], axis=1)
            ka = jnp.concatenate([k_ref[:, sl].astype(BF16), kx_ref[0, h]], axis=1)
            s = lax.dot_general(qa, ka, _NT, preferred_element_type=F32) * (FOX_HD ** -0.5 * LOG2E)
            if masked:
                s = jnp.where(valid, s, NEG_BIG)
            m_new = jnp.maximum(m_old[h], jnp.max(s, axis=1, keepdims=True))
            alpha = jnp.exp2(m_old[h] - m_new)
            p = jnp.exp2(s - m_new)
            m_out.append(m_new)
            l_out.append(alpha * l_old[h] + jnp.sum(p, axis=1, keepdims=True))
            acc_out.append(alpha * acc_old[:, sl] + jnp.dot(p.astype(BF16), v_ref[:, sl].astype(BF16),
                                                            preferred_element_type=F32))
        for h in range(hp):
            m_sc[h] = m_out[h]
            l_sc[h] = l_out[h]
            acc_sc[:, h * FOX_HD:(h + 1) * FOX_HD] = acc_out[h]

    edge = jnp.logical_or(ki == qi, ki == 0)
    pl.when(edge)(functools.partial(update, True))
    pl.when(jnp.logical_not(edge))(functools.partial(update, False))

    @pl.when(ki == qi)
    def _():
        for h in range(hp):
            sl = slice(h * FOX_HD, (h + 1) * FOX_HD)
            o_ref[:, sl] = (acc_sc[:, sl] / l_sc[h]).astype(o_ref.dtype)


def _split3(x):
    hi = x.astype(BF16)
    r1 = x - hi.astype(F32)
    mid = r1.astype(BF16)
    return hi, mid, (r1 - mid.astype(F32)).astype(BF16)


def _fox_prompt(proj, c, batch, tp):
    n = proj.shape[0]
    tq = _pick(tp, (384, 256, 128))
    nq = tp // tq
    hp = FOX_PAIR
    pairs = [(i, j) for i in range(nq) for j in range(i + 1)]
    qi_tab = jnp.asarray(np.array([p[0] for p in pairs], np.int32))
    ki_tab = jnp.asarray(np.array([p[1] for p in pairs], np.int32))
    hi, mid, lo = _split3(c * (FOX_HD ** 0.5))
    one = jnp.ones_like(hi)
    fill = jnp.zeros(c.shape + (FOX_HD - 6,), BF16)
    qx = jnp.concatenate([jnp.stack([hi, mid, lo, one, one, one], axis=-1), fill], axis=-1)
    kx = jnp.concatenate([jnp.stack([one, one, one, -hi, -mid, -lo], axis=-1), fill], axis=-1)
    cb = lambda col: col // (hp * FOX_HD)
    wide = hp * FOX_HD
    grid_spec = pltpu.PrefetchScalarGridSpec(
        num_scalar_prefetch=2,
        grid=(batch, FOX_HEADS // hp, len(pairs)),
        in_specs=[
            pl.BlockSpec((tq, wide), lambda b, h, s, qi, ki: (b * nq + qi[s], cb(COL_FQ) + h)),
            pl.BlockSpec((tq, wide), lambda b, h, s, qi, ki: (b * nq + ki[s], cb(COL_FK) + h)),
            pl.BlockSpec((tq, wide), lambda b, h, s, qi, ki: (b * nq + ki[s], cb(COL_FV) + h)),
            pl.BlockSpec((1, hp, tq, FOX_HD), lambda b, h, s, qi, ki: (b, h, qi[s], 0)),
            pl.BlockSpec((1, hp, tq, FOX_HD), lambda b, h, s, qi, ki: (b, h, ki[s], 0)),
        ],
        out_specs=pl.BlockSpec((tq, wide), lambda b, h, s, qi, ki: (b * nq + qi[s], h)),
        scratch_shapes=[pltpu.VMEM((hp, tq, 1), F32), pltpu.VMEM((hp, tq, 1), F32), pltpu.VMEM((tq, wide), F32)],
    )
    return pl.pallas_call(
        functools.partial(_fox_body, tq=tq, tk=tq, hp=hp),
        grid_spec=grid_spec,
        out_shape=jax.ShapeDtypeStruct((n, FOX_W), BF16),
        compiler_params=_cparams("parallel", "parallel", "arbitrary"),
        name="fox_prompt",
    )(qi_tab, ki_tab, proj, proj, proj, qx, kx)


def _softplus(z):
    return jnp.maximum(z, 0.0) + jnp.log1p(jnp.exp(-jnp.abs(z)))


def _rwkv_prep_math(c, prev, mu, w0, w2, a0, a2, g2, kkp, ka, exact):
    w = RWKV_W
    xm = c + mu * (prev - c)
    r, k, v = xm[:, 0:w], xm[:, w:2 * w], xm[:, 2 * w:3 * w]
    wd = xm[:, 3 * w:3 * w + RWKV_W_RANK]
    ad = xm[:, 3 * w + RWKV_W_RANK:3 * w + RWKV_W_RANK + RWKV_A_RANK]
    gd = xm[:, 3 * w + RWKV_W_RANK + RWKV_A_RANK:]
    if exact:
        mm = lambda x, m: jnp.dot(x, m, precision=HIGHEST, preferred_element_type=F32)
    else:
        mm = lambda x, m: jnp.dot(x.astype(BF16), m.astype(BF16), preferred_element_type=F32)
    w_log = -_softplus(-(w0 + mm(jnp.tanh(wd), w2))) - 0.5
    lw = -jnp.exp(w_log)
    a = _sigmoid(a0 + mm(ad, a2))
    g = mm(_sigmoid(gd), g2)
    kk0 = k * kkp
    kmod = k * (1.0 + (a - 1.0) * ka)
    return r, kmod, v, lw, kk0, a, g


def _rwkv_prep_body(c_ref, mu_ref, w0_ref, w2_ref, a0_ref, a2_ref, g2_ref, kkp_ref, ka_ref,
                    r_ref, k_ref, v_ref, lw_ref, kk_ref, a_ref, g_ref, carry):
    t = pl.program_id(1)

    @pl.when(t == 0)
    def _():
        carry[...] = jnp.zeros(carry.shape, F32)

    c = c_ref[...]
    rows = c.shape[0]
    prev = pltpu.roll(c, 1, axis=0)
    row = lax.broadcasted_iota(jnp.int32, c.shape, 0)
    prev = jnp.where(row == 0, carry[...], prev)
    carry[...] = c[rows - 1:rows, :]
    outs = _rwkv_prep_math(c, prev, mu_ref[...], w0_ref[...], w2_ref[...], a0_ref[...], a2_ref[...], g2_ref[...],
                           kkp_ref[...], ka_ref[...], exact=False)
    for ref, val in zip((r_ref, k_ref, v_ref, lw_ref, kk_ref, a_ref, g_ref), outs):
        ref[...] = val


def _rwkv_prep_prompt(proj, lp, batch, tp):
    n = proj.shape[0]
    tb = CHUNK
    nt = tp // tb
    row = lambda x: x.reshape(1, -1)
    full = lambda shape: pl.BlockSpec(shape, lambda b, t: (0, 0))
    out_spec = pl.BlockSpec((tb, RWKV_W), lambda b, t: (b * nt + t, 0))
    return pl.pallas_call(
        _rwkv_prep_body,
        grid=(batch, nt),
        in_specs=[pl.BlockSpec((tb, RWKV_PROJ), lambda b, t: (b * nt + t, COL_RW // RWKV_PROJ)),
                  full((1, RWKV_PROJ)), full((1, RWKV_W)), full((RWKV_W_RANK, RWKV_W)), full((1, RWKV_W)),
                  full((RWKV_A_RANK, RWKV_W)), full((RWKV_G_RANK, RWKV_W)), full((1, RWKV_W)), full((1, RWKV_W))],
        out_specs=[out_spec] * 7,
        out_shape=[jax.ShapeDtypeStruct((n, RWKV_W), F32)] * 7,
        scratch_shapes=[pltpu.VMEM((1, RWKV_PROJ), F32)],
        compiler_params=_cparams("parallel", "arbitrary"),
        name="rwkv_prep",
    )(proj, row(lp['mu']), row(lp['w0']), lp['w2'], row(lp['a0']), lp['a2'], lp['g2'], row(lp['kk']), row(lp['ka']))


def _rwkv_chunk_body(r_ref, k_ref, v_ref, lw_ref, kk_ref, a_ref, g_ref, rk_ref, lnw_ref, lnb_ref, o_ref, s_ref):
    cn = pl.program_id(1)

    @pl.when(cn == 0)
    def _():
        s_ref[...] = jnp.zeros(s_ref.shape, F32)

    cs = RWKV_CHUNK
    hd = RWKV_HD
    gh = RWKV_GROUP
    n = gh * cs
    gw = gh * hd
    bits = int(np.log2(cs))
    row = lax.broadcasted_iota(jnp.int32, (n, n), 0)
    col = lax.broadcasted_iota(jnp.int32, (n, n), 1)
    same_head = (row >> bits) == (col >> bits)
    strict = same_head & (col < row)
    incl = same_head & (col <= row)
    eye = (row == col).astype(F32)
    pair_masks = [((row >> (bit + 1)) == (col >> (bit + 1))) & ((row & (1 << bit)) != 0) & ((col & (1 << bit)) == 0)
                  for bit in range(bits)]
    trow = lax.broadcasted_iota(jnp.int32, (cs, cs), 0)
    tcol = lax.broadcasted_iota(jnp.int32, (cs, cs), 1)
    cum_all = jnp.dot((tcol <= trow).astype(F32), lw_ref[...], precision=HIGHEST, preferred_element_type=F32)
    decay_all = jnp.exp(cum_all[cs - 1:cs, :])
    tail_all = jnp.exp(cum_all[cs - 1:cs, :] - cum_all)
    dot = lambda x, y: jnp.dot(x.astype(BF16), y.astype(BF16), preferred_element_type=F32)
    dot_nt = lambda x, y: lax.dot_general(x.astype(BF16), y.astype(BF16), _NT, preferred_element_type=F32)
    dot_tn = lambda x, y: lax.dot_general(x.astype(BF16), y.astype(BF16), _TN, preferred_element_type=F32)
    for g in range(RWKV_HEADS // gh):
        lanes = slice(g * gw, (g + 1) * gw)
        stack = lambda x: jnp.concatenate([x[:, g * gw + h * hd:g * gw + (h + 1) * hd] for h in range(gh)], axis=0)
        rows = lambda x: jnp.concatenate([jnp.broadcast_to(x[:, g * gw + h * hd:g * gw + (h + 1) * hd], (cs, hd))
                                          for h in range(gh)], axis=0)
        lw, cum, tail = stack(lw_ref[...]), stack(cum_all), stack(tail_all)
        kk0 = stack(kk_ref[...])
        kk = kk0 * lax.rsqrt(jnp.sum(kk0 * kk0, axis=-1, keepdims=True) + 1e-12)
        b = kk * stack(a_ref[...])
        r, k, v = stack(r_ref[...]), stack(k_ref[...]), stack(v_ref[...])
        e_in = jnp.exp(cum)
        e_neg = jnp.exp(-cum)
        ar = jnp.concatenate([-kk * jnp.exp(cum - lw), r * e_in], axis=0)
        bk = jnp.concatenate([b * e_neg, k * e_neg], axis=0)
        gram = dot_nt(ar, bk)
        l_ab = jnp.where(strict, gram[:n, :n], 0.0)
        l_ak = jnp.where(strict, gram[:n, n:], 0.0)
        m_rbk = jnp.concatenate([jnp.where(incl, gram[n:, :n], 0.0), jnp.where(incl, gram[n:, n:], 0.0)], axis=1)
        tinv = eye + jnp.where(pair_masks[0], l_ab, 0.0)
        for mask in pair_masks[1:]:
            tinv = tinv + dot(dot(tinv, jnp.where(mask, l_ab, 0.0)), tinv)
        s_old = s_ref[0, g * n:(g + 1) * n, :]
        ars = dot_nt(ar, s_old)
        v_bd = jnp.where(same_head, jnp.concatenate([v_ref[:, lanes]] * gh, axis=0), 0.0)
        u = dot(tinv, jnp.where(same_head, ars[:n], 0.0) + dot(l_ak, v_bd))
        uv = jnp.concatenate([u, v_bd], axis=0)
        y_bd = jnp.where(same_head, ars[n:], 0.0) + dot(m_rbk, uv)
        s_ref[0, g * n:(g + 1) * n, :] = s_old * rows(decay_all) + dot_tn(uv, jnp.concatenate([b * tail, k * tail], axis=0))
        y = sum(y_bd[:, h * hd:(h + 1) * hd] for h in range(gh))
        yn = _head_norm(y, RWKV_GN_EPS) * rows(lnw_ref[...]) + rows(lnb_ref[...])
        bonus = jnp.sum(r * k * rows(rk_ref[...]), axis=-1, keepdims=True) * v
        out = (yn + bonus) * stack(g_ref[...])
        o_ref[:, lanes] = jnp.concatenate([out[h * cs:(h + 1) * cs] for h in range(gh)], axis=1).astype(o_ref.dtype)


def _rwkv_chunk_prompt(prep, lp, batch, tp):
    n = prep[0].shape[0]
    cs = RWKV_CHUNK
    ncn = tp // cs
    row = lambda x: x.reshape(1, -1)
    blk = pl.BlockSpec((cs, RWKV_W), lambda b, c: (b * ncn + c, 0))
    full = pl.BlockSpec((1, RWKV_W), lambda b, c: (0, 0))
    out, state = pl.pallas_call(
        _rwkv_chunk_body,
        grid=(batch, ncn),
        in_specs=[blk] * 7 + [full] * 3,
        out_specs=[blk, pl.BlockSpec((1, RWKV_W, RWKV_HD), lambda b, c: (b, 0, 0))],
        out_shape=[jax.ShapeDtypeStruct((n, RWKV_W), BF16), jax.ShapeDtypeStruct((batch, RWKV_W, RWKV_HD), F32)],
        compiler_params=_cparams("parallel", "arbitrary"),
        name="rwkv_chunk",
    )(*prep, row(lp['rk']), row(lp['ln_w']), row(lp['ln_b']))
    return out, state.reshape(batch, RWKV_HEADS, RWKV_HD, RWKV_HD)


def _merge_body(ro_ref, fo_ref, wo_ref, wr_ref, wf_ref, ww_ref, g0_ref, g1_ref, g2_ref, o_ref):
    dot = functools.partial(jnp.dot, preferred_element_type=F32)
    m = (_sigmoid(g0_ref[...]) * dot(ro_ref[...], wr_ref[...])
         + _sigmoid(g1_ref[...]) * dot(fo_ref[...], wf_ref[...])
         + _sigmoid(g2_ref[...]) * dot(wo_ref[...], ww_ref[...]))
    o_ref[...] = m.astype(o_ref.dtype)


def _merge(ret_o, fox_o, rw_o, wb_ret, wb_fox, wb_rwkv, gates, gate_col, out_dtype):
    m = ret_o.shape[0]
    tm = _pick(m, (768, 384, 128))
    tn = 512
    nj = D_MODEL // tn
    act = lambda width: pl.BlockSpec((tm, width), lambda i, j: (i, 0))
    wgt = lambda width: pl.BlockSpec((width, tn), lambda i, j: (0, j))
    gate = lambda br: pl.BlockSpec((tm, tn), lambda i, j, br=br: (i, gate_col // tn + br * nj + j))
    return pl.pallas_call(
        _merge_body,
        grid=(m // tm, nj),
        in_specs=[act(RET_W), act(FOX_W), act(RWKV_W), wgt(RET_W), wgt(FOX_W), wgt(RWKV_W), gate(0), gate(1), gate(2)],
        out_specs=pl.BlockSpec((tm, tn), lambda i, j: (i, j)),
        out_shape=jax.ShapeDtypeStruct((m, D_MODEL), out_dtype),
        compiler_params=_cparams("parallel", "arbitrary"),
        name="merge",
    )(ret_o, fox_o, rw_o, wb_ret, wb_fox, wb_rwkv, gates, gates, gates)


def _cast_body(x_ref, o_ref):
    o_ref[...] = x_ref[...].astype(o_ref.dtype)


def _expert_weights_bf16(w, layer):
    _, e, r, c = w.shape
    return pl.pallas_call(
        _cast_body,
        grid=(e,),
        in_specs=[pl.BlockSpec((None, 1, r, c), lambda i: (layer, i, 0, 0))],
        out_specs=pl.BlockSpec((1, r, c), lambda i: (i, 0, 0)),
        out_shape=jax.ShapeDtypeStruct((e, r, c), BF16),
        compiler_params=_cparams("parallel"),
        name="expert_weights_bf16",
    )(w)


def _route(logits, bg, be):
    n = logits.shape[0]
    gp = jax.nn.softmax(logits[:, :N_GROUPS] + bg.astype(F32), axis=-1)
    gidx = jnp.argmax(gp, axis=-1)
    pg = jnp.take_along_axis(gp, gidx[:, None], axis=-1)
    el = logits[:, N_GROUPS:N_GROUPS + N_EXPERTS].reshape(n, N_GROUPS, EXPERTS_PER_GROUP) + be.astype(F32)[None]
    el = jnp.take_along_axis(el, gidx[:, None, None], axis=1)[:, 0]
    topv, topi = lax.top_k(jax.nn.softmax(el, axis=-1), TOP_K)
    gate = pg * topv / jnp.sum(topv, axis=-1, keepdims=True)
    eid = (gidx[:, None] * EXPERTS_PER_GROUP + topi).astype(jnp.int32)
    return eid, gate


def _router_weights(wg, we):
    d = wg.shape[0]
    wr = jnp.concatenate([wg, jnp.transpose(we, (1, 0, 2)).reshape(d, N_EXPERTS)], axis=1)
    return jnp.pad(wr, ((0, 0), (0, ROUTER_COLS - wr.shape[1])))


def _moe_body(be_ref, nused_ref, tok_ref, dst_ref, h_hbm, w1_ref, w3_ref, w2_ref, y_in, y_hbm, xbuf, ybuf, sem_in, sem_out):
    del y_in
    i = pl.program_id(0)
    n_used = nused_ref[0]
    slot = i % 2
    rows = range(MOE_BLOCK)

    def gather(blk, buf, r):
        return pltpu.make_async_copy(h_hbm.at[pl.ds(tok_ref[blk * MOE_BLOCK + r], 1)], xbuf.at[buf, pl.ds(r, 1)],
                                     sem_in.at[buf])

    def scatter(blk, r):
        return pltpu.make_async_copy(ybuf.at[pl.ds(r, 1)], y_hbm.at[pl.ds(dst_ref[blk * MOE_BLOCK + r], 1)], sem_out)

    @pl.when(jnp.logical_and(i == 0, n_used > 0))
    def _():
        for r in rows:
            gather(0, 0, r).start()

    @pl.when(i < n_used)
    def _():
        for r in rows:
            gather(i, slot, r).wait()
        nxt = jnp.minimum(i + 1, n_used - 1)
        for r in rows:
            gather(nxt, 1 - slot, r).start()
        x = xbuf[slot].astype(BF16)
        a = jnp.dot(x, w1_ref[0], preferred_element_type=F32)
        b = jnp.dot(x, w3_ref[0], preferred_element_type=F32)
        y = jnp.dot((_silu(a) * b).astype(BF16), w2_ref[0], preferred_element_type=F32)

        @pl.when(i > 0)
        def _():
            for r in rows:
                scatter(i - 1, r).wait()

        ybuf[...] = y
        for r in rows:
            scatter(i, r).start()

        @pl.when(i == n_used - 1)
        def _():
            for r in rows:
                gather(nxt, 1 - slot, r).wait()
                scatter(i, r).wait()


def _moe_prompt(h, eid, gate, valid, w1, w3, w2):
    n, d = h.shape
    a_tot = n * TOP_K
    n_real = int(np.sum(valid)) * TOP_K
    n_blk = (n_real + N_EXPERTS * (MOE_BLOCK - 1) + MOE_BLOCK - 1) // MOE_BLOCK
    cap = n_blk * MOE_BLOCK
    validf = jnp.repeat(jnp.asarray(valid), TOP_K)
    eflat = eid.reshape(-1)
    onehot = jnp.logical_and(eflat[:, None] == jnp.arange(N_EXPERTS, dtype=jnp.int32)[None, :], validf[:, None]).astype(jnp.int32)
    blocks = onehot.astype(F32).reshape(a_tot // CHUNK, CHUNK, N_EXPERTS)
    below = jnp.tril(jnp.ones((CHUNK, CHUNK), F32), -1)
    inner = jnp.einsum('ij,bjk->bik', below, blocks)
    totals = jnp.sum(blocks, axis=1)
    offset = jnp.cumsum(totals, axis=0) - totals
    before = (inner + offset[:, None, :]).reshape(a_tot, N_EXPERTS).astype(jnp.int32)
    rank = jnp.sum(before * onehot, axis=1)
    counts = jnp.sum(onehot, axis=0)
    padded = (counts + MOE_BLOCK - 1) // MOE_BLOCK * MOE_BLOCK
    pad_end = jnp.cumsum(padded)
    pad_start = pad_end - padded
    dest = jnp.where(validf, pad_start[eflat] + rank, cap)
    assign = jnp.arange(a_tot, dtype=jnp.int32)
    zero_row = int(np.argmin(valid))
    tok_buf = jnp.full((cap,), zero_row, jnp.int32).at[dest].set(assign // TOP_K, mode='drop')
    dump = a_tot + (jnp.arange(cap, dtype=jnp.int32) % MOE_BLOCK)
    dst_buf = dump.at[dest].set((assign % TOP_K) * n + assign // TOP_K, mode='drop')
    n_used = (pad_end[-1] // MOE_BLOCK).astype(jnp.int32)
    blk = jnp.arange(n_blk, dtype=jnp.int32)
    blk_e = jnp.minimum(jnp.searchsorted(pad_end, blk * MOE_BLOCK, side='right'), N_EXPERTS - 1).astype(jnp.int32)
    blk_e = jnp.where(blk < n_used, blk_e, blk_e[jnp.maximum(n_used - 1, 0)])
    y_rows = a_tot + MOE_BLOCK
    y0 = jnp.zeros((y_rows, d), F32)
    ff = w1.shape[2]
    grid_spec = pltpu.PrefetchScalarGridSpec(
        num_scalar_prefetch=4,
        grid=(n_blk,),
        in_specs=[pl.BlockSpec(memory_space=pl.ANY),
                  pl.BlockSpec((1, d, ff), lambda i, be, nu, tk, ds: (be[i], 0, 0)),
                  pl.BlockSpec((1, d, ff), lambda i, be, nu, tk, ds: (be[i], 0, 0)),
                  pl.BlockSpec((1, ff, d), lambda i, be, nu, tk, ds: (be[i], 0, 0)),
                  pl.BlockSpec(memory_space=pl.ANY)],
        out_specs=pl.BlockSpec(memory_space=pl.ANY),
        scratch_shapes=[pltpu.VMEM((2, MOE_BLOCK, d), F32), pltpu.VMEM((MOE_BLOCK, d), F32),
                        pltpu.SemaphoreType.DMA((2,)), pltpu.SemaphoreType.DMA(())],
    )
    y2 = pl.pallas_call(
        _moe_body,
        grid_spec=grid_spec,
        out_shape=jax.ShapeDtypeStruct((y_rows, d), F32),
        input_output_aliases={8: 0},
        compiler_params=_cparams("arbitrary"),
        name="moe_experts",
    )(blk_e, n_used.reshape(1), tok_buf, dst_buf, h, w1, w3, w2, y0)
    return y2


def _combine_body(x_ref, y0_ref, y1_ref, g_ref, o_ref):
    g = g_ref[...]
    o_ref[...] = x_ref[...] + (y0_ref[...] * g[:, 0:1] + y1_ref[...] * g[:, 1:2])


def _moe_combine(x, y2, gate):
    n, d = x.shape
    tm = _pick(n, (256, 128, 8))
    gpad = jnp.pad(gate, ((0, 0), (0, 128 - TOP_K)))
    return pl.pallas_call(
        _combine_body,
        grid=(n // tm,),
        in_specs=[pl.BlockSpec((tm, d), lambda i: (i, 0)), pl.BlockSpec((tm, d), lambda i: (i, 0)),
                  pl.BlockSpec((tm, d), lambda i: (n // tm + i, 0)), pl.BlockSpec((tm, 128), lambda i: (i, 0))],
        out_specs=pl.BlockSpec((tm, d), lambda i: (i, 0)),
        out_shape=jax.ShapeDtypeStruct((n, d), F32),
        compiler_params=_cparams("parallel"),
        name="moe_combine",
    )(x, y2, y2, gpad)


def _pack_w_in(w_in):
    ff = jnp.pad(w_in[:, SRC_FF:SRC_FF + FOX_HEADS], ((0, 0), (0, FF_PAD - FOX_HEADS)))
    return jnp.concatenate([w_in[:, SRC_RW:SRC_RW + RWKV_PROJ], w_in[:, :SRC_FF], ff, w_in[:, SRC_GATE:]],
                           axis=1).astype(BF16)


def _prompt_layer(x, lp, moe, batch, tp, valid):
    h = _rmsnorm(x, lp['norm_mix'], BF16)
    proj = _matmul(h, lp['w_in'])
    ret_o, ret_s = _retention_prompt(proj, batch, tp)
    ff = proj[:, COL_FF:COL_FF + FOX_HEADS].reshape(batch, tp, FOX_HEADS)
    logf = jax.nn.log_sigmoid(ff + lp['fox_b'].astype(F32))
    c = jnp.cumsum(logf, axis=1).transpose(0, 2, 1)
    fox_o = _fox_prompt(proj, c, batch, tp)
    prep = _rwkv_prep_prompt(proj, lp, batch, tp)
    rw_o, rw_s = _rwkv_chunk_prompt(prep, lp, batch, tp)
    merged = _merge(ret_o, fox_o, rw_o, lp['wb_ret'], lp['wb_fox'], lp['wb_rwkv'], proj, COL_GATE, BF16)
    x = _matmul_residual(merged, lp['w_out'], x, tp=tp)
    h2, logits = _rmsnorm_router(x, lp['norm_ffn'], moe['wr'])
    eid, gate = _route(logits, moe['bg'], moe['be'])
    y2 = _moe_prompt(h2, eid, gate, valid, moe['w1'], moe['w3'], moe['w2'])
    x = _moe_combine(x, y2, gate)
    p3 = proj.reshape(batch, tp, PROJ_PACKED)
    heads = lambda col: p3[:, PAD_FRONT:, col:col + FOX_W].reshape(batch, tp - PAD_FRONT, FOX_HEADS, FOX_HD)
    state = (heads(COL_FK), heads(COL_FV), logf[:, PAD_FRONT:], ret_s, rw_s, p3[:, tp - 1, COL_RW:COL_RW + RWKV_PROJ])
    return x, state


def _rows16(x):
    return jnp.concatenate([x.astype(BF16), jnp.zeros(x.shape, BF16)], axis=0)


def _mm_sample(x, w):
    return _matmul(_rows16(x), w, tn_prefs=(1536, 1024, 512, 256, 128))[:x.shape[0]]


def _pad8(x):
    first = lax.broadcasted_iota(jnp.int32, (8, x.shape[1]), 0) == 0
    return jnp.where(first, jnp.broadcast_to(x, (8, x.shape[1])), 0.0)


def _ret_sample_body(q_ref, k_ref, v_ref, g_ref, cos_ref, sin_ref, dec_ref, s0_ref, o_ref, s_ref):
    b = pl.program_id(1)
    row1 = lambda ref: ref[pl.ds(b, 1), :]
    cos = cos_ref[...]
    sin = sin_ref[...]
    q = _rope_halves(row1(q_ref), cos, sin)
    k = _rope_halves(row1(k_ref), cos, sin) * (RET_DK ** -0.5)
    v = row1(v_ref)
    dec = dec_ref[0]
    s0 = s0_ref[0, 0]
    rnd = lambda x: x.astype(BF16).astype(F32)
    cross = jnp.dot(_pad8(q).astype(BF16), s0.astype(BF16), preferred_element_type=F32)[0:1] * dec
    intra = jnp.sum(rnd(q) * rnd(k), axis=-1, keepdims=True) * v
    s_ref[0, 0] = dec * s0 + lax.dot_general(_pad8(k), _pad8(v), _TN, precision=HIGHEST, preferred_element_type=F32)
    o_ref[pl.ds(b, 1), :] = _head_norm(intra + cross, GN_EPS) * _silu(row1(g_ref))


def _retention_sample(proj, s0, pos):
    nb = proj.shape[0]
    lg = _ret_tables(1)[0]
    dec = jnp.broadcast_to(jnp.exp(lg)[:, None, None], (RET_HEADS, 1, RET_DV))
    cos, sin = _rope_tables(pos)
    blk = lambda col: pl.BlockSpec((nb, RET_DK), lambda h, b, col=col: (0, col // RET_DK + h))
    rope = pl.BlockSpec((1, RET_DK // 2), lambda h, b: (0, 0))
    st = pl.BlockSpec((1, 1, RET_DK, RET_DV), lambda h, b: (b, h, 0, 0))
    return pl.pallas_call(
        _ret_sample_body,
        grid=(RET_HEADS, nb),
        in_specs=[blk(COL_RQ), blk(COL_RK), blk(COL_RV), blk(COL_RG), rope, rope,
                  pl.BlockSpec((1, 1, RET_DV), lambda h, b: (h, 0, 0)), st],
        out_specs=[pl.BlockSpec((nb, RET_DV), lambda h, b: (0, h)), st],
        out_shape=[jax.ShapeDtypeStruct((nb, RET_W), F32), jax.ShapeDtypeStruct(s0.shape, F32)],
        compiler_params=_cparams("parallel", "arbitrary"),
        name="retention_sample",
    )(proj, proj, proj, proj, cos, sin, dec, s0)


def _rwkv_prep_sample_body(c_ref, prev_ref, mu_ref, w0_ref, w2_ref, a0_ref, a2_ref, g2_ref, kkp_ref, ka_ref, *out_refs):
    outs = _rwkv_prep_math(c_ref[...], prev_ref[...], mu_ref[...], w0_ref[...], w2_ref[...], a0_ref[...], a2_ref[...],
                           g2_ref[...], kkp_ref[...], ka_ref[...], exact=False)
    for ref, val in zip(out_refs, outs):
        ref[...] = val


def _rwkv_step_body(r_ref, k_ref, v_ref, lw_ref, kk_ref, a_ref, g_ref, rk_ref, lnw_ref, lnb_ref, s0_ref, o_ref, s_ref):
    b = pl.program_id(0)
    dg = functools.partial(lax.dot_general, precision=HIGHEST, preferred_element_type=F32)
    r_all, k_all, v_all, lw_all, kk_all, a_all, g_all = (
        ref[pl.ds(b, 1), :] for ref in (r_ref, k_ref, v_ref, lw_ref, kk_ref, a_ref, g_ref))
    rk_all, lnw_all, lnb_all = rk_ref[...], lnw_ref[...], lnb_ref[...]
    outs = []
    for h in range(RWKV_HEADS):
        sl = slice(h * RWKV_HD, (h + 1) * RWKV_HD)
        kk0 = kk_all[:, sl]
        kk = kk0 * lax.rsqrt(jnp.sum(kk0 * kk0, axis=-1, keepdims=True) + 1e-12)
        bb = kk * a_all[:, sl]
        w = jnp.exp(lw_all[:, sl])
        r, k, v = r_all[:, sl], k_all[:, sl], v_all[:, sl]
        s0 = s0_ref[0, h]
        sa = lax.dot_general(s0.astype(BF16), _pad8(-kk).astype(BF16), _NT, preferred_element_type=F32)[:, 0:1]
        s_new = s0 * w + sa * bb + dg(_pad8(v), _pad8(k), _TN)
        s_ref[0, h] = s_new
        y = lax.dot_general(_pad8(r).astype(BF16), s_new.astype(BF16), _NT, preferred_element_type=F32)[0:1]
        yn = _head_norm(y, RWKV_GN_EPS) * lnw_all[:, sl] + lnb_all[:, sl]
        bonus = jnp.sum(r * k * rk_all[:, sl], axis=-1, keepdims=True) * v
        outs.append((yn + bonus) * g_all[:, sl])
    o_ref[pl.ds(b, 1), :] = jnp.concatenate(outs, axis=1)


def _rwkv_sample(proj, lp, s0, shift0):
    nb = proj.shape[0]
    row = lambda x: x.reshape(1, -1)
    prep = pl.pallas_call(
        _rwkv_prep_sample_body,
        grid=(1,),
        in_specs=[pl.BlockSpec((nb, RWKV_PROJ), lambda i: (0, COL_RW // RWKV_PROJ)), pl.BlockSpec((nb, RWKV_PROJ), lambda i: (0, 0)),
                  pl.BlockSpec((1, RWKV_PROJ), lambda i: (0, 0)), pl.BlockSpec((1, RWKV_W), lambda i: (0, 0)),
                  pl.BlockSpec((RWKV_W_RANK, RWKV_W), lambda i: (0, 0)), pl.BlockSpec((1, RWKV_W), lambda i: (0, 0)),
                  pl.BlockSpec((RWKV_A_RANK, RWKV_W), lambda i: (0, 0)), pl.BlockSpec((RWKV_G_RANK, RWKV_W), lambda i: (0, 0)),
                  pl.BlockSpec((1, RWKV_W), lambda i: (0, 0)), pl.BlockSpec((1, RWKV_W), lambda i: (0, 0))],
        out_specs=[pl.BlockSpec((nb, RWKV_W), lambda i: (0, 0))] * 7,
        out_shape=[jax.ShapeDtypeStruct((nb, RWKV_W), F32)] * 7,
        compiler_params=_cparams("arbitrary"),
        name="rwkv_prep_sample",
    )(proj, shift0, row(lp['mu']), row(lp['w0']), lp['w2'], row(lp['a0']), lp['a2'], lp['g2'], row(lp['kk']), row(lp['ka']))
    act = pl.BlockSpec((nb, RWKV_W), lambda b: (0, 0))
    par = pl.BlockSpec((1, RWKV_W), lambda b: (0, 0))
    st = pl.BlockSpec((1, RWKV_HEADS, RWKV_HD, RWKV_HD), lambda b: (b, 0, 0, 0))
    return pl.pallas_call(
        _rwkv_step_body,
        grid=(nb,),
        in_specs=[act] * 7 + [par] * 3 + [st],
        out_specs=[act, st],
        out_shape=[jax.ShapeDtypeStruct((nb, RWKV_W), F32), jax.ShapeDtypeStruct(s0.shape, F32)],
        compiler_params=_cparams("arbitrary"),
        name="rwkv_step",
    )(*prep, row(lp['rk']), row(lp['ln_w']), row(lp['ln_b']), s0)


def _fox_decode_body(pt_ref, q_ref, kn_ref, vn_ref, bias_ref, *refs, npg):
    del pt_ref
    k_refs, v_refs = refs[:npg], refs[npg:2 * npg]
    o_ref, s_sc, m_sc, l_sc, acc_sc = refs[2 * npg:]
    phase = pl.program_id(1)
    j = pl.program_id(2)
    last = pl.num_programs(2) - 1
    scale = FOX_HD ** -0.5
    rnd = lambda x: x.astype(BF16).astype(F32)

    @pl.when(jnp.logical_and(phase == 0, j == 0))
    def _():
        m_sc[...] = jnp.sum(rnd(q_ref[0]) * rnd(kn_ref[0]), axis=-1, keepdims=True) * scale
        l_sc[...] = jnp.ones(l_sc.shape, F32)

    @pl.when(phase == 0)
    def _():
        qb = q_ref[0].astype(BF16)
        for g in range(npg):
            rows = k_refs[g].shape[2] * FOX_HEADS
            kf = k_refs[g][0, 0].reshape(rows, FOX_HD).astype(BF16)
            s = lax.dot_general(qb, kf, _NT, preferred_element_type=F32) * scale + bias_ref[0, g]
            s_sc[j * npg + g] = s
            m_old = m_sc[...]
            m_new = jnp.maximum(m_old, jnp.max(s, axis=-1, keepdims=True))
            l_sc[...] = jnp.exp(m_old - m_new) * l_sc[...] + jnp.sum(jnp.exp(s - m_new), axis=-1, keepdims=True)
            m_sc[...] = m_new

    @pl.when(phase == 1)
    def _():
        @pl.when(j == 0)
        def _():
            self_score = jnp.sum(rnd(q_ref[0]) * rnd(kn_ref[0]), axis=-1, keepdims=True) * scale
            acc_sc[...] = rnd(jnp.exp(self_score - m_sc[...]) / l_sc[...]) * rnd(vn_ref[0])

        for g in range(npg):
            rows = v_refs[g].shape[2] * FOX_HEADS
            vf = v_refs[g][0, 0].reshape(rows, FOX_HD).astype(BF16)
            p = jnp.exp(s_sc[j * npg + g] - m_sc[...]) / l_sc[...]
            acc_sc[...] += jnp.dot(p.astype(BF16), vf, preferred_element_type=F32)

        @pl.when(j == last)
        def _():
            o_ref[0] = acc_sc[...]


def _fox_decode(q, k_new, v_new, logf_new, cache_k, cache_v, cache_logf, page_table, layer):
    nb, n_pages = page_table.shape
    page = cache_k.shape[2]
    npg = _pick(n_pages, (8, 4, 2, 1))
    plogf = cache_logf[layer][page_table].astype(F32)
    totals = jnp.sum(plogf, axis=2)
    later = lax.cumsum(totals, axis=1, reverse=True) - totals
    dsuf = lax.cumsum(plogf, axis=2, reverse=True) - plogf + later[:, :, None, :]
    bias = dsuf + logf_new[:, None, None, :]
    own = jnp.eye(FOX_HEADS, dtype=bool)[None, None, :, None, :]
    bias = jnp.where(own, bias[:, :, None, :, :], NEG_BIG).reshape(nb, n_pages, FOX_HEADS, page * FOX_HEADS)
    nst = n_pages // npg
    tok = pl.BlockSpec((1, FOX_HEADS, FOX_HD), lambda b, ph, j, pt: (b, 0, 0))
    k_step = lambda ph, j: j * (1 - ph) + (nst - 1) * ph
    v_step = lambda ph, j: j * ph
    page_spec = lambda step, g: pl.BlockSpec(
        (1, 1, page, FOX_HEADS, FOX_HD), lambda b, ph, j, pt: (layer, pt[b, step(ph, j) * npg + g], 0, 0, 0))
    grid_spec = pltpu.PrefetchScalarGridSpec(
        num_scalar_prefetch=1,
        grid=(nb, 2, nst),
        in_specs=[tok, tok, tok,
                  pl.BlockSpec((1, npg, FOX_HEADS, page * FOX_HEADS), lambda b, ph, j, pt: (b, k_step(ph, j), 0, 0))]
                 + [page_spec(k_step, g) for g in range(npg)] + [page_spec(v_step, g) for g in range(npg)],
        out_specs=tok,
        scratch_shapes=[pltpu.VMEM((n_pages, FOX_HEADS, page * FOX_HEADS), F32), pltpu.VMEM((FOX_HEADS, 1), F32),
                        pltpu.VMEM((FOX_HEADS, 1), F32), pltpu.VMEM((FOX_HEADS, FOX_HD), F32)],
    )
    r3 = lambda x: x.reshape(nb, FOX_HEADS, FOX_HD)
    o = pl.pallas_call(
        functools.partial(_fox_decode_body, npg=npg),
        grid_spec=grid_spec,
        out_shape=jax.ShapeDtypeStruct((nb, FOX_HEADS, FOX_HD), F32),
        compiler_params=_cparams("parallel", "arbitrary", "arbitrary"),
        name="fox_decode",
    )(page_table, r3(q), r3(k_new), r3(v_new), bias, *([cache_k] * npg), *([cache_v] * npg))
    return o.reshape(nb, FOX_W)


def _moe_sample_body(e_ref, h_ref, w1_ref, w3_ref, w2_ref, wv_ref, o_ref):
    del e_ref
    m = o_ref.shape[0]

    @pl.when(pl.program_id(0) == 0)
    def _():
        o_ref[...] = jnp.zeros(o_ref.shape, F32)

    h2 = h_ref[...]
    a = jnp.dot(h2, w1_ref[0], preferred_element_type=F32)
    b = jnp.dot(h2, w3_ref[0], preferred_element_type=F32)
    y = jnp.dot((_silu(a) * b).astype(BF16), w2_ref[0], preferred_element_type=F32)
    o_ref[...] += wv_ref[0][:, 0:1] * y[:m]


def _moe_sample(h, eid, gate, w1, w3, w2):
    m, d = h.shape
    na = m * TOP_K
    order = jnp.argsort(eid.reshape(-1))
    e_sorted = eid.reshape(-1)[order].astype(jnp.int32)
    wv = jnp.zeros((na, m), F32).at[jnp.arange(na), order // TOP_K].set(gate.reshape(-1)[order])
    wv = jnp.broadcast_to(wv[:, :, None], (na, m, 128))
    ff = w1.shape[2]
    grid_spec = pltpu.PrefetchScalarGridSpec(
        num_scalar_prefetch=1,
        grid=(na,),
        in_specs=[pl.BlockSpec((2 * m, d), lambda s, e: (0, 0)),
                  pl.BlockSpec((1, d, ff), lambda s, e: (e[s], 0, 0)),
                  pl.BlockSpec((1, d, ff), lambda s, e: (e[s], 0, 0)),
                  pl.BlockSpec((1, ff, d), lambda s, e: (e[s], 0, 0)),
                  pl.BlockSpec((1, m, 128), lambda s, e: (s, 0, 0))],
        out_specs=pl.BlockSpec((m, d), lambda s, e: (0, 0)),
    )
    return pl.pallas_call(
        _moe_sample_body,
        grid_spec=grid_spec,
        out_shape=jax.ShapeDtypeStruct((m, d), F32),
        compiler_params=_cparams("arbitrary"),
        name="moe_sample",
    )(e_sorted, _rows16(h), w1, w3, w2, wv)


def _sample_layer(x, lp, moe, layer, cache_k, cache_v, cache_logf, page_table, s_ret, s_rwkv, s_shift, pos):
    nb = x.shape[0]
    proj = _mm_sample(_rmsnorm(x, lp['norm_mix'], F32), lp['w_in'])
    ret_o, ret_s = _retention_sample(proj, s_ret, pos)
    logf = jax.nn.log_sigmoid(proj[:, COL_FF:COL_FF + FOX_HEADS] + lp['fox_b'].astype(F32))
    fk, fv = proj[:, COL_FK:COL_FK + FOX_W], proj[:, COL_FV:COL_FV + FOX_W]
    fox_o = _fox_decode(proj[:, COL_FQ:COL_FQ + FOX_W], fk, fv, logf, cache_k, cache_v, cache_logf, page_table, layer)
    rw_o, rw_s = _rwkv_sample(proj, lp, s_rwkv, s_shift)
    g2 = jnp.concatenate([proj[:, COL_GATE:]] * 2, axis=0)
    m2 = _merge(_rows16(ret_o), _rows16(fox_o), _rows16(rw_o), lp['wb_ret'], lp['wb_fox'], lp['wb_rwkv'], g2, 0, F32)
    x = x + _mm_sample(m2[:nb], lp['w_out'])
    h2, logits = _rmsnorm_router(x, lp['norm_ffn'], moe['wr'])
    eid, gate = _route(logits, moe['bg'], moe['be'])
    x = x + _moe_sample(h2, eid, gate, moe['w1'], moe['w3'], moe['w2'])
    heads = lambda a: a.reshape(nb, 1, FOX_HEADS, FOX_HD)
    state = (heads(fk), heads(fv), logf.reshape(nb, 1, FOX_HEADS), ret_s, rw_s, proj[:, COL_RW:COL_RW + RWKV_PROJ])
    return x, state


def kernel(x_prompt, x_sample, cache_k, cache_v, cache_logf, page_table, state_ret, state_rwkv, state_shift,
           meta_tokens, norm_mix, norm_ffn, norm_final, w_in, fox_forget_bias,
           rwkv_mu, rwkv_w0, rwkv_w2, rwkv_a0, rwkv_a2, rwkv_g2, rwkv_kk, rwkv_ka, rwkv_rk, rwkv_ln_w, rwkv_ln_b,
           w_branch_ret, w_branch_fox, w_branch_rwkv, w_out,
           router_group_w, router_group_b, router_expert_w, router_expert_b, expert_w1, expert_w3, expert_w2):
    batch, s_len, d = x_prompt.shape
    nb, n_new, _ = x_sample.shape
    assert n_new == 1 and d == D_MODEL and s_len % CHUNK == 0
    depth = w_in.shape[0]
    tp = PAD_FRONT + N_META + s_len
    past_len = page_table.shape[1] * cache_k.shape[2]
    valid = np.tile(np.arange(tp) >= PAD_FRONT, batch)
    xp = jnp.concatenate([jnp.zeros((batch, PAD_FRONT, d), F32),
                          jnp.broadcast_to(meta_tokens[None].astype(F32), (batch, N_META, d)), x_prompt], axis=1)
    xp = xp.reshape(batch * tp, d)
    xs = x_sample.reshape(nb, d)
    pos_s = jnp.full((1,), past_len, jnp.int32)
    outs_p = [[] for _ in range(6)]
    outs_s = [[] for _ in range(6)]
    for l in range(depth):
        lp = dict(norm_mix=norm_mix[l], norm_ffn=norm_ffn[l], w_in=_pack_w_in(w_in[l]), fox_b=fox_forget_bias[l],
                  mu=rwkv_mu[l], w0=rwkv_w0[l], w2=rwkv_w2[l], a0=rwkv_a0[l], a2=rwkv_a2[l], g2=rwkv_g2[l],
                  kk=rwkv_kk[l], ka=rwkv_ka[l], rk=rwkv_rk[l], ln_w=rwkv_ln_w[l], ln_b=rwkv_ln_b[l],
                  wb_ret=w_branch_ret[l].astype(BF16), wb_fox=w_branch_fox[l].astype(BF16),
                  wb_rwkv=w_branch_rwkv[l].astype(BF16), w_out=w_out[l].astype(BF16))
        moe = dict(wr=_router_weights(router_group_w[l], router_expert_w[l]), bg=router_group_b[l], be=router_expert_b[l],
                   w1=_expert_weights_bf16(expert_w1, l), w3=_expert_weights_bf16(expert_w3, l),
                   w2=_expert_weights_bf16(expert_w2, l))
        xp, st = _prompt_layer(xp, lp, moe, batch, tp, valid)
        for j in range(6):
            outs_p[j].append(st[j])
        xs, st = _sample_layer(xs, lp, moe, l, cache_k, cache_v, cache_logf, page_table,
                               state_ret[l], state_rwkv[l], state_shift[l], pos_s)
        for j in range(6):
            outs_s[j].append(st[j])
    y_prompt = _final_norm_prompt(xp, norm_final, batch, tp)
    y_sample = _rmsnorm(xs, norm_final, F32).reshape(nb, 1, d)
    return (y_prompt, y_sample, *[jnp.stack(o, axis=0) for o in outs_p], *[jnp.stack(o, axis=0) for o in outs_s])
```

```python
import functools

import numpy as np
import jax
import jax.numpy as jnp
from jax import lax
from jax.experimental import pallas as pl
from jax.experimental.pallas import tpu as pltpu

F32 = jnp.float32
BF16 = jnp.bfloat16
HIGHEST = lax.Precision.HIGHEST

D_MODEL = 2048
N_META = 16
CHUNK = 128
PAD_FRONT = CHUNK - N_META
RMS_EPS = 1e-6
GN_EPS = 1e-5
RET_HEADS = 4
RET_DK = 256
RET_DV = 256
RET_W = RET_HEADS * RET_DK
ROPE_BASE = 10000.0
FOX_HEADS = 8
FOX_HD = 128
FOX_W = FOX_HEADS * FOX_HD
FOX_PAIR = 2
LOG2E = 1.4426950408889634
RWKV_HEADS = 16
RWKV_HD = 64
RWKV_W = RWKV_HEADS * RWKV_HD
RWKV_W_RANK = 64
RWKV_A_RANK = 64
RWKV_G_RANK = 128
RWKV_GN_EPS = 64e-5
RWKV_PROJ = 3 * RWKV_W + RWKV_W_RANK + RWKV_A_RANK + RWKV_G_RANK
RWKV_CHUNK = 64
RWKV_GROUP = 4
N_BRANCH = 3
N_GROUPS = 4
EXPERTS_PER_GROUP = 8
N_EXPERTS = N_GROUPS * EXPERTS_PER_GROUP
TOP_K = 2
EXPERT_FF = 1024
MOE_BLOCK = 128
ROUTER_COLS = 128

FF_PAD = 256
COL_RW = 0
COL_RQ = COL_RW + RWKV_PROJ
COL_RK = COL_RQ + RET_W
COL_RV = COL_RK + RET_W
COL_RG = COL_RV + RET_W
COL_FQ = COL_RG + RET_W
COL_FK = COL_FQ + FOX_W
COL_FV = COL_FK + FOX_W
COL_FF = COL_FV + FOX_W
COL_GATE = COL_FF + FF_PAD
PROJ_PACKED = COL_GATE + N_BRANCH * D_MODEL
SRC_RQ = 0
SRC_FF = 4 * RET_W + 3 * FOX_W
SRC_RW = SRC_FF + FOX_HEADS
SRC_GATE = SRC_RW + RWKV_PROJ

VMEM_LIMIT = 56 * 1024 * 1024
NEG_BIG = -1e30
_NT = (((1,), (1,)), ((), ()))
_TN = (((0,), (0,)), ((), ()))


def _cparams(*sem):
    return pltpu.CompilerParams(dimension_semantics=sem, vmem_limit_bytes=VMEM_LIMIT)


def _pick(n, prefs):
    for p in prefs:
        if n % p == 0:
            return p
    return n


def _rms(x, g):
    return x * lax.rsqrt(jnp.mean(x * x, axis=-1, keepdims=True) + RMS_EPS) * g


def _rms_body(x_ref, g_ref, o_ref):
    o_ref[...] = _rms(x_ref[...], g_ref[...]).astype(o_ref.dtype)


def _rmsnorm(x, g, out_dtype):
    n, d = x.shape
    tm = _pick(n, (256, 128, 8))
    return pl.pallas_call(
        _rms_body,
        grid=(n // tm,),
        in_specs=[pl.BlockSpec((tm, d), lambda i: (i, 0)), pl.BlockSpec((1, d), lambda i: (0, 0))],
        out_specs=pl.BlockSpec((tm, d), lambda i: (i, 0)),
        out_shape=jax.ShapeDtypeStruct((n, d), out_dtype),
        compiler_params=_cparams("parallel"),
        name="rmsnorm",
    )(x, g.reshape(1, d))


def _rms_router_body(x_ref, g_ref, wr_ref, h_ref, lg_ref):
    h = _rms(x_ref[...], g_ref[...])
    h_ref[...] = h
    lg_ref[...] = jnp.dot(h.astype(BF16), wr_ref[...].astype(BF16), preferred_element_type=F32)


def _rmsnorm_router(x, g, wr):
    n, d = x.shape
    tm = _pick(n, (256, 128, 8))
    return pl.pallas_call(
        _rms_router_body,
        grid=(n // tm,),
        in_specs=[pl.BlockSpec((tm, d), lambda i: (i, 0)), pl.BlockSpec((1, d), lambda i: (0, 0)),
                  pl.BlockSpec((d, ROUTER_COLS), lambda i: (0, 0))],
        out_specs=[pl.BlockSpec((tm, d), lambda i: (i, 0)), pl.BlockSpec((tm, ROUTER_COLS), lambda i: (i, 0))],
        out_shape=[jax.ShapeDtypeStruct((n, d), F32), jax.ShapeDtypeStruct((n, ROUTER_COLS), F32)],
        compiler_params=_cparams("parallel"),
        name="rmsnorm_router",
    )(x, g.reshape(1, d), wr)


def _final_norm_prompt(x, g, batch, tp):
    d = x.shape[1]
    nb = tp // CHUNK
    return pl.pallas_call(
        _rms_body,
        grid=(batch, nb - 1),
        in_specs=[pl.BlockSpec((CHUNK, d), lambda b, j: (b * nb + 1 + j, 0)), pl.BlockSpec((1, d), lambda b, j: (0, 0))],
        out_specs=pl.BlockSpec((CHUNK, d), lambda b, j: (b * (nb - 1) + j, 0)),
        out_shape=jax.ShapeDtypeStruct((batch * (tp - CHUNK), d), F32),
        compiler_params=_cparams("parallel", "parallel"),
        name="final_norm",
    )(x, g.reshape(1, d)).reshape(batch, tp - CHUNK, d)


def _mm_body(a_ref, w_ref, o_ref):
    o_ref[...] = jnp.dot(a_ref[...], w_ref[...], preferred_element_type=F32).astype(o_ref.dtype)


def _matmul(a, w, out_dtype=F32, tm_prefs=(1408, 768, 512, 384, 256, 128), tn_prefs=(512, 256, 128)):
    m, k = a.shape
    n = w.shape[1]
    tm = _pick(m, tm_prefs)
    tn = _pick(n, tn_prefs)
    return pl.pallas_call(
        _mm_body,
        grid=(m // tm, n // tn),
        in_specs=[pl.BlockSpec((tm, k), lambda i, j: (i, 0)), pl.BlockSpec((k, tn), lambda i, j: (0, j))],
        out_specs=pl.BlockSpec((tm, tn), lambda i, j: (i, j)),
        out_shape=jax.ShapeDtypeStruct((m, n), out_dtype),
        compiler_params=_cparams("parallel", "arbitrary"),
        name="matmul",
    )(a, w)


def _mm_res_body(a_ref, w_ref, r_ref, o_ref, *, blocks_per_seq, pad):
    y = r_ref[...] + jnp.dot(a_ref[...], w_ref[...], preferred_element_type=F32)
    if pad:
        first = (pl.program_id(0) % blocks_per_seq) == 0
        row = lax.broadcasted_iota(jnp.int32, y.shape, 0)
        y = jnp.where(jnp.logical_and(first, row < pad), 0.0, y)
    o_ref[...] = y


def _matmul_residual(a, w, res, tp=None):
    m, k = a.shape
    n = w.shape[1]
    tm = _pick(tp, (768, 384, 128)) if tp else m
    assert m % tm == 0
    tn = _pick(n, (512, 256, 128))
    body = functools.partial(_mm_res_body, blocks_per_seq=(tp // tm if tp else 1), pad=(PAD_FRONT if tp else 0))
    return pl.pallas_call(
        body,
        grid=(m // tm, n // tn),
        in_specs=[pl.BlockSpec((tm, k), lambda i, j: (i, 0)), pl.BlockSpec((k, tn), lambda i, j: (0, j)),
                  pl.BlockSpec((tm, tn), lambda i, j: (i, j))],
        out_specs=pl.BlockSpec((tm, tn), lambda i, j: (i, j)),
        out_shape=jax.ShapeDtypeStruct((m, n), F32),
        compiler_params=_cparams("parallel", "arbitrary"),
        name="matmul_residual",
    )(a, w, res)


def _rope_halves(x, cos, sin):
    half = x.shape[-1] // 2
    x1, x2 = x[:, :half], x[:, half:]
    return jnp.concatenate([x1 * cos - x2 * sin, x1 * sin + x2 * cos], axis=-1)


def _head_norm(y, eps):
    mu = jnp.mean(y, axis=-1, keepdims=True)
    yc = y - mu
    return yc * lax.rsqrt(jnp.mean(yc * yc, axis=-1, keepdims=True) + eps)


def _silu(x):
    return x / (1.0 + jnp.exp(-x))


def _sigmoid(x):
    return 1.0 / (1.0 + jnp.exp(-x))


def _ret_body(q_ref, k_ref, v_ref, g_ref, cos_ref, sin_ref, dm_ref, cd_ref, kd_ref, sd_ref, o_ref, s_ref):
    c = pl.program_id(2)

    @pl.when(c == 0)
    def _():
        s_ref[...] = jnp.zeros(s_ref.shape, F32)

    cos = cos_ref[...]
    sin = sin_ref[...]
    q = _rope_halves(q_ref[...], cos, sin)
    k = _rope_halves(k_ref[...], cos, sin) * (RET_DK ** -0.5)
    qb = q.astype(BF16)
    kb = k.astype(BF16)
    vb = v_ref[...].astype(BF16)
    s_old = s_ref[0, 0]
    scores = lax.dot_general(qb, kb, (((1,), (1,)), ((), ())), preferred_element_type=F32) * dm_ref[0]
    intra = jnp.dot(scores.astype(BF16), vb, preferred_element_type=F32)
    cross = jnp.dot(qb, s_old.astype(BF16), preferred_element_type=F32) * cd_ref[0]
    kdec = (k * kd_ref[0]).astype(BF16)
    s_ref[0, 0] = sd_ref[0] * s_old + lax.dot_general(kdec, vb, (((0,), (0,)), ((), ())), preferred_element_type=F32)
    o = _head_norm(intra + cross, GN_EPS) * _silu(g_ref[...])
    o_ref[...] = o.astype(o_ref.dtype)


def _ret_tables(length):
    lg = jnp.log1p(-jnp.power(2.0, -5.0 - jnp.arange(RET_HEADS, dtype=F32)))
    i = jnp.arange(length, dtype=F32)
    diff = i[:, None] - i[None, :]
    dmask = jnp.where(diff >= 0, jnp.exp(lg[:, None, None] * jnp.maximum(diff, 0.0)), 0.0)
    cdec = jnp.exp(lg[:, None] * (i + 1.0)[None, :])
    kdec = jnp.exp(lg[:, None] * (length - 1.0 - i)[None, :])
    sdec = jnp.exp(lg * length)
    return lg, dmask, cdec, kdec, sdec


def _rope_tables(pos):
    half = RET_DK // 2
    inv = ROPE_BASE ** (-jnp.arange(half, dtype=F32) / half)
    ang = pos.astype(F32)[:, None] * inv[None, :]
    return jnp.cos(ang), jnp.sin(ang)


def _retention_prompt(proj, batch, tp):
    n = proj.shape[0]
    nc = tp // CHUNK
    _, dmask, cdec, kdec, sdec = _ret_tables(CHUNK)
    cdec = jnp.broadcast_to(cdec[:, :, None], (RET_HEADS, CHUNK, RET_DV))
    kdec = jnp.broadcast_to(kdec[:, :, None], (RET_HEADS, CHUNK, RET_DK))
    sdec = jnp.broadcast_to(sdec[:, None, None], (RET_HEADS, 1, RET_DV))
    cos, sin = _rope_tables(jnp.arange(tp, dtype=jnp.int32) - PAD_FRONT)
    blk = lambda col: pl.BlockSpec((CHUNK, RET_DK), lambda b, h, c, col=col: (b * nc + c, col // RET_DK + h))
    tab = lambda shape: pl.BlockSpec((1,) + shape, lambda b, h, c: (h, 0, 0))
    rope = pl.BlockSpec((CHUNK, RET_DK // 2), lambda b, h, c: (c, 0))
    return pl.pallas_call(
        _ret_body,
        grid=(batch, RET_HEADS, nc),
        in_specs=[blk(COL_RQ), blk(COL_RK), blk(COL_RV), blk(COL_RG), rope, rope,
                  tab((CHUNK, CHUNK)), tab((CHUNK, RET_DV)), tab((CHUNK, RET_DK)), tab((1, RET_DV))],
        out_specs=[pl.BlockSpec((CHUNK, RET_DV), lambda b, h, c: (b * nc + c, h)),
                   pl.BlockSpec((1, 1, RET_DK, RET_DV), lambda b, h, c: (b, h, 0, 0))],
        out_shape=[jax.ShapeDtypeStruct((n, RET_W), BF16), jax.ShapeDtypeStruct((batch, RET_HEADS, RET_DK, RET_DV), F32)],
        compiler_params=_cparams("parallel", "parallel", "arbitrary"),
        name="retention_prompt",
    )(proj, proj, proj, proj, cos, sin, dmask, cdec, kdec, sdec)


def _fox_body(qi_ref, ki_ref, q_ref, k_ref, v_ref, qx_ref, kx_ref, o_ref, m_sc, l_sc, acc_sc, *, tq, tk, hp):
    step = pl.program_id(2)
    qi = qi_ref[step]
    ki = ki_ref[step]

    @pl.when(ki == 0)
    def _():
        m_sc[...] = jnp.full(m_sc.shape, NEG_BIG, F32)
        l_sc[...] = jnp.zeros(l_sc.shape, F32)
        acc_sc[...] = jnp.zeros(acc_sc.shape, F32)

    def update(masked):
        if masked:
            qpos = qi * tq + lax.broadcasted_iota(jnp.int32, (tq, tk), 0)
            kpos = ki * tk + lax.broadcasted_iota(jnp.int32, (tq, tk), 1)
            valid = jnp.logical_and(kpos <= qpos, kpos >= PAD_FRONT)
        m_old, l_old, acc_old = m_sc[...], l_sc[...], acc_sc[...]
        m_out, l_out, acc_out = [], [], []
        for h in range(hp):
            sl = slice(h * FOX_HD, (h + 1) * FOX_HD)
            qa = jnp.concatenate([q_ref[:, sl].astype(BF16), qx_ref[0, h]], axis=1)
            ka = jnp.concatenate([k_ref[:, sl].astype(BF16), kx_ref[0, h]], axis=1)
            s = lax.dot_general(qa, ka, _NT, preferred_element_type=F32) * (FOX_HD ** -0.5 * LOG2E)
            if masked:
                s = jnp.where(valid, s, NEG_BIG)
            m_new = jnp.maximum(m_old[h], jnp.max(s, axis=1, keepdims=True))
            alpha = jnp.exp2(m_old[h] - m_new)
            p = jnp.exp2(s - m_new)
            m_out.append(m_new)
            l_out.append(alpha * l_old[h] + jnp.sum(p, axis=1, keepdims=True))
            acc_out.append(alpha * acc_old[:, sl] + jnp.dot(p.astype(BF16), v_ref[:, sl].astype(BF16),
                                                            preferred_element_type=F32))
        for h in range(hp):
            m_sc[h] = m_out[h]
            l_sc[h] = l_out[h]
            acc_sc[:, h * FOX_HD:(h + 1) * FOX_HD] = acc_out[h]

    edge = jnp.logical_or(ki == qi, ki == 0)
    pl.when(edge)(functools.partial(update, True))
    pl.when(jnp.logical_not(edge))(functools.partial(update, False))

    @pl.when(ki == qi)
    def _():
        for h in range(hp):
            sl = slice(h * FOX_HD, (h + 1) * FOX_HD)
            o_ref[:, sl] = (acc_sc[:, sl] / l_sc[h]).astype(o_ref.dtype)


def _split3(x):
    hi = x.astype(BF16)
    r1 = x - hi.astype(F32)
    mid = r1.astype(BF16)
    return hi, mid, (r1 - mid.astype(F32)).astype(BF16)


def _fox_prompt(proj, c, batch, tp):
    n = proj.shape[0]
    tq = _pick(tp, (384, 256, 128))
    nq = tp // tq
    hp = FOX_PAIR
    pairs = [(i, j) for i in range(nq) for j in range(i + 1)]
    qi_tab = jnp.asarray(np.array([p[0] for p in pairs], np.int32))
    ki_tab = jnp.asarray(np.array([p[1] for p in pairs], np.int32))
    hi, mid, lo = _split3(c * (FOX_HD ** 0.5))
    one = jnp.ones_like(hi)
    fill = jnp.zeros(c.shape + (FOX_HD - 6,), BF16)
    qx = jnp.concatenate([jnp.stack([hi, mid, lo, one, one, one], axis=-1), fill], axis=-1)
    kx = jnp.concatenate([jnp.stack([one, one, one, -hi, -mid, -lo], axis=-1), fill], axis=-1)
    cb = lambda col: col // (hp * FOX_HD)
    wide = hp * FOX_HD
    grid_spec = pltpu.PrefetchScalarGridSpec(
        num_scalar_prefetch=2,
        grid=(batch, FOX_HEADS // hp, len(pairs)),
        in_specs=[
            pl.BlockSpec((tq, wide), lambda b, h, s, qi, ki: (b * nq + qi[s], cb(COL_FQ) + h)),
            pl.BlockSpec((tq, wide), lambda b, h, s, qi, ki: (b * nq + ki[s], cb(COL_FK) + h)),
            pl.BlockSpec((tq, wide), lambda b, h, s, qi, ki: (b * nq + ki[s], cb(COL_FV) + h)),
            pl.BlockSpec((1, hp, tq, FOX_HD), lambda b, h, s, qi, ki: (b, h, qi[s], 0)),
            pl.BlockSpec((1, hp, tq, FOX_HD), lambda b, h, s, qi, ki: (b, h, ki[s], 0)),
        ],
        out_specs=pl.BlockSpec((tq, wide), lambda b, h, s, qi, ki: (b * nq + qi[s], h)),
        scratch_shapes=[pltpu.VMEM((hp, tq, 1), F32), pltpu.VMEM((hp, tq, 1), F32), pltpu.VMEM((tq, wide), F32)],
    )
    return pl.pallas_call(
        functools.partial(_fox_body, tq=tq, tk=tq, hp=hp),
        grid_spec=grid_spec,
        out_shape=jax.ShapeDtypeStruct((n, FOX_W), BF16),
        compiler_params=_cparams("parallel", "parallel", "arbitrary"),
        name="fox_prompt",
    )(qi_tab, ki_tab, proj, proj, proj, qx, kx)


def _softplus(z):
    return jnp.maximum(z, 0.0) + jnp.log1p(jnp.exp(-jnp.abs(z)))


def _rwkv_prep_math(c, prev, mu, w0, w2, a0, a2, g2, kkp, ka, exact):
    w = RWKV_W
    xm = c + mu * (prev - c)
    r, k, v = xm[:, 0:w], xm[:, w:2 * w], xm[:, 2 * w:3 * w]
    wd = xm[:, 3 * w:3 * w + RWKV_W_RANK]
    ad = xm[:, 3 * w + RWKV_W_RANK:3 * w + RWKV_W_RANK + RWKV_A_RANK]
    gd = xm[:, 3 * w + RWKV_W_RANK + RWKV_A_RANK:]
    if exact:
        mm = lambda x, m: jnp.dot(x, m, precision=HIGHEST, preferred_element_type=F32)
    else:
        mm = lambda x, m: jnp.dot(x.astype(BF16), m.astype(BF16), preferred_element_type=F32)
    w_log = -_softplus(-(w0 + mm(jnp.tanh(wd), w2))) - 0.5
    lw = -jnp.exp(w_log)
    a = _sigmoid(a0 + mm(ad, a2))
    g = mm(_sigmoid(gd), g2)
    kk0 = k * kkp
    kmod = k * (1.0 + (a - 1.0) * ka)
    return r, kmod, v, lw, kk0, a, g


def _rwkv_prep_body(c_ref, mu_ref, w0_ref, w2_ref, a0_ref, a2_ref, g2_ref, kkp_ref, ka_ref,
                    r_ref, k_ref, v_ref, lw_ref, kk_ref, a_ref, g_ref, carry):
    t = pl.program_id(1)

    @pl.when(t == 0)
    def _():
        carry[...] = jnp.zeros(carry.shape, F32)

    c = c_ref[...]
    rows = c.shape[0]
    prev = pltpu.roll(c, 1, axis=0)
    row = lax.broadcasted_iota(jnp.int32, c.shape, 0)
    prev = jnp.where(row == 0, carry[...], prev)
    carry[...] = c[rows - 1:rows, :]
    outs = _rwkv_prep_math(c, prev, mu_ref[...], w0_ref[...], w2_ref[...], a0_ref[...], a2_ref[...], g2_ref[...],
                           kkp_ref[...], ka_ref[...], exact=False)
    for ref, val in zip((r_ref, k_ref, v_ref, lw_ref, kk_ref, a_ref, g_ref), outs):
        ref[...] = val


def _rwkv_prep_prompt(proj, lp, batch, tp):
    n = proj.shape[0]
    tb = CHUNK
    nt = tp // tb
    row = lambda x: x.reshape(1, -1)
    full = lambda shape: pl.BlockSpec(shape, lambda b, t: (0, 0))
    out_spec = pl.BlockSpec((tb, RWKV_W), lambda b, t: (b * nt + t, 0))
    return pl.pallas_call(
        _rwkv_prep_body,
        grid=(batch, nt),
        in_specs=[pl.BlockSpec((tb, RWKV_PROJ), lambda b, t: (b * nt + t, COL_RW // RWKV_PROJ)),
                  full((1, RWKV_PROJ)), full((1, RWKV_W)), full((RWKV_W_RANK, RWKV_W)), full((1, RWKV_W)),
                  full((RWKV_A_RANK, RWKV_W)), full((RWKV_G_RANK, RWKV_W)), full((1, RWKV_W)), full((1, RWKV_W))],
        out_specs=[out_spec] * 7,
        out_shape=[jax.ShapeDtypeStruct((n, RWKV_W), F32)] * 7,
        scratch_shapes=[pltpu.VMEM((1, RWKV_PROJ), F32)],
        compiler_params=_cparams("parallel", "arbitrary"),
        name="rwkv_prep",
    )(proj, row(lp['mu']), row(lp['w0']), lp['w2'], row(lp['a0']), lp['a2'], lp['g2'], row(lp['kk']), row(lp['ka']))


def _rwkv_chunk_body(r_ref, k_ref, v_ref, lw_ref, kk_ref, a_ref, g_ref, rk_ref, lnw_ref, lnb_ref, o_ref, s_ref):
    cn = pl.program_id(1)

    @pl.when(cn == 0)
    def _():
        s_ref[...] = jnp.zeros(s_ref.shape, F32)

    cs = RWKV_CHUNK
    hd = RWKV_HD
    gh = RWKV_GROUP
    n = gh * cs
    gw = gh * hd
    bits = int(np.log2(cs))
    row = lax.broadcasted_iota(jnp.int32, (n, n), 0)
    col = lax.broadcasted_iota(jnp.int32, (n, n), 1)
    same_head = (row >> bits) == (col >> bits)
    strict = same_head & (col < row)
    incl = same_head & (col <= row)
    eye = (row == col).astype(F32)
    pair_masks = [((row >> (bit + 1)) == (col >> (bit + 1))) & ((row & (1 << bit)) != 0) & ((col & (1 << bit)) == 0)
                  for bit in range(bits)]
    trow = lax.broadcasted_iota(jnp.int32, (cs, cs), 0)
    tcol = lax.broadcasted_iota(jnp.int32, (cs, cs), 1)
    cum_all = jnp.dot((tcol <= trow).astype(F32), lw_ref[...], precision=HIGHEST, preferred_element_type=F32)
    decay_all = jnp.exp(cum_all[cs - 1:cs, :])
    tail_all = jnp.exp(cum_all[cs - 1:cs, :] - cum_all)
    dot = lambda x, y: jnp.dot(x.astype(BF16), y.astype(BF16), preferred_element_type=F32)
    dot_nt = lambda x, y: lax.dot_general(x.astype(BF16), y.astype(BF16), _NT, preferred_element_type=F32)
    dot_tn = lambda x, y: lax.dot_general(x.astype(BF16), y.astype(BF16), _TN, preferred_element_type=F32)
    for g in range(RWKV_HEADS // gh):
        lanes = slice(g * gw, (g + 1) * gw)
        stack = lambda x: jnp.concatenate([x[:, g * gw + h * hd:g * gw + (h + 1) * hd] for h in range(gh)], axis=0)
        rows = lambda x: jnp.concatenate([jnp.broadcast_to(x[:, g * gw + h * hd:g * gw + (h + 1) * hd], (cs, hd))
                                          for h in range(gh)], axis=0)
        lw, cum, tail = stack(lw_ref[...]), stack(cum_all), stack(tail_all)
        kk0 = stack(kk_ref[...])
        kk = kk0 * lax.rsqrt(jnp.sum(kk0 * kk0, axis=-1, keepdims=True) + 1e-12)
        b = kk * stack(a_ref[...])
        r, k, v = stack(r_ref[...]), stack(k_ref[...]), stack(v_ref[...])
        e_in = jnp.exp(cum)
        e_neg = jnp.exp(-cum)
        ar = jnp.concatenate([-kk * jnp.exp(cum - lw), r * e_in], axis=0)
        bk = jnp.concatenate([b * e_neg, k * e_neg], axis=0)
        gram = dot_nt(ar, bk)
        l_ab = jnp.where(strict, gram[:n, :n], 0.0)
        l_ak = jnp.where(strict, gram[:n, n:], 0.0)
        m_rbk = jnp.concatenate([jnp.where(incl, gram[n:, :n], 0.0), jnp.where(incl, gram[n:, n:], 0.0)], axis=1)
        tinv = eye + jnp.where(pair_masks[0], l_ab, 0.0)
        for mask in pair_masks[1:]:
            tinv = tinv + dot(dot(tinv, jnp.where(mask, l_ab, 0.0)), tinv)
        s_old = s_ref[0, g * n:(g + 1) * n, :]
        ars = dot_nt(ar, s_old)
        v_bd = jnp.where(same_head, jnp.concatenate([v_ref[:, lanes]] * gh, axis=0), 0.0)
        u = dot(tinv, jnp.where(same_head, ars[:n], 0.0) + dot(l_ak, v_bd))
        uv = jnp.concatenate([u, v_bd], axis=0)
        y_bd = jnp.where(same_head, ars[n:], 0.0) + dot(m_rbk, uv)
        s_ref[0, g * n:(g + 1) * n, :] = s_old * rows(decay_all) + dot_tn(uv, jnp.concatenate([b * tail, k * tail], axis=0))
        y = sum(y_bd[:, h * hd:(h + 1) * hd] for h in range(gh))
        yn = _head_norm(y, RWKV_GN_EPS) * rows(lnw_ref[...]) + rows(lnb_ref[...])
        bonus = jnp.sum(r * k * rows(rk_ref[...]), axis=-1, keepdims=True) * v
        out = (yn + bonus) * stack(g_ref[...])
        o_ref[:, lanes] = jnp.concatenate([out[h * cs:(h + 1) * cs] for h in range(gh)], axis=1).astype(o_ref.dtype)


def _rwkv_chunk_prompt(prep, lp, batch, tp):
    n = prep[0].shape[0]
    cs = RWKV_CHUNK
    ncn = tp // cs
    row = lambda x: x.reshape(1, -1)
    blk = pl.BlockSpec((cs, RWKV_W), lambda b, c: (b * ncn + c, 0))
    full = pl.BlockSpec((1, RWKV_W), lambda b, c: (0, 0))
    out, state = pl.pallas_call(
        _rwkv_chunk_body,
        grid=(batch, ncn),
        in_specs=[blk] * 7 + [full] * 3,
        out_specs=[blk, pl.BlockSpec((1, RWKV_W, RWKV_HD), lambda b, c: (b, 0, 0))],
        out_shape=[jax.ShapeDtypeStruct((n, RWKV_W), BF16), jax.ShapeDtypeStruct((batch, RWKV_W, RWKV_HD), F32)],
        compiler_params=_cparams("parallel", "arbitrary"),
        name="rwkv_chunk",
    )(*prep, row(lp['rk']), row(lp['ln_w']), row(lp['ln_b']))
    return out, state.reshape(batch, RWKV_HEADS, RWKV_HD, RWKV_HD)


def _merge_body(ro_ref, fo_ref, wo_ref, wr_ref, wf_ref, ww_ref, g0_ref, g1_ref, g2_ref, o_ref):
    dot = functools.partial(jnp.dot, preferred_element_type=F32)
    m = (_sigmoid(g0_ref[...]) * dot(ro_ref[...], wr_ref[...])
         + _sigmoid(g1_ref[...]) * dot(fo_ref[...], wf_ref[...])
         + _sigmoid(g2_ref[...]) * dot(wo_ref[...], ww_ref[...]))
    o_ref[...] = m.astype(o_ref.dtype)


def _merge(ret_o, fox_o, rw_o, wb_ret, wb_fox, wb_rwkv, gates, gate_col, out_dtype):
    m = ret_o.shape[0]
    tm = _pick(m, (768, 384, 128))
    tn = 512
    nj = D_MODEL // tn
    act = lambda width: pl.BlockSpec((tm, width), lambda i, j: (i, 0))
    wgt = lambda width: pl.BlockSpec((width, tn), lambda i, j: (0, j))
    gate = lambda br: pl.BlockSpec((tm, tn), lambda i, j, br=br: (i, gate_col // tn + br * nj + j))
    return pl.pallas_call(
        _merge_body,
        grid=(m // tm, nj),
        in_specs=[act(RET_W), act(FOX_W), act(RWKV_W), wgt(RET_W), wgt(FOX_W), wgt(RWKV_W), gate(0), gate(1), gate(2)],
        out_specs=pl.BlockSpec((tm, tn), lambda i, j: (i, j)),
        out_shape=jax.ShapeDtypeStruct((m, D_MODEL), out_dtype),
        compiler_params=_cparams("parallel", "arbitrary"),
        name="merge",
    )(ret_o, fox_o, rw_o, wb_ret, wb_fox, wb_rwkv, gates, gates, gates)


def _cast_body(x_ref, o_ref):
    o_ref[...] = x_ref[...].astype(o_ref.dtype)


def _expert_weights_bf16(w, layer):
    _, e, r, c = w.shape
    return pl.pallas_call(
        _cast_body,
        grid=(e,),
        in_specs=[pl.BlockSpec((None, 1, r, c), lambda i: (layer, i, 0, 0))],
        out_specs=pl.BlockSpec((1, r, c), lambda i: (i, 0, 0)),
        out_shape=jax.ShapeDtypeStruct((e, r, c), BF16),
        compiler_params=_cparams("parallel"),
        name="expert_weights_bf16",
    )(w)


def _route(logits, bg, be):
    n = logits.shape[0]
    gp = jax.nn.softmax(logits[:, :N_GROUPS] + bg.astype(F32), axis=-1)
    gidx = jnp.argmax(gp, axis=-1)
    pg = jnp.take_along_axis(gp, gidx[:, None], axis=-1)
    el = logits[:, N_GROUPS:N_GROUPS + N_EXPERTS].reshape(n, N_GROUPS, EXPERTS_PER_GROUP) + be.astype(F32)[None]
    el = jnp.take_along_axis(el, gidx[:, None, None], axis=1)[:, 0]
    topv, topi = lax.top_k(jax.nn.softmax(el, axis=-1), TOP_K)
    gate = pg * topv / jnp.sum(topv, axis=-1, keepdims=True)
    eid = (gidx[:, None] * EXPERTS_PER_GROUP + topi).astype(jnp.int32)
    return eid, gate


def _router_weights(wg, we):
    d = wg.shape[0]
    wr = jnp.concatenate([wg, jnp.transpose(we, (1, 0, 2)).reshape(d, N_EXPERTS)], axis=1)
    return jnp.pad(wr, ((0, 0), (0, ROUTER_COLS - wr.shape[1])))


def _moe_body(be_ref, nused_ref, tok_ref, dst_ref, h_hbm, w1_ref, w3_ref, w2_ref, y_in, y_hbm, xbuf, ybuf, sem_in, sem_out):
    del y_in
    i = pl.program_id(0)
    n_used = nused_ref[0]
    slot = i % 2
    rows = range(MOE_BLOCK)

    def gather(blk, buf, r):
        return pltpu.make_async_copy(h_hbm.at[pl.ds(tok_ref[blk * MOE_BLOCK + r], 1)], xbuf.at[buf, pl.ds(r, 1)],
                                     sem_in.at[buf])

    def scatter(blk, r):
        return pltpu.make_async_copy(ybuf.at[pl.ds(r, 1)], y_hbm.at[pl.ds(dst_ref[blk * MOE_BLOCK + r], 1)], sem_out)

    @pl.when(jnp.logical_and(i == 0, n_used > 0))
    def _():
        for r in rows:
            gather(0, 0, r).start()

    @pl.when(i < n_used)
    def _():
        for r in rows:
            gather(i, slot, r).wait()
        nxt = jnp.minimum(i + 1, n_used - 1)
        for r in rows:
            gather(nxt, 1 - slot, r).start()
        x = xbuf[slot].astype(BF16)
        a = jnp.dot(x, w1_ref[0], preferred_element_type=F32)
        b = jnp.dot(x, w3_ref[0], preferred_element_type=F32)
        y = jnp.dot((_silu(a) * b).astype(BF16), w2_ref[0], preferred_element_type=F32)

        @pl.when(i > 0)
        def _():
            for r in rows:
                scatter(i - 1, r).wait()

        ybuf[...] = y
        for r in rows:
            scatter(i, r).start()

        @pl.when(i == n_used - 1)
        def _():
            for r in rows:
                gather(nxt, 1 - slot, r).wait()
                scatter(i, r).wait()


def _moe_prompt(h, eid, gate, valid, w1, w3, w2):
    n, d = h.shape
    a_tot = n * TOP_K
    n_real = int(np.sum(valid)) * TOP_K
    n_blk = (n_real + N_EXPERTS * (MOE_BLOCK - 1) + MOE_BLOCK - 1) // MOE_BLOCK
    cap = n_blk * MOE_BLOCK
    validf = jnp.repeat(jnp.asarray(valid), TOP_K)
    eflat = eid.reshape(-1)
    onehot = jnp.logical_and(eflat[:, None] == jnp.arange(N_EXPERTS, dtype=jnp.int32)[None, :], validf[:, None]).astype(jnp.int32)
    blocks = onehot.astype(F32).reshape(a_tot // CHUNK, CHUNK, N_EXPERTS)
    below = jnp.tril(jnp.ones((CHUNK, CHUNK), F32), -1)
    inner = jnp.einsum('ij,bjk->bik', below, blocks)
    totals = jnp.sum(blocks, axis=1)
    offset = jnp.cumsum(totals, axis=0) - totals
    before = (inner + offset[:, None, :]).reshape(a_tot, N_EXPERTS).astype(jnp.int32)
    rank = jnp.sum(before * onehot, axis=1)
    counts = jnp.sum(onehot, axis=0)
    padded = (counts + MOE_BLOCK - 1) // MOE_BLOCK * MOE_BLOCK
    pad_end = jnp.cumsum(padded)
    pad_start = pad_end - padded
    dest = jnp.where(validf, pad_start[eflat] + rank, cap)
    assign = jnp.arange(a_tot, dtype=jnp.int32)
    zero_row = int(np.argmin(valid))
    held = jnp.full((cap,), -1, jnp.int32).at[dest].set(assign, mode='drop')
    tok_buf = jnp.where(held >= 0, held // TOP_K, zero_row)
    dump = a_tot + (jnp.arange(cap, dtype=jnp.int32) % MOE_BLOCK)
    dst_buf = jnp.where(held >= 0, (held % TOP_K) * n + held // TOP_K, dump)
    n_used = (pad_end[-1] // MOE_BLOCK).astype(jnp.int32)
    blk = jnp.arange(n_blk, dtype=jnp.int32)
    blk_e = jnp.sum((pad_end[None, :] <= (blk * MOE_BLOCK)[:, None]).astype(jnp.int32), axis=1)
    blk_e = jnp.minimum(blk_e, N_EXPERTS - 1)
    blk_e = jnp.where(blk < n_used, blk_e, blk_e[jnp.maximum(n_used - 1, 0)])
    y_rows = a_tot + MOE_BLOCK
    y0 = jnp.zeros((y_rows, d), F32)
    ff = w1.shape[2]
    grid_spec = pltpu.PrefetchScalarGridSpec(
        num_scalar_prefetch=4,
        grid=(n_blk,),
        in_specs=[pl.BlockSpec(memory_space=pl.ANY),
                  pl.BlockSpec((1, d, ff), lambda i, be, nu, tk, ds: (be[i], 0, 0)),
                  pl.BlockSpec((1, d, ff), lambda i, be, nu, tk, ds: (be[i], 0, 0)),
                  pl.BlockSpec((1, ff, d), lambda i, be, nu, tk, ds: (be[i], 0, 0)),
                  pl.BlockSpec(memory_space=pl.ANY)],
        out_specs=pl.BlockSpec(memory_space=pl.ANY),
        scratch_shapes=[pltpu.VMEM((2, MOE_BLOCK, d), F32), pltpu.VMEM((MOE_BLOCK, d), F32),
                        pltpu.SemaphoreType.DMA((2,)), pltpu.SemaphoreType.DMA(())],
    )
    y2 = pl.pallas_call(
        _moe_body,
        grid_spec=grid_spec,
        out_shape=jax.ShapeDtypeStruct((y_rows, d), F32),
        input_output_aliases={8: 0},
        compiler_params=_cparams("arbitrary"),
        name="moe_experts",
    )(blk_e, n_used.reshape(1), tok_buf, dst_buf, h, w1, w3, w2, y0)
    return y2


def _combine_body(x_ref, y0_ref, y1_ref, g_ref, o_ref):
    g = g_ref[...]
    o_ref[...] = x_ref[...] + (y0_ref[...] * g[:, 0:1] + y1_ref[...] * g[:, 1:2])


def _moe_combine(x, y2, gate):
    n, d = x.shape
    tm = _pick(n, (256, 128, 8))
    gpad = jnp.pad(gate, ((0, 0), (0, 128 - TOP_K)))
    return pl.pallas_call(
        _combine_body,
        grid=(n // tm,),
        in_specs=[pl.BlockSpec((tm, d), lambda i: (i, 0)), pl.BlockSpec((tm, d), lambda i: (i, 0)),
                  pl.BlockSpec((tm, d), lambda i: (n // tm + i, 0)), pl.BlockSpec((tm, 128), lambda i: (i, 0))],
        out_specs=pl.BlockSpec((tm, d), lambda i: (i, 0)),
        out_shape=jax.ShapeDtypeStruct((n, d), F32),
        compiler_params=_cparams("parallel"),
        name="moe_combine",
    )(x, y2, y2, gpad)


def _pack_w_in(w_in):
    ff = jnp.pad(w_in[:, SRC_FF:SRC_FF + FOX_HEADS], ((0, 0), (0, FF_PAD - FOX_HEADS)))
    return jnp.concatenate([w_in[:, SRC_RW:SRC_RW + RWKV_PROJ], w_in[:, :SRC_FF], ff, w_in[:, SRC_GATE:]],
                           axis=1).astype(BF16)


def _prompt_layer(x, lp, moe, batch, tp, valid):
    h = _rmsnorm(x, lp['norm_mix'], BF16)
    proj = _matmul(h, lp['w_in'])
    ret_o, ret_s = _retention_prompt(proj, batch, tp)
    ff = proj[:, COL_FF:COL_FF + FOX_HEADS].reshape(batch, tp, FOX_HEADS)
    logf = jax.nn.log_sigmoid(ff + lp['fox_b'].astype(F32))
    c = jnp.cumsum(logf, axis=1).transpose(0, 2, 1)
    fox_o = _fox_prompt(proj, c, batch, tp)
    prep = _rwkv_prep_prompt(proj, lp, batch, tp)
    rw_o, rw_s = _rwkv_chunk_prompt(prep, lp, batch, tp)
    merged = _merge(ret_o, fox_o, rw_o, lp['wb_ret'], lp['wb_fox'], lp['wb_rwkv'], proj, COL_GATE, BF16)
    x = _matmul_residual(merged, lp['w_out'], x, tp=tp)
    h2, logits = _rmsnorm_router(x, lp['norm_ffn'], moe['wr'])
    eid, gate = _route(logits, moe['bg'], moe['be'])
    y2 = _moe_prompt(h2, eid, gate, valid, moe['w1'], moe['w3'], moe['w2'])
    x = _moe_combine(x, y2, gate)
    p3 = proj.reshape(batch, tp, PROJ_PACKED)
    heads = lambda col: p3[:, PAD_FRONT:, col:col + FOX_W].reshape(batch, tp - PAD_FRONT, FOX_HEADS, FOX_HD)
    state = (heads(COL_FK), heads(COL_FV), logf[:, PAD_FRONT:], ret_s, rw_s, p3[:, tp - 1, COL_RW:COL_RW + RWKV_PROJ])
    return x, state


def _rows16(x):
    return jnp.concatenate([x.astype(BF16), jnp.zeros(x.shape, BF16)], axis=0)


def _mm_sample(x, w):
    return _matmul(_rows16(x), w, tn_prefs=(1536, 1024, 512, 256, 128))[:x.shape[0]]


def _pad8(x):
    first = lax.broadcasted_iota(jnp.int32, (8, x.shape[1]), 0) == 0
    return jnp.where(first, jnp.broadcast_to(x, (8, x.shape[1])), 0.0)


def _ret_sample_body(q_ref, k_ref, v_ref, g_ref, cos_ref, sin_ref, dec_ref, s0_ref, o_ref, s_ref):
    b = pl.program_id(1)
    row1 = lambda ref: ref[pl.ds(b, 1), :]
    cos = cos_ref[...]
    sin = sin_ref[...]
    q = _rope_halves(row1(q_ref), cos, sin)
    k = _rope_halves(row1(k_ref), cos, sin) * (RET_DK ** -0.5)
    v = row1(v_ref)
    dec = dec_ref[0]
    s0 = s0_ref[0, 0]
    rnd = lambda x: x.astype(BF16).astype(F32)
    cross = jnp.dot(_pad8(q).astype(BF16), s0.astype(BF16), preferred_element_type=F32)[0:1] * dec
    intra = jnp.sum(rnd(q) * rnd(k), axis=-1, keepdims=True) * v
    s_ref[0, 0] = dec * s0 + lax.dot_general(_pad8(k), _pad8(v), _TN, precision=HIGHEST, preferred_element_type=F32)
    o_ref[pl.ds(b, 1), :] = _head_norm(intra + cross, GN_EPS) * _silu(row1(g_ref))


def _retention_sample(proj, s0, pos):
    nb = proj.shape[0]
    lg = _ret_tables(1)[0]
    dec = jnp.broadcast_to(jnp.exp(lg)[:, None, None], (RET_HEADS, 1, RET_DV))
    cos, sin = _rope_tables(pos)
    blk = lambda col: pl.BlockSpec((nb, RET_DK), lambda h, b, col=col: (0, col // RET_DK + h))
    rope = pl.BlockSpec((1, RET_DK // 2), lambda h, b: (0, 0))
    st = pl.BlockSpec((1, 1, RET_DK, RET_DV), lambda h, b: (b, h, 0, 0))
    return pl.pallas_call(
        _ret_sample_body,
        grid=(RET_HEADS, nb),
        in_specs=[blk(COL_RQ), blk(COL_RK), blk(COL_RV), blk(COL_RG), rope, rope,
                  pl.BlockSpec((1, 1, RET_DV), lambda h, b: (h, 0, 0)), st],
        out_specs=[pl.BlockSpec((nb, RET_DV), lambda h, b: (0, h)), st],
        out_shape=[jax.ShapeDtypeStruct((nb, RET_W), F32), jax.ShapeDtypeStruct(s0.shape, F32)],
        compiler_params=_cparams("parallel", "arbitrary"),
        name="retention_sample",
    )(proj, proj, proj, proj, cos, sin, dec, s0)


def _rwkv_prep_sample_body(c_ref, prev_ref, mu_ref, w0_ref, w2_ref, a0_ref, a2_ref, g2_ref, kkp_ref, ka_ref, *out_refs):
    outs = _rwkv_prep_math(c_ref[...], prev_ref[...], mu_ref[...], w0_ref[...], w2_ref[...], a0_ref[...], a2_ref[...],
                           g2_ref[...], kkp_ref[...], ka_ref[...], exact=False)
    for ref, val in zip(out_refs, outs):
        ref[...] = val


def _rwkv_step_body(r_ref, k_ref, v_ref, lw_ref, kk_ref, a_ref, g_ref, rk_ref, lnw_ref, lnb_ref, s0_ref, o_ref, s_ref):
    b = pl.program_id(0)
    dg = functools.partial(lax.dot_general, precision=HIGHEST, preferred_element_type=F32)
    r_all, k_all, v_all, lw_all, kk_all, a_all, g_all = (
        ref[pl.ds(b, 1), :] for ref in (r_ref, k_ref, v_ref, lw_ref, kk_ref, a_ref, g_ref))
    rk_all, lnw_all, lnb_all = rk_ref[...], lnw_ref[...], lnb_ref[...]
    outs = []
    for h in range(RWKV_HEADS):
        sl = slice(h * RWKV_HD, (h + 1) * RWKV_HD)
        kk0 = kk_all[:, sl]
        kk = kk0 * lax.rsqrt(jnp.sum(kk0 * kk0, axis=-1, keepdims=True) + 1e-12)
        bb = kk * a_all[:, sl]
        w = jnp.exp(lw_all[:, sl])
        r, k, v = r_all[:, sl], k_all[:, sl], v_all[:, sl]
        s0 = s0_ref[0, h]
        sa = lax.dot_general(s0.astype(BF16), _pad8(-kk).astype(BF16), _NT, preferred_element_type=F32)[:, 0:1]
        s_new = s0 * w + sa * bb + dg(_pad8(v), _pad8(k), _TN)
        s_ref[0, h] = s_new
        y = lax.dot_general(_pad8(r).astype(BF16), s_new.astype(BF16), _NT, preferred_element_type=F32)[0:1]
        yn = _head_norm(y, RWKV_GN_EPS) * lnw_all[:, sl] + lnb_all[:, sl]
        bonus = jnp.sum(r * k * rk_all[:, sl], axis=-1, keepdims=True) * v
        outs.append((yn + bonus) * g_all[:, sl])
    o_ref[pl.ds(b, 1), :] = jnp.concatenate(outs, axis=1)


def _rwkv_sample(proj, lp, s0, shift0):
    nb = proj.shape[0]
    row = lambda x: x.reshape(1, -1)
    prep = pl.pallas_call(
        _rwkv_prep_sample_body,
        grid=(1,),
        in_specs=[pl.BlockSpec((nb, RWKV_PROJ), lambda i: (0, COL_RW // RWKV_PROJ)), pl.BlockSpec((nb, RWKV_PROJ), lambda i: (0, 0)),
                  pl.BlockSpec((1, RWKV_PROJ), lambda i: (0, 0)), pl.BlockSpec((1, RWKV_W), lambda i: (0, 0)),
                  pl.BlockSpec((RWKV_W_RANK, RWKV_W), lambda i: (0, 0)), pl.BlockSpec((1, RWKV_W), lambda i: (0, 0)),
                  pl.BlockSpec((RWKV_A_RANK, RWKV_W), lambda i: (0, 0)), pl.BlockSpec((RWKV_G_RANK, RWKV_W), lambda i: (0, 0)),
                  pl.BlockSpec((1, RWKV_W), lambda i: (0, 0)), pl.BlockSpec((1, RWKV_W), lambda i: (0, 0))],
        out_specs=[pl.BlockSpec((nb, RWKV_W), lambda i: (0, 0))] * 7,
        out_shape=[jax.ShapeDtypeStruct((nb, RWKV_W), F32)] * 7,
        compiler_params=_cparams("arbitrary"),
        name="rwkv_prep_sample",
    )(proj, shift0, row(lp['mu']), row(lp['w0']), lp['w2'], row(lp['a0']), lp['a2'], lp['g2'], row(lp['kk']), row(lp['ka']))
    act = pl.BlockSpec((nb, RWKV_W), lambda b: (0, 0))
    par = pl.BlockSpec((1, RWKV_W), lambda b: (0, 0))
    st = pl.BlockSpec((1, RWKV_HEADS, RWKV_HD, RWKV_HD), lambda b: (b, 0, 0, 0))
    return pl.pallas_call(
        _rwkv_step_body,
        grid=(nb,),
        in_specs=[act] * 7 + [par] * 3 + [st],
        out_specs=[act, st],
        out_shape=[jax.ShapeDtypeStruct((nb, RWKV_W), F32), jax.ShapeDtypeStruct(s0.shape, F32)],
        compiler_params=_cparams("arbitrary"),
        name="rwkv_step",
    )(*prep, row(lp['rk']), row(lp['ln_w']), row(lp['ln_b']), s0)


def _fox_decode_body(pt_ref, q_ref, kn_ref, vn_ref, bias_ref, *refs, npg):
    del pt_ref
    k_refs, v_refs = refs[:npg], refs[npg:2 * npg]
    o_ref, s_sc, m_sc, l_sc, acc_sc = refs[2 * npg:]
    phase = pl.program_id(1)
    j = pl.program_id(2)
    last = pl.num_programs(2) - 1
    scale = FOX_HD ** -0.5
    rnd = lambda x: x.astype(BF16).astype(F32)

    @pl.when(jnp.logical_and(phase == 0, j == 0))
    def _():
        m_sc[...] = jnp.sum(rnd(q_ref[0]) * rnd(kn_ref[0]), axis=-1, keepdims=True) * scale
        l_sc[...] = jnp.ones(l_sc.shape, F32)

    @pl.when(phase == 0)
    def _():
        qb = q_ref[0].astype(BF16)
        for g in range(npg):
            rows = k_refs[g].shape[2] * FOX_HEADS
            kf = k_refs[g][0, 0].reshape(rows, FOX_HD).astype(BF16)
            s = lax.dot_general(qb, kf, _NT, preferred_element_type=F32) * scale + bias_ref[0, g]
            s_sc[j * npg + g] = s
            m_old = m_sc[...]
            m_new = jnp.maximum(m_old, jnp.max(s, axis=-1, keepdims=True))
            l_sc[...] = jnp.exp(m_old - m_new) * l_sc[...] + jnp.sum(jnp.exp(s - m_new), axis=-1, keepdims=True)
            m_sc[...] = m_new

    @pl.when(phase == 1)
    def _():
        @pl.when(j == 0)
        def _():
            self_score = jnp.sum(rnd(q_ref[0]) * rnd(kn_ref[0]), axis=-1, keepdims=True) * scale
            acc_sc[...] = rnd(jnp.exp(self_score - m_sc[...]) / l_sc[...]) * rnd(vn_ref[0])

        for g in range(npg):
            rows = v_refs[g].shape[2] * FOX_HEADS
            vf = v_refs[g][0, 0].reshape(rows, FOX_HD).astype(BF16)
            p = jnp.exp(s_sc[j * npg + g] - m_sc[...]) / l_sc[...]
            acc_sc[...] += jnp.dot(p.astype(BF16), vf, preferred_element_type=F32)

        @pl.when(j == last)
        def _():
            o_ref[0] = acc_sc[...]


def _fox_decode(q, k_new, v_new, logf_new, cache_k, cache_v, cache_logf, page_table, layer):
    nb, n_pages = page_table.shape
    page = cache_k.shape[2]
    npg = _pick(n_pages, (8, 4, 2, 1))
    plogf = cache_logf[layer][page_table].astype(F32)
    totals = jnp.sum(plogf, axis=2)
    later = lax.cumsum(totals, axis=1, reverse=True) - totals
    dsuf = lax.cumsum(plogf, axis=2, reverse=True) - plogf + later[:, :, None, :]
    bias = dsuf + logf_new[:, None, None, :]
    own = jnp.eye(FOX_HEADS, dtype=bool)[None, None, :, None, :]
    bias = jnp.where(own, bias[:, :, None, :, :], NEG_BIG).reshape(nb, n_pages, FOX_HEADS, page * FOX_HEADS)
    nst = n_pages // npg
    tok = pl.BlockSpec((1, FOX_HEADS, FOX_HD), lambda b, ph, j, pt: (b, 0, 0))
    k_step = lambda ph, j: j * (1 - ph) + (nst - 1) * ph
    v_step = lambda ph, j: j * ph
    page_spec = lambda step, g: pl.BlockSpec(
        (1, 1, page, FOX_HEADS, FOX_HD), lambda b, ph, j, pt: (layer, pt[b, step(ph, j) * npg + g], 0, 0, 0))
    grid_spec = pltpu.PrefetchScalarGridSpec(
        num_scalar_prefetch=1,
        grid=(nb, 2, nst),
        in_specs=[tok, tok, tok,
                  pl.BlockSpec((1, npg, FOX_HEADS, page * FOX_HEADS), lambda b, ph, j, pt: (b, k_step(ph, j), 0, 0))]
                 + [page_spec(k_step, g) for g in range(npg)] + [page_spec(v_step, g) for g in range(npg)],
        out_specs=tok,
        scratch_shapes=[pltpu.VMEM((n_pages, FOX_HEADS, page * FOX_HEADS), F32), pltpu.VMEM((FOX_HEADS, 1), F32),
                        pltpu.VMEM((FOX_HEADS, 1), F32), pltpu.VMEM((FOX_HEADS, FOX_HD), F32)],
    )
    r3 = lambda x: x.reshape(nb, FOX_HEADS, FOX_HD)
    o = pl.pallas_call(
        functools.partial(_fox_decode_body, npg=npg),
        grid_spec=grid_spec,
        out_shape=jax.ShapeDtypeStruct((nb, FOX_HEADS, FOX_HD), F32),
        compiler_params=_cparams("parallel", "arbitrary", "arbitrary"),
        name="fox_decode",
    )(page_table, r3(q), r3(k_new), r3(v_new), bias, *([cache_k] * npg), *([cache_v] * npg))
    return o.reshape(nb, FOX_W)


def _moe_sample_body(e_ref, h_ref, w1_ref, w3_ref, w2_ref, wv_ref, o_ref):
    del e_ref
    m = o_ref.shape[0]

    @pl.when(pl.program_id(0) == 0)
    def _():
        o_ref[...] = jnp.zeros(o_ref.shape, F32)

    h2 = h_ref[...]
    a = jnp.dot(h2, w1_ref[0], preferred_element_type=F32)
    b = jnp.dot(h2, w3_ref[0], preferred_element_type=F32)
    y = jnp.dot((_silu(a) * b).astype(BF16), w2_ref[0], preferred_element_type=F32)
    o_ref[...] += wv_ref[0][:, 0:1] * y[:m]


def _moe_sample(h, eid, gate, w1, w3, w2):
    m, d = h.shape
    na = m * TOP_K
    order = jnp.argsort(eid.reshape(-1))
    e_sorted = eid.reshape(-1)[order].astype(jnp.int32)
    wv = jnp.zeros((na, m), F32).at[jnp.arange(na), order // TOP_K].set(gate.reshape(-1)[order])
    wv = jnp.broadcast_to(wv[:, :, None], (na, m, 128))
    ff = w1.shape[2]
    grid_spec = pltpu.PrefetchScalarGridSpec(
        num_scalar_prefetch=1,
        grid=(na,),
        in_specs=[pl.BlockSpec((2 * m, d), lambda s, e: (0, 0)),
                  pl.BlockSpec((1, d, ff), lambda s, e: (e[s], 0, 0)),
                  pl.BlockSpec((1, d, ff), lambda s, e: (e[s], 0, 0)),
                  pl.BlockSpec((1, ff, d), lambda s, e: (e[s], 0, 0)),
                  pl.BlockSpec((1, m, 128), lambda s, e: (s, 0, 0))],
        out_specs=pl.BlockSpec((m, d), lambda s, e: (0, 0)),
    )
    return pl.pallas_call(
        _moe_sample_body,
        grid_spec=grid_spec,
        out_shape=jax.ShapeDtypeStruct((m, d), F32),
        compiler_params=_cparams("arbitrary"),
        name="moe_sample",
    )(e_sorted, _rows16(h), w1, w3, w2, wv)


def _sample_layer(x, lp, moe, layer, cache_k, cache_v, cache_logf, page_table, s_ret, s_rwkv, s_shift, pos):
    nb = x.shape[0]
    proj = _mm_sample(_rmsnorm(x, lp['norm_mix'], F32), lp['w_in'])
    ret_o, ret_s = _retention_sample(proj, s_ret, pos)
    logf = jax.nn.log_sigmoid(proj[:, COL_FF:COL_FF + FOX_HEADS] + lp['fox_b'].astype(F32))
    fk, fv = proj[:, COL_FK:COL_FK + FOX_W], proj[:, COL_FV:COL_FV + FOX_W]
    fox_o = _fox_decode(proj[:, COL_FQ:COL_FQ + FOX_W], fk, fv, logf, cache_k, cache_v, cache_logf, page_table, layer)
    rw_o, rw_s = _rwkv_sample(proj, lp, s_rwkv, s_shift)
    g2 = jnp.concatenate([proj[:, COL_GATE:]] * 2, axis=0)
    m2 = _merge(_rows16(ret_o), _rows16(fox_o), _rows16(rw_o), lp['wb_ret'], lp['wb_fox'], lp['wb_rwkv'], g2, 0, F32)
    x = x + _mm_sample(m2[:nb], lp['w_out'])
    h2, logits = _rmsnorm_router(x, lp['norm_ffn'], moe['wr'])
    eid, gate = _route(logits, moe['bg'], moe['be'])
    x = x + _moe_sample(h2, eid, gate, moe['w1'], moe['w3'], moe['w2'])
    heads = lambda a: a.reshape(nb, 1, FOX_HEADS, FOX_HD)
    state = (heads(fk), heads(fv), logf.reshape(nb, 1, FOX_HEADS), ret_s, rw_s, proj[:, COL_RW:COL_RW + RWKV_PROJ])
    return x, state


def kernel(x_prompt, x_sample, cache_k, cache_v, cache_logf, page_table, state_ret, state_rwkv, state_shift,
           meta_tokens, norm_mix, norm_ffn, norm_final, w_in, fox_forget_bias,
           rwkv_mu, rwkv_w0, rwkv_w2, rwkv_a0, rwkv_a2, rwkv_g2, rwkv_kk, rwkv_ka, rwkv_rk, rwkv_ln_w, rwkv_ln_b,
           w_branch_ret, w_branch_fox, w_branch_rwkv, w_out,
           router_group_w, router_group_b, router_expert_w, router_expert_b, expert_w1, expert_w3, expert_w2):
    batch, s_len, d = x_prompt.shape
    nb, n_new, _ = x_sample.shape
    assert n_new == 1 and d == D_MODEL and s_len % CHUNK == 0
    depth = w_in.shape[0]
    tp = PAD_FRONT + N_META + s_len
    past_len = page_table.shape[1] * cache_k.shape[2]
    valid = np.tile(np.arange(tp) >= PAD_FRONT, batch)
    xp = jnp.concatenate([jnp.zeros((batch, PAD_FRONT, d), F32),
                          jnp.broadcast_to(meta_tokens[None].astype(F32), (batch, N_META, d)), x_prompt], axis=1)
    xp = xp.reshape(batch * tp, d)
    xs = x_sample.reshape(nb, d)
    pos_s = jnp.full((1,), past_len, jnp.int32)
    outs_p = [[] for _ in range(6)]
    outs_s = [[] for _ in range(6)]
    for l in range(depth):
        lp = dict(norm_mix=norm_mix[l], norm_ffn=norm_ffn[l], w_in=_pack_w_in(w_in[l]), fox_b=fox_forget_bias[l],
                  mu=rwkv_mu[l], w0=rwkv_w0[l], w2=rwkv_w2[l], a0=rwkv_a0[l], a2=rwkv_a2[l], g2=rwkv_g2[l],
                  kk=rwkv_kk[l], ka=rwkv_ka[l], rk=rwkv_rk[l], ln_w=rwkv_ln_w[l], ln_b=rwkv_ln_b[l],
                  wb_ret=w_branch_ret[l].astype(BF16), wb_fox=w_branch_fox[l].astype(BF16),
                  wb_rwkv=w_branch_rwkv[l].astype(BF16), w_out=w_out[l].astype(BF16))
        moe = dict(wr=_router_weights(router_group_w[l], router_expert_w[l]), bg=router_group_b[l], be=router_expert_b[l],
                   w1=_expert_weights_bf16(expert_w1, l), w3=_expert_weights_bf16(expert_w3, l),
                   w2=_expert_weights_bf16(expert_w2, l))
        xp, st = _prompt_layer(xp, lp, moe, batch, tp, valid)
        for j in range(6):
            outs_p[j].append(st[j])
        xs, st = _sample_layer(xs, lp, moe, l, cache_k, cache_v, cache_logf, page_table,
                               state_ret[l], state_rwkv[l], state_shift[l], pos_s)
        for j in range(6):
            outs_s[j].append(st[j])
    y_prompt = _final_norm_prompt(xp, norm_final, batch, tp)
    y_sample = _rmsnorm(xs, norm_final, F32).reshape(nb, 1, d)
    return (y_prompt, y_sample, *[jnp.stack(o, axis=0) for o in outs_p], *[jnp.stack(o, axis=0) for o in outs_s])
```

```python
import functools

import numpy as np
import jax
import jax.numpy as jnp
from jax import lax
from jax.experimental import pallas as pl
from jax.experimental.pallas import tpu as pltpu

F32 = jnp.float32
BF16 = jnp.bfloat16
HIGHEST = lax.Precision.HIGHEST

D_MODEL = 2048
N_META = 16
CHUNK = 128
PAD_FRONT = CHUNK - N_META
RMS_EPS = 1e-6
GN_EPS = 1e-5
RET_HEADS = 4
RET_DK = 256
RET_DV = 256
RET_W = RET_HEADS * RET_DK
ROPE_BASE = 10000.0
FOX_HEADS = 8
FOX_HD = 128
FOX_W = FOX_HEADS * FOX_HD
FOX_PAIR = 2
LOG2E = 1.4426950408889634
RWKV_HEADS = 16
RWKV_HD = 64
RWKV_W = RWKV_HEADS * RWKV_HD
RWKV_W_RANK = 64
RWKV_A_RANK = 64
RWKV_G_RANK = 128
RWKV_GN_EPS = 64e-5
RWKV_PROJ = 3 * RWKV_W + RWKV_W_RANK + RWKV_A_RANK + RWKV_G_RANK
RWKV_CHUNK = 64
RWKV_GROUP = 4
N_BRANCH = 3
N_GROUPS = 4
EXPERTS_PER_GROUP = 8
N_EXPERTS = N_GROUPS * EXPERTS_PER_GROUP
TOP_K = 2
EXPERT_FF = 1024
MOE_BLOCK = 128
ROUTER_COLS = 128

FF_PAD = 256
COL_RW = 0
COL_RQ = COL_RW + RWKV_PROJ
COL_RK = COL_RQ + RET_W
COL_RV = COL_RK + RET_W
COL_RG = COL_RV + RET_W
COL_FQ = COL_RG + RET_W
COL_FK = COL_FQ + FOX_W
COL_FV = COL_FK + FOX_W
COL_FF = COL_FV + FOX_W
COL_GATE = COL_FF + FF_PAD
PROJ_PACKED = COL_GATE + N_BRANCH * D_MODEL
SRC_RQ = 0
SRC_FF = 4 * RET_W + 3 * FOX_W
SRC_RW = SRC_FF + FOX_HEADS
SRC_GATE = SRC_RW + RWKV_PROJ

VMEM_LIMIT = 56 * 1024 * 1024
NEG_BIG = -1e30
_NT = (((1,), (1,)), ((), ()))
_TN = (((0,), (0,)), ((), ()))


def _cparams(*sem):
    return pltpu.CompilerParams(dimension_semantics=sem, vmem_limit_bytes=VMEM_LIMIT)


def _pick(n, prefs):
    for p in prefs:
        if n % p == 0:
            return p
    return n


def _rms(x, g):
    return x * lax.rsqrt(jnp.mean(x * x, axis=-1, keepdims=True) + RMS_EPS) * g


def _rms_body(x_ref, g_ref, o_ref):
    o_ref[...] = _rms(x_ref[...], g_ref[...]).astype(o_ref.dtype)


def _rmsnorm(x, g, out_dtype):
    n, d = x.shape
    tm = _pick(n, (256, 128, 8))
    return pl.pallas_call(
        _rms_body,
        grid=(n // tm,),
        in_specs=[pl.BlockSpec((tm, d), lambda i: (i, 0)), pl.BlockSpec((1, d), lambda i: (0, 0))],
        out_specs=pl.BlockSpec((tm, d), lambda i: (i, 0)),
        out_shape=jax.ShapeDtypeStruct((n, d), out_dtype),
        compiler_params=_cparams("parallel"),
        name="rmsnorm",
    )(x, g.reshape(1, d))


def _rms_router_body(x_ref, g_ref, wr_ref, h_ref, lg_ref):
    h = _rms(x_ref[...], g_ref[...])
    h_ref[...] = h
    lg_ref[...] = jnp.dot(h.astype(BF16), wr_ref[...].astype(BF16), preferred_element_type=F32)


def _rmsnorm_router(x, g, wr):
    n, d = x.shape
    tm = _pick(n, (256, 128, 8))
    return pl.pallas_call(
        _rms_router_body,
        grid=(n // tm,),
        in_specs=[pl.BlockSpec((tm, d), lambda i: (i, 0)), pl.BlockSpec((1, d), lambda i: (0, 0)),
                  pl.BlockSpec((d, ROUTER_COLS), lambda i: (0, 0))],
        out_specs=[pl.BlockSpec((tm, d), lambda i: (i, 0)), pl.BlockSpec((tm, ROUTER_COLS), lambda i: (i, 0))],
        out_shape=[jax.ShapeDtypeStruct((n, d), F32), jax.ShapeDtypeStruct((n, ROUTER_COLS), F32)],
        compiler_params=_cparams("parallel"),
        name="rmsnorm_router",
    )(x, g.reshape(1, d), wr)


def _final_norm_prompt(x, g, batch, tp):
    d = x.shape[1]
    nb = tp // CHUNK
    return pl.pallas_call(
        _rms_body,
        grid=(batch, nb - 1),
        in_specs=[pl.BlockSpec((CHUNK, d), lambda b, j: (b * nb + 1 + j, 0)), pl.BlockSpec((1, d), lambda b, j: (0, 0))],
        out_specs=pl.BlockSpec((CHUNK, d), lambda b, j: (b * (nb - 1) + j, 0)),
        out_shape=jax.ShapeDtypeStruct((batch * (tp - CHUNK), d), F32),
        compiler_params=_cparams("parallel", "parallel"),
        name="final_norm",
    )(x, g.reshape(1, d)).reshape(batch, tp - CHUNK, d)


def _mm_body(a_ref, w_ref, o_ref):
    o_ref[...] = jnp.dot(a_ref[...], w_ref[...], preferred_element_type=F32).astype(o_ref.dtype)


def _matmul(a, w, out_dtype=F32, tm_prefs=(1408, 768, 512, 384, 256, 128), tn_prefs=(512, 256, 128)):
    m, k = a.shape
    n = w.shape[1]
    tm = _pick(m, tm_prefs)
    tn = _pick(n, tn_prefs)
    return pl.pallas_call(
        _mm_body,
        grid=(m // tm, n // tn),
        in_specs=[pl.BlockSpec((tm, k), lambda i, j: (i, 0)), pl.BlockSpec((k, tn), lambda i, j: (0, j))],
        out_specs=pl.BlockSpec((tm, tn), lambda i, j: (i, j)),
        out_shape=jax.ShapeDtypeStruct((m, n), out_dtype),
        compiler_params=_cparams("parallel", "arbitrary"),
        name="matmul",
    )(a, w)


def _mm_res_body(a_ref, w_ref, r_ref, o_ref, *, blocks_per_seq, pad):
    y = r_ref[...] + jnp.dot(a_ref[...], w_ref[...], preferred_element_type=F32)
    if pad:
        first = (pl.program_id(0) % blocks_per_seq) == 0
        row = lax.broadcasted_iota(jnp.int32, y.shape, 0)
        y = jnp.where(jnp.logical_and(first, row < pad), 0.0, y)
    o_ref[...] = y


def _matmul_residual(a, w, res, tp=None):
    m, k = a.shape
    n = w.shape[1]
    tm = _pick(tp, (768, 384, 128)) if tp else m
    assert m % tm == 0
    tn = _pick(n, (512, 256, 128))
    body = functools.partial(_mm_res_body, blocks_per_seq=(tp // tm if tp else 1), pad=(PAD_FRONT if tp else 0))
    return pl.pallas_call(
        body,
        grid=(m // tm, n // tn),
        in_specs=[pl.BlockSpec((tm, k), lambda i, j: (i, 0)), pl.BlockSpec((k, tn), lambda i, j: (0, j)),
                  pl.BlockSpec((tm, tn), lambda i, j: (i, j))],
        out_specs=pl.BlockSpec((tm, tn), lambda i, j: (i, j)),
        out_shape=jax.ShapeDtypeStruct((m, n), F32),
        compiler_params=_cparams("parallel", "arbitrary"),
        name="matmul_residual",
    )(a, w, res)


def _rope_halves(x, cos, sin):
    half = x.shape[-1] // 2
    x1, x2 = x[:, :half], x[:, half:]
    return jnp.concatenate([x1 * cos - x2 * sin, x1 * sin + x2 * cos], axis=-1)


def _head_norm(y, eps):
    mu = jnp.mean(y, axis=-1, keepdims=True)
    yc = y - mu
    return yc * lax.rsqrt(jnp.mean(yc * yc, axis=-1, keepdims=True) + eps)


def _silu(x):
    return x / (1.0 + jnp.exp(-x))


def _sigmoid(x):
    return 1.0 / (1.0 + jnp.exp(-x))


def _ret_body(q_ref, k_ref, v_ref, g_ref, cos_ref, sin_ref, dm_ref, cd_ref, kd_ref, sd_ref, o_ref, s_ref):
    c = pl.program_id(2)

    @pl.when(c == 0)
    def _():
        s_ref[...] = jnp.zeros(s_ref.shape, F32)

    cos = cos_ref[...]
    sin = sin_ref[...]
    q = _rope_halves(q_ref[...], cos, sin)
    k = _rope_halves(k_ref[...], cos, sin) * (RET_DK ** -0.5)
    qb = q.astype(BF16)
    kb = k.astype(BF16)
    vb = v_ref[...].astype(BF16)
    s_old = s_ref[0, 0]
    scores = lax.dot_general(qb, kb, (((1,), (1,)), ((), ())), preferred_element_type=F32) * dm_ref[0]
    intra = jnp.dot(scores.astype(BF16), vb, preferred_element_type=F32)
    cross = jnp.dot(qb, s_old.astype(BF16), preferred_element_type=F32) * cd_ref[0]
    kdec = (k * kd_ref[0]).astype(BF16)
    s_ref[0, 0] = sd_ref[0] * s_old + lax.dot_general(kdec, vb, (((0,), (0,)), ((), ())), preferred_element_type=F32)
    o = _head_norm(intra + cross, GN_EPS) * _silu(g_ref[...])
    o_ref[...] = o.astype(o_ref.dtype)


def _ret_tables(length):
    lg = jnp.log1p(-jnp.power(2.0, -5.0 - jnp.arange(RET_HEADS, dtype=F32)))
    i = jnp.arange(length, dtype=F32)
    diff = i[:, None] - i[None, :]
    dmask = jnp.where(diff >= 0, jnp.exp(lg[:, None, None] * jnp.maximum(diff, 0.0)), 0.0)
    cdec = jnp.exp(lg[:, None] * (i + 1.0)[None, :])
    kdec = jnp.exp(lg[:, None] * (length - 1.0 - i)[None, :])
    sdec = jnp.exp(lg * length)
    return lg, dmask, cdec, kdec, sdec


def _rope_tables(pos):
    half = RET_DK // 2
    inv = ROPE_BASE ** (-jnp.arange(half, dtype=F32) / half)
    ang = pos.astype(F32)[:, None] * inv[None, :]
    return jnp.cos(ang), jnp.sin(ang)


def _retention_prompt(proj, batch, tp):
    n = proj.shape[0]
    nc = tp // CHUNK
    _, dmask, cdec, kdec, sdec = _ret_tables(CHUNK)
    cdec = jnp.broadcast_to(cdec[:, :, None], (RET_HEADS, CHUNK, RET_DV))
    kdec = jnp.broadcast_to(kdec[:, :, None], (RET_HEADS, CHUNK, RET_DK))
    sdec = jnp.broadcast_to(sdec[:, None, None], (RET_HEADS, 1, RET_DV))
    cos, sin = _rope_tables(jnp.arange(tp, dtype=jnp.int32) - PAD_FRONT)
    blk = lambda col: pl.BlockSpec((CHUNK, RET_DK), lambda b, h, c, col=col: (b * nc + c, col // RET_DK + h))
    tab = lambda shape: pl.BlockSpec((1,) + shape, lambda b, h, c: (h, 0, 0))
    rope = pl.BlockSpec((CHUNK, RET_DK // 2), lambda b, h, c: (c, 0))
    return pl.pallas_call(
        _ret_body,
        grid=(batch, RET_HEADS, nc),
        in_specs=[blk(COL_RQ), blk(COL_RK), blk(COL_RV), blk(COL_RG), rope, rope,
                  tab((CHUNK, CHUNK)), tab((CHUNK, RET_DV)), tab((CHUNK, RET_DK)), tab((1, RET_DV))],
        out_specs=[pl.BlockSpec((CHUNK, RET_DV), lambda b, h, c: (b * nc + c, h)),
                   pl.BlockSpec((1, 1, RET_DK, RET_DV), lambda b, h, c: (b, h, 0, 0))],
        out_shape=[jax.ShapeDtypeStruct((n, RET_W), BF16), jax.ShapeDtypeStruct((batch, RET_HEADS, RET_DK, RET_DV), F32)],
        compiler_params=_cparams("parallel", "parallel", "arbitrary"),
        name="retention_prompt",
    )(proj, proj, proj, proj, cos, sin, dmask, cdec, kdec, sdec)


def _fox_body(qi_ref, ki_ref, q_ref, k_ref, v_ref, qx_ref, kx_ref, o_ref, m_sc, l_sc, acc_sc, *, tq, tk, hp):
    step = pl.program_id(2)
    qi = qi_ref[step]
    ki = ki_ref[step]

    @pl.when(ki == 0)
    def _():
        m_sc[...] = jnp.full(m_sc.shape, NEG_BIG, F32)
        l_sc[...] = jnp.zeros(l_sc.shape, F32)
        acc_sc[...] = jnp.zeros(acc_sc.shape, F32)

    def update(masked):
        if masked:
            qpos = qi * tq + lax.broadcasted_iota(jnp.int32, (tq, tk), 0)
            kpos = ki * tk + lax.broadcasted_iota(jnp.int32, (tq, tk), 1)
            valid = jnp.logical_and(kpos <= qpos, kpos >= PAD_FRONT)
        m_old, l_old, acc_old = m_sc[...], l_sc[...], acc_sc[...]
        m_out, l_out, acc_out = [], [], []
        for h in range(hp):
            sl = slice(h * FOX_HD, (h + 1) * FOX_HD)
            qa = jnp.concatenate([q_ref[:, sl].astype(BF16), qx_ref[0, h]], axis=1)
            ka = jnp.concatenate([k_ref[:, sl].astype(BF16), kx_ref[0, h]], axis=1)
            s = lax.dot_general(qa, ka, _NT, preferred_element_type=F32) * (FOX_HD ** -0.5 * LOG2E)
            if masked:
                s = jnp.where(valid, s, NEG_BIG)
            m_new = jnp.maximum(m_old[h], jnp.max(s, axis=1, keepdims=True))
            alpha = jnp.exp2(m_old[h] - m_new)
            p = jnp.exp2(s - m_new)
            m_out.append(m_new)
            l_out.append(alpha * l_old[h] + jnp.sum(p, axis=1, keepdims=True))
            acc_out.append(alpha * acc_old[:, sl] + jnp.dot(p.astype(BF16), v_ref[:, sl].astype(BF16),
                                                            preferred_element_type=F32))
        for h in range(hp):
            m_sc[h] = m_out[h]
            l_sc[h] = l_out[h]
            acc_sc[:, h * FOX_HD:(h + 1) * FOX_HD] = acc_out[h]

    edge = jnp.logical_or(ki == qi, ki == 0)
    pl.when(edge)(functools.partial(update, True))
    pl.when(jnp.logical_not(edge))(functools.partial(update, False))

    @pl.when(ki == qi)
    def _():
        for h in range(hp):
            sl = slice(h * FOX_HD, (h + 1) * FOX_HD)
            o_ref[:, sl] = (acc_sc[:, sl] / l_sc[h]).astype(o_ref.dtype)


def _split3(x):
    hi = x.astype(BF16)
    r1 = x - hi.astype(F32)
    mid = r1.astype(BF16)
    return hi, mid, (r1 - mid.astype(F32)).astype(BF16)


def _fox_prompt(proj, c, batch, tp):
    n = proj.shape[0]
    tq = _pick(tp, (384, 256, 128))
    nq = tp // tq
    hp = FOX_PAIR
    pairs = [(i, j) for i in range(nq) for j in range(i + 1)]
    qi_tab = jnp.asarray(np.array([p[0] for p in pairs], np.int32))
    ki_tab = jnp.asarray(np.array([p[1] for p in pairs], np.int32))
    hi, mid, lo = _split3(c * (FOX_HD ** 0.5))
    one = jnp.ones_like(hi)
    fill = jnp.zeros(c.shape + (FOX_HD - 6,), BF16)
    qx = jnp.concatenate([jnp.stack([hi, mid, lo, one, one, one], axis=-1), fill], axis=-1)
    kx = jnp.concatenate([jnp.stack([one, one, one, -hi, -mid, -lo], axis=-1), fill], axis=-1)
    cb = lambda col: col // (hp * FOX_HD)
    wide = hp * FOX_HD
    grid_spec = pltpu.PrefetchScalarGridSpec(
        num_scalar_prefetch=2,
        grid=(batch, FOX_HEADS // hp, len(pairs)),
        in_specs=[
            pl.BlockSpec((tq, wide), lambda b, h, s, qi, ki: (b * nq + qi[s], cb(COL_FQ) + h)),
            pl.BlockSpec((tq, wide), lambda b, h, s, qi, ki: (b * nq + ki[s], cb(COL_FK) + h)),
            pl.BlockSpec((tq, wide), lambda b, h, s, qi, ki: (b * nq + ki[s], cb(COL_FV) + h)),
            pl.BlockSpec((1, hp, tq, FOX_HD), lambda b, h, s, qi, ki: (b, h, qi[s], 0)),
            pl.BlockSpec((1, hp, tq, FOX_HD), lambda b, h, s, qi, ki: (b, h, ki[s], 0)),
        ],
        out_specs=pl.BlockSpec((tq, wide), lambda b, h, s, qi, ki: (b * nq + qi[s], h)),
        scratch_shapes=[pltpu.VMEM((hp, tq, 1), F32), pltpu.VMEM((hp, tq, 1), F32), pltpu.VMEM((tq, wide), F32)],
    )
    return pl.pallas_call(
        functools.partial(_fox_body, tq=tq, tk=tq, hp=hp),
        grid_spec=grid_spec,
        out_shape=jax.ShapeDtypeStruct((n, FOX_W), BF16),
        compiler_params=_cparams("parallel", "parallel", "arbitrary"),
        name="fox_prompt",
    )(qi_tab, ki_tab, proj, proj, proj, qx, kx)


def _softplus(z):
    return jnp.maximum(z, 0.0) + jnp.log1p(jnp.exp(-jnp.abs(z)))


def _rwkv_prep_math(c, prev, mu, w0, w2, a0, a2, g2, kkp, ka, exact):
    w = RWKV_W
    xm = c + mu * (prev - c)
    r, k, v = xm[:, 0:w], xm[:, w:2 * w], xm[:, 2 * w:3 * w]
    wd = xm[:, 3 * w:3 * w + RWKV_W_RANK]
    ad = xm[:, 3 * w + RWKV_W_RANK:3 * w + RWKV_W_RANK + RWKV_A_RANK]
    gd = xm[:, 3 * w + RWKV_W_RANK + RWKV_A_RANK:]
    if exact:
        mm = lambda x, m: jnp.dot(x, m, precision=HIGHEST, preferred_element_type=F32)
    else:
        mm = lambda x, m: jnp.dot(x.astype(BF16), m.astype(BF16), preferred_element_type=F32)
    w_log = -_softplus(-(w0 + mm(jnp.tanh(wd), w2))) - 0.5
    lw = -jnp.exp(w_log)
    a = _sigmoid(a0 + mm(ad, a2))
    g = mm(_sigmoid(gd), g2)
    kk0 = k * kkp
    kmod = k * (1.0 + (a - 1.0) * ka)
    return r, kmod, v, lw, kk0, a, g


def _rwkv_prep_body(c_ref, mu_ref, w0_ref, w2_ref, a0_ref, a2_ref, g2_ref, kkp_ref, ka_ref,
                    r_ref, k_ref, v_ref, lw_ref, kk_ref, a_ref, g_ref, carry):
    t = pl.program_id(1)

    @pl.when(t == 0)
    def _():
        carry[...] = jnp.zeros(carry.shape, F32)

    c = c_ref[...]
    rows = c.shape[0]
    prev = pltpu.roll(c, 1, axis=0)
    row = lax.broadcasted_iota(jnp.int32, c.shape, 0)
    prev = jnp.where(row == 0, carry[...], prev)
    carry[...] = c[rows - 1:rows, :]
    outs = _rwkv_prep_math(c, prev, mu_ref[...], w0_ref[...], w2_ref[...], a0_ref[...], a2_ref[...], g2_ref[...],
                           kkp_ref[...], ka_ref[...], exact=False)
    for ref, val in zip((r_ref, k_ref, v_ref, lw_ref, kk_ref, a_ref, g_ref), outs):
        ref[...] = val


def _rwkv_prep_prompt(proj, lp, batch, tp):
    n = proj.shape[0]
    tb = CHUNK
    nt = tp // tb
    row = lambda x: x.reshape(1, -1)
    full = lambda shape: pl.BlockSpec(shape, lambda b, t: (0, 0))
    out_spec = pl.BlockSpec((tb, RWKV_W), lambda b, t: (b * nt + t, 0))
    return pl.pallas_call(
        _rwkv_prep_body,
        grid=(batch, nt),
        in_specs=[pl.BlockSpec((tb, RWKV_PROJ), lambda b, t: (b * nt + t, COL_RW // RWKV_PROJ)),
                  full((1, RWKV_PROJ)), full((1, RWKV_W)), full((RWKV_W_RANK, RWKV_W)), full((1, RWKV_W)),
                  full((RWKV_A_RANK, RWKV_W)), full((RWKV_G_RANK, RWKV_W)), full((1, RWKV_W)), full((1, RWKV_W))],
        out_specs=[out_spec] * 7,
        out_shape=[jax.ShapeDtypeStruct((n, RWKV_W), F32)] * 7,
        scratch_shapes=[pltpu.VMEM((1, RWKV_PROJ), F32)],
        compiler_params=_cparams("parallel", "arbitrary"),
        name="rwkv_prep",
    )(proj, row(lp['mu']), row(lp['w0']), lp['w2'], row(lp['a0']), lp['a2'], lp['g2'], row(lp['kk']), row(lp['ka']))


def _rwkv_chunk_body(r_ref, k_ref, v_ref, lw_ref, kk_ref, a_ref, g_ref, rk_ref, lnw_ref, lnb_ref, o_ref, s_ref):
    cn = pl.program_id(1)

    @pl.when(cn == 0)
    def _():
        s_ref[...] = jnp.zeros(s_ref.shape, F32)

    cs = RWKV_CHUNK
    hd = RWKV_HD
    gh = RWKV_GROUP
    n = gh * cs
    gw = gh * hd
    bits = int(np.log2(cs))
    row = lax.broadcasted_iota(jnp.int32, (n, n), 0)
    col = lax.broadcasted_iota(jnp.int32, (n, n), 1)
    same_head = (row >> bits) == (col >> bits)
    strict = same_head & (col < row)
    incl = same_head & (col <= row)
    eye = (row == col).astype(F32)
    pair_masks = [((row >> (bit + 1)) == (col >> (bit + 1))) & ((row & (1 << bit)) != 0) & ((col & (1 << bit)) == 0)
                  for bit in range(bits)]
    trow = lax.broadcasted_iota(jnp.int32, (cs, cs), 0)
    tcol = lax.broadcasted_iota(jnp.int32, (cs, cs), 1)
    cum_all = jnp.dot((tcol <= trow).astype(F32), lw_ref[...], precision=HIGHEST, preferred_element_type=F32)
    decay_all = jnp.exp(cum_all[cs - 1:cs, :])
    tail_all = jnp.exp(cum_all[cs - 1:cs, :] - cum_all)
    dot = lambda x, y: jnp.dot(x.astype(BF16), y.astype(BF16), preferred_element_type=F32)
    dot_nt = lambda x, y: lax.dot_general(x.astype(BF16), y.astype(BF16), _NT, preferred_element_type=F32)
    dot_tn = lambda x, y: lax.dot_general(x.astype(BF16), y.astype(BF16), _TN, preferred_element_type=F32)
    for g in range(RWKV_HEADS // gh):
        lanes = slice(g * gw, (g + 1) * gw)
        stack = lambda x: jnp.concatenate([x[:, g * gw + h * hd:g * gw + (h + 1) * hd] for h in range(gh)], axis=0)
        rows = lambda x: jnp.concatenate([jnp.broadcast_to(x[:, g * gw + h * hd:g * gw + (h + 1) * hd], (cs, hd))
                                          for h in range(gh)], axis=0)
        lw, cum, tail = stack(lw_ref[...]), stack(cum_all), stack(tail_all)
        kk0 = stack(kk_ref[...])
        kk = kk0 * lax.rsqrt(jnp.sum(kk0 * kk0, axis=-1, keepdims=True) + 1e-12)
        b = kk * stack(a_ref[...])
        r, k, v = stack(r_ref[...]), stack(k_ref[...]), stack(v_ref[...])
        e_in = jnp.exp(cum)
        e_neg = jnp.exp(-cum)
        ar = jnp.concatenate([-kk * jnp.exp(cum - lw), r * e_in], axis=0)
        bk = jnp.concatenate([b * e_neg, k * e_neg], axis=0)
        gram = dot_nt(ar, bk)
        l_ab = jnp.where(strict, gram[:n, :n], 0.0)
        l_ak = jnp.where(strict, gram[:n, n:], 0.0)
        m_rbk = jnp.concatenate([jnp.where(incl, gram[n:, :n], 0.0), jnp.where(incl, gram[n:, n:], 0.0)], axis=1)
        tinv = eye + jnp.where(pair_masks[0], l_ab, 0.0)
        for mask in pair_masks[1:]:
            tinv = tinv + dot(dot(tinv, jnp.where(mask, l_ab, 0.0)), tinv)
        s_old = s_ref[0, g * n:(g + 1) * n, :]
        ars = dot_nt(ar, s_old)
        v_bd = jnp.where(same_head, jnp.concatenate([v_ref[:, lanes]] * gh, axis=0), 0.0)
        u = dot(tinv, jnp.where(same_head, ars[:n], 0.0) + dot(l_ak, v_bd))
        uv = jnp.concatenate([u, v_bd], axis=0)
        y_bd = jnp.where(same_head, ars[n:], 0.0) + dot(m_rbk, uv)
        s_ref[0, g * n:(g + 1) * n, :] = s_old * rows(decay_all) + dot_tn(uv, jnp.concatenate([b * tail, k * tail], axis=0))
        y = sum(y_bd[:, h * hd:(h + 1) * hd] for h in range(gh))
        yn = _head_norm(y, RWKV_GN_EPS) * rows(lnw_ref[...]) + rows(lnb_ref[...])
        bonus = jnp.sum(r * k * rows(rk_ref[...]), axis=-1, keepdims=True) * v
        out = (yn + bonus) * stack(g_ref[...])
        o_ref[:, lanes] = jnp.concatenate([out[h * cs:(h + 1) * cs] for h in range(gh)], axis=1).astype(o_ref.dtype)


def _rwkv_chunk_prompt(prep, lp, batch, tp):
    n = prep[0].shape[0]
    cs = RWKV_CHUNK
    ncn = tp // cs
    row = lambda x: x.reshape(1, -1)
    blk = pl.BlockSpec((cs, RWKV_W), lambda b, c: (b * ncn + c, 0))
    full = pl.BlockSpec((1, RWKV_W), lambda b, c: (0, 0))
    out, state = pl.pallas_call(
        _rwkv_chunk_body,
        grid=(batch, ncn),
        in_specs=[blk] * 7 + [full] * 3,
        out_specs=[blk, pl.BlockSpec((1, RWKV_W, RWKV_HD), lambda b, c: (b, 0, 0))],
        out_shape=[jax.ShapeDtypeStruct((n, RWKV_W), BF16), jax.ShapeDtypeStruct((batch, RWKV_W, RWKV_HD), F32)],
        compiler_params=_cparams("parallel", "arbitrary"),
        name="rwkv_chunk",
    )(*prep, row(lp['rk']), row(lp['ln_w']), row(lp['ln_b']))
    return out, state.reshape(batch, RWKV_HEADS, RWKV_HD, RWKV_HD)


def _merge_body(ro_ref, fo_ref, wo_ref, wr_ref, wf_ref, ww_ref, g0_ref, g1_ref, g2_ref, o_ref):
    dot = functools.partial(jnp.dot, preferred_element_type=F32)
    m = (_sigmoid(g0_ref[...]) * dot(ro_ref[...], wr_ref[...])
         + _sigmoid(g1_ref[...]) * dot(fo_ref[...], wf_ref[...])
         + _sigmoid(g2_ref[...]) * dot(wo_ref[...], ww_ref[...]))
    o_ref[...] = m.astype(o_ref.dtype)


def _merge(ret_o, fox_o, rw_o, wb_ret, wb_fox, wb_rwkv, gates, gate_col, out_dtype):
    m = ret_o.shape[0]
    tm = _pick(m, (768, 384, 128))
    tn = 512
    nj = D_MODEL // tn
    act = lambda width: pl.BlockSpec((tm, width), lambda i, j: (i, 0))
    wgt = lambda width: pl.BlockSpec((width, tn), lambda i, j: (0, j))
    gate = lambda br: pl.BlockSpec((tm, tn), lambda i, j, br=br: (i, gate_col // tn + br * nj + j))
    return pl.pallas_call(
        _merge_body,
        grid=(m // tm, nj),
        in_specs=[act(RET_W), act(FOX_W), act(RWKV_W), wgt(RET_W), wgt(FOX_W), wgt(RWKV_W), gate(0), gate(1), gate(2)],
        out_specs=pl.BlockSpec((tm, tn), lambda i, j: (i, j)),
        out_shape=jax.ShapeDtypeStruct((m, D_MODEL), out_dtype),
        compiler_params=_cparams("parallel", "arbitrary"),
        name="merge",
    )(ret_o, fox_o, rw_o, wb_ret, wb_fox, wb_rwkv, gates, gates, gates)


def _cast_body(x_ref, o_ref):
    o_ref[...] = x_ref[...].astype(o_ref.dtype)


def _expert_weights_bf16(w, layer):
    _, e, r, c = w.shape
    return pl.pallas_call(
        _cast_body,
        grid=(e,),
        in_specs=[pl.BlockSpec((None, 1, r, c), lambda i: (layer, i, 0, 0))],
        out_specs=pl.BlockSpec((1, r, c), lambda i: (i, 0, 0)),
        out_shape=jax.ShapeDtypeStruct((e, r, c), BF16),
        compiler_params=_cparams("parallel"),
        name="expert_weights_bf16",
    )(w)


def _route(logits, bg, be):
    n = logits.shape[0]
    gp = jax.nn.softmax(logits[:, :N_GROUPS] + bg.astype(F32), axis=-1)
    gidx = jnp.argmax(gp, axis=-1)
    pg = jnp.take_along_axis(gp, gidx[:, None], axis=-1)
    el = logits[:, N_GROUPS:N_GROUPS + N_EXPERTS].reshape(n, N_GROUPS, EXPERTS_PER_GROUP) + be.astype(F32)[None]
    el = jnp.take_along_axis(el, gidx[:, None, None], axis=1)[:, 0]
    topv, topi = lax.top_k(jax.nn.softmax(el, axis=-1), TOP_K)
    gate = pg * topv / jnp.sum(topv, axis=-1, keepdims=True)
    eid = (gidx[:, None] * EXPERTS_PER_GROUP + topi).astype(jnp.int32)
    return eid, gate


def _router_weights(wg, we):
    d = wg.shape[0]
    wr = jnp.concatenate([wg, jnp.transpose(we, (1, 0, 2)).reshape(d, N_EXPERTS)], axis=1)
    return jnp.pad(wr, ((0, 0), (0, ROUTER_COLS - wr.shape[1])))


def _moe_body(be_ref, nused_ref, tok_ref, dst_ref, h_hbm, w1_ref, w3_ref, w2_ref, y_in, y_hbm, xbuf, ybuf, sem_in, sem_out):
    del y_in
    i = pl.program_id(0)
    n_used = nused_ref[0]
    slot = i % 2
    rows = range(MOE_BLOCK)

    def gather(blk, buf, r):
        return pltpu.make_async_copy(h_hbm.at[pl.ds(tok_ref[blk * MOE_BLOCK + r], 1)], xbuf.at[buf, pl.ds(r, 1)],
                                     sem_in.at[buf])

    def scatter(blk, r):
        return pltpu.make_async_copy(ybuf.at[pl.ds(r, 1)], y_hbm.at[pl.ds(dst_ref[blk * MOE_BLOCK + r], 1)], sem_out)

    @pl.when(jnp.logical_and(i == 0, n_used > 0))
    def _():
        for r in rows:
            gather(0, 0, r).start()

    @pl.when(i < n_used)
    def _():
        for r in rows:
            gather(i, slot, r).wait()
        nxt = jnp.minimum(i + 1, n_used - 1)
        for r in rows:
            gather(nxt, 1 - slot, r).start()
        x = xbuf[slot].astype(BF16)
        a = jnp.dot(x, w1_ref[0], preferred_element_type=F32)
        b = jnp.dot(x, w3_ref[0], preferred_element_type=F32)
        y = jnp.dot((_silu(a) * b).astype(BF16), w2_ref[0], preferred_element_type=F32)

        @pl.when(i > 0)
        def _():
            for r in rows:
                scatter(i - 1, r).wait()

        ybuf[...] = y
        for r in rows:
            scatter(i, r).start()

        @pl.when(i == n_used - 1)
        def _():
            for r in rows:
                gather(nxt, 1 - slot, r).wait()
                scatter(i, r).wait()


def _moe_prompt(h, eid, gate, valid, w1, w3, w2):
    n, d = h.shape
    a_tot = n * TOP_K
    n_real = int(np.sum(valid)) * TOP_K
    n_blk = (n_real + N_EXPERTS * (MOE_BLOCK - 1) + MOE_BLOCK - 1) // MOE_BLOCK
    cap = n_blk * MOE_BLOCK
    validf = jnp.repeat(jnp.asarray(valid), TOP_K)
    eflat = eid.reshape(-1)
    onehot = jnp.logical_and(eflat[:, None] == jnp.arange(N_EXPERTS, dtype=jnp.int32)[None, :], validf[:, None]).astype(jnp.int32)
    blocks = onehot.astype(F32).reshape(a_tot // CHUNK, CHUNK, N_EXPERTS)
    below = jnp.tril(jnp.ones((CHUNK, CHUNK), F32), -1)
    inner = jnp.einsum('ij,bjk->bik', below, blocks)
    totals = jnp.sum(blocks, axis=1)
    offset = jnp.cumsum(totals, axis=0) - totals
    before = (inner + offset[:, None, :]).reshape(a_tot, N_EXPERTS).astype(jnp.int32)
    rank = jnp.sum(before * onehot, axis=1)
    counts = jnp.sum(onehot, axis=0)
    padded = (counts + MOE_BLOCK - 1) // MOE_BLOCK * MOE_BLOCK
    pad_end = jnp.cumsum(padded)
    pad_start = pad_end - padded
    dest = jnp.where(validf, pad_start[eflat] + rank, cap)
    assign = jnp.arange(a_tot, dtype=jnp.int32)
    zero_row = int(np.argmin(valid))
    held = jnp.full((cap,), -1, jnp.int32).at[dest].set(assign, mode='drop')
    tok_buf = jnp.where(held >= 0, held // TOP_K, zero_row)
    dump = a_tot + (jnp.arange(cap, dtype=jnp.int32) % MOE_BLOCK)
    dst_buf = jnp.where(held >= 0, (held % TOP_K) * n + held // TOP_K, dump)
    n_used = (pad_end[-1] // MOE_BLOCK).astype(jnp.int32)
    blk = jnp.arange(n_blk, dtype=jnp.int32)
    blk_e = jnp.sum((pad_end[None, :] <= (blk * MOE_BLOCK)[:, None]).astype(jnp.int32), axis=1)
    blk_e = jnp.minimum(blk_e, N_EXPERTS - 1)
    blk_e = jnp.where(blk < n_used, blk_e, blk_e[jnp.maximum(n_used - 1, 0)])
    y_rows = a_tot + MOE_BLOCK
    y0 = jnp.zeros((y_rows, d), F32)
    ff = w1.shape[2]
    grid_spec = pltpu.PrefetchScalarGridSpec(
        num_scalar_prefetch=4,
        grid=(n_blk,),
        in_specs=[pl.BlockSpec(memory_space=pl.ANY),
                  pl.BlockSpec((1, d, ff), lambda i, be, nu, tk, ds: (be[i], 0, 0)),
                  pl.BlockSpec((1, d, ff), lambda i, be, nu, tk, ds: (be[i], 0, 0)),
                  pl.BlockSpec((1, ff, d), lambda i, be, nu, tk, ds: (be[i], 0, 0)),
                  pl.BlockSpec(memory_space=pl.ANY)],
        out_specs=pl.BlockSpec(memory_space=pl.ANY),
        scratch_shapes=[pltpu.VMEM((2, MOE_BLOCK, d), F32), pltpu.VMEM((MOE_BLOCK, d), F32),
                        pltpu.SemaphoreType.DMA((2,)), pltpu.SemaphoreType.DMA(())],
    )
    y2 = pl.pallas_call(
        _moe_body,
        grid_spec=grid_spec,
        out_shape=jax.ShapeDtypeStruct((y_rows, d), F32),
        input_output_aliases={8: 0},
        compiler_params=_cparams("arbitrary"),
        name="moe_experts",
    )(blk_e, n_used.reshape(1), tok_buf, dst_buf, h, w1, w3, w2, y0)
    return y2


def _combine_body(x_ref, y0_ref, y1_ref, g_ref, o_ref):
    g = g_ref[...]
    o_ref[...] = x_ref[...] + (y0_ref[...] * g[:, 0:1] + y1_ref[...] * g[:, 1:2])


def _moe_combine(x, y2, gate):
    n, d = x.shape
    tm = _pick(n, (256, 128, 8))
    gpad = jnp.pad(gate, ((0, 0), (0, 128 - TOP_K)))
    return pl.pallas_call(
        _combine_body,
        grid=(n // tm,),
        in_specs=[pl.BlockSpec((tm, d), lambda i: (i, 0)), pl.BlockSpec((tm, d), lambda i: (i, 0)),
                  pl.BlockSpec((tm, d), lambda i: (n // tm + i, 0)), pl.BlockSpec((tm, 128), lambda i: (i, 0))],
        out_specs=pl.BlockSpec((tm, d), lambda i: (i, 0)),
        out_shape=jax.ShapeDtypeStruct((n, d), F32),
        compiler_params=_cparams("parallel"),
        name="moe_combine",
    )(x, y2, y2, gpad)


def _pack_w_in(w_in):
    ff = jnp.pad(w_in[:, SRC_FF:SRC_FF + FOX_HEADS], ((0, 0), (0, FF_PAD - FOX_HEADS)))
    return jnp.concatenate([w_in[:, SRC_RW:SRC_RW + RWKV_PROJ], w_in[:, :SRC_FF], ff, w_in[:, SRC_GATE:]],
                           axis=1).astype(BF16)


def _prompt_layer(x, lp, moe, batch, tp, valid):
    h = _rmsnorm(x, lp['norm_mix'], BF16)
    proj = _matmul(h, lp['w_in'], tn_prefs=(1536, 1024, 512, 256, 128))
    ret_o, ret_s = _retention_prompt(proj, batch, tp)
    ff = proj[:, COL_FF:COL_FF + FOX_HEADS].reshape(batch, tp, FOX_HEADS)
    logf = jax.nn.log_sigmoid(ff + lp['fox_b'].astype(F32))
    c = jnp.cumsum(logf, axis=1).transpose(0, 2, 1)
    fox_o = _fox_prompt(proj, c, batch, tp)
    prep = _rwkv_prep_prompt(proj, lp, batch, tp)
    rw_o, rw_s = _rwkv_chunk_prompt(prep, lp, batch, tp)
    merged = _merge(ret_o, fox_o, rw_o, lp['wb_ret'], lp['wb_fox'], lp['wb_rwkv'], proj, COL_GATE, BF16)
    x = _matmul_residual(merged, lp['w_out'], x, tp=tp)
    h2, logits = _rmsnorm_router(x, lp['norm_ffn'], moe['wr'])
    eid, gate = _route(logits, moe['bg'], moe['be'])
    y2 = _moe_prompt(h2, eid, gate, valid, moe['w1'], moe['w3'], moe['w2'])
    x = _moe_combine(x, y2, gate)
    p3 = proj.reshape(batch, tp, PROJ_PACKED)
    heads = lambda col: p3[:, PAD_FRONT:, col:col + FOX_W].reshape(batch, tp - PAD_FRONT, FOX_HEADS, FOX_HD)
    state = (heads(COL_FK), heads(COL_FV), logf[:, PAD_FRONT:], ret_s, rw_s, p3[:, tp - 1, COL_RW:COL_RW + RWKV_PROJ])
    return x, state


def _rows16(x):
    return jnp.concatenate([x.astype(BF16), jnp.zeros(x.shape, BF16)], axis=0)


def _mm_sample(x, w):
    return _matmul(_rows16(x), w, tn_prefs=(1536, 1024, 512, 256, 128))[:x.shape[0]]


def _pad8(x):
    first = lax.broadcasted_iota(jnp.int32, (8, x.shape[1]), 0) == 0
    return jnp.where(first, jnp.broadcast_to(x, (8, x.shape[1])), 0.0)


def _ret_sample_body(q_ref, k_ref, v_ref, g_ref, cos_ref, sin_ref, dec_ref, s0_ref, o_ref, s_ref):
    b = pl.program_id(1)
    row1 = lambda ref: ref[pl.ds(b, 1), :]
    cos = cos_ref[...]
    sin = sin_ref[...]
    q = _rope_halves(row1(q_ref), cos, sin)
    k = _rope_halves(row1(k_ref), cos, sin) * (RET_DK ** -0.5)
    v = row1(v_ref)
    dec = dec_ref[0]
    s0 = s0_ref[0, 0]
    rnd = lambda x: x.astype(BF16).astype(F32)
    cross = jnp.dot(_pad8(q).astype(BF16), s0.astype(BF16), preferred_element_type=F32)[0:1] * dec
    intra = jnp.sum(rnd(q) * rnd(k), axis=-1, keepdims=True) * v
    s_ref[0, 0] = dec * s0 + lax.dot_general(_pad8(k), _pad8(v), _TN, precision=HIGHEST, preferred_element_type=F32)
    o_ref[pl.ds(b, 1), :] = _head_norm(intra + cross, GN_EPS) * _silu(row1(g_ref))


def _retention_sample(proj, s0, pos):
    nb = proj.shape[0]
    lg = _ret_tables(1)[0]
    dec = jnp.broadcast_to(jnp.exp(lg)[:, None, None], (RET_HEADS, 1, RET_DV))
    cos, sin = _rope_tables(pos)
    blk = lambda col: pl.BlockSpec((nb, RET_DK), lambda h, b, col=col: (0, col // RET_DK + h))
    rope = pl.BlockSpec((1, RET_DK // 2), lambda h, b: (0, 0))
    st = pl.BlockSpec((1, 1, RET_DK, RET_DV), lambda h, b: (b, h, 0, 0))
    return pl.pallas_call(
        _ret_sample_body,
        grid=(RET_HEADS, nb),
        in_specs=[blk(COL_RQ), blk(COL_RK), blk(COL_RV), blk(COL_RG), rope, rope,
                  pl.BlockSpec((1, 1, RET_DV), lambda h, b: (h, 0, 0)), st],
        out_specs=[pl.BlockSpec((nb, RET_DV), lambda h, b: (0, h)), st],
        out_shape=[jax.ShapeDtypeStruct((nb, RET_W), F32), jax.ShapeDtypeStruct(s0.shape, F32)],
        compiler_params=_cparams("parallel", "arbitrary"),
        name="retention_sample",
    )(proj, proj, proj, proj, cos, sin, dec, s0)


def _rwkv_prep_sample_body(c_ref, prev_ref, mu_ref, w0_ref, w2_ref, a0_ref, a2_ref, g2_ref, kkp_ref, ka_ref, *out_refs):
    outs = _rwkv_prep_math(c_ref[...], prev_ref[...], mu_ref[...], w0_ref[...], w2_ref[...], a0_ref[...], a2_ref[...],
                           g2_ref[...], kkp_ref[...], ka_ref[...], exact=False)
    for ref, val in zip(out_refs, outs):
        ref[...] = val


def _rwkv_step_body(r_ref, k_ref, v_ref, lw_ref, kk_ref, a_ref, g_ref, rk_ref, lnw_ref, lnb_ref, s0_ref, o_ref, s_ref):
    b = pl.program_id(0)
    dg = functools.partial(lax.dot_general, precision=HIGHEST, preferred_element_type=F32)
    r_all, k_all, v_all, lw_all, kk_all, a_all, g_all = (
        ref[pl.ds(b, 1), :] for ref in (r_ref, k_ref, v_ref, lw_ref, kk_ref, a_ref, g_ref))
    rk_all, lnw_all, lnb_all = rk_ref[...], lnw_ref[...], lnb_ref[...]
    outs = []
    for h in range(RWKV_HEADS):
        sl = slice(h * RWKV_HD, (h + 1) * RWKV_HD)
        kk0 = kk_all[:, sl]
        kk = kk0 * lax.rsqrt(jnp.sum(kk0 * kk0, axis=-1, keepdims=True) + 1e-12)
        bb = kk * a_all[:, sl]
        w = jnp.exp(lw_all[:, sl])
        r, k, v = r_all[:, sl], k_all[:, sl], v_all[:, sl]
        s0 = s0_ref[0, h]
        sa = lax.dot_general(s0.astype(BF16), _pad8(-kk).astype(BF16), _NT, preferred_element_type=F32)[:, 0:1]
        s_new = s0 * w + sa * bb + dg(_pad8(v), _pad8(k), _TN)
        s_ref[0, h] = s_new
        y = lax.dot_general(_pad8(r).astype(BF16), s_new.astype(BF16), _NT, preferred_element_type=F32)[0:1]
        yn = _head_norm(y, RWKV_GN_EPS) * lnw_all[:, sl] + lnb_all[:, sl]
        bonus = jnp.sum(r * k * rk_all[:, sl], axis=-1, keepdims=True) * v
        outs.append((yn + bonus) * g_all[:, sl])
    o_ref[pl.ds(b, 1), :] = jnp.concatenate(outs, axis=1)


def _rwkv_sample(proj, lp, s0, shift0):
    nb = proj.shape[0]
    row = lambda x: x.reshape(1, -1)
    prep = pl.pallas_call(
        _rwkv_prep_sample_body,
        grid=(1,),
        in_specs=[pl.BlockSpec((nb, RWKV_PROJ), lambda i: (0, COL_RW // RWKV_PROJ)), pl.BlockSpec((nb, RWKV_PROJ), lambda i: (0, 0)),
                  pl.BlockSpec((1, RWKV_PROJ), lambda i: (0, 0)), pl.BlockSpec((1, RWKV_W), lambda i: (0, 0)),
                  pl.BlockSpec((RWKV_W_RANK, RWKV_W), lambda i: (0, 0)), pl.BlockSpec((1, RWKV_W), lambda i: (0, 0)),
                  pl.BlockSpec((RWKV_A_RANK, RWKV_W), lambda i: (0, 0)), pl.BlockSpec((RWKV_G_RANK, RWKV_W), lambda i: (0, 0)),
                  pl.BlockSpec((1, RWKV_W), lambda i: (0, 0)), pl.BlockSpec((1, RWKV_W), lambda i: (0, 0))],
        out_specs=[pl.BlockSpec((nb, RWKV_W), lambda i: (0, 0))] * 7,
        out_shape=[jax.ShapeDtypeStruct((nb, RWKV_W), F32)] * 7,
        compiler_params=_cparams("arbitrary"),
        name="rwkv_prep_sample",
    )(proj, shift0, row(lp['mu']), row(lp['w0']), lp['w2'], row(lp['a0']), lp['a2'], lp['g2'], row(lp['kk']), row(lp['ka']))
    act = pl.BlockSpec((nb, RWKV_W), lambda b: (0, 0))
    par = pl.BlockSpec((1, RWKV_W), lambda b: (0, 0))
    st = pl.BlockSpec((1, RWKV_HEADS, RWKV_HD, RWKV_HD), lambda b: (b, 0, 0, 0))
    return pl.pallas_call(
        _rwkv_step_body,
        grid=(nb,),
        in_specs=[act] * 7 + [par] * 3 + [st],
        out_specs=[act, st],
        out_shape=[jax.ShapeDtypeStruct((nb, RWKV_W), F32), jax.ShapeDtypeStruct(s0.shape, F32)],
        compiler_params=_cparams("arbitrary"),
        name="rwkv_step",
    )(*prep, row(lp['rk']), row(lp['ln_w']), row(lp['ln_b']), s0)


def _fox_decode_body(pt_ref, q_ref, kn_ref, vn_ref, bias_ref, *refs, npg):
    del pt_ref
    k_refs, v_refs = refs[:npg], refs[npg:2 * npg]
    o_ref, s_sc, m_sc, l_sc, acc_sc = refs[2 * npg:]
    phase = pl.program_id(1)
    j = pl.program_id(2)
    last = pl.num_programs(2) - 1
    scale = FOX_HD ** -0.5
    rnd = lambda x: x.astype(BF16).astype(F32)

    @pl.when(jnp.logical_and(phase == 0, j == 0))
    def _():
        m_sc[...] = jnp.sum(rnd(q_ref[0]) * rnd(kn_ref[0]), axis=-1, keepdims=True) * scale
        l_sc[...] = jnp.ones(l_sc.shape, F32)

    @pl.when(phase == 0)
    def _():
        qb = q_ref[0].astype(BF16)
        for g in range(npg):
            rows = k_refs[g].shape[2] * FOX_HEADS
            kf = k_refs[g][0, 0].reshape(rows, FOX_HD).astype(BF16)
            s = lax.dot_general(qb, kf, _NT, preferred_element_type=F32) * scale + bias_ref[0, g]
            s_sc[j * npg + g] = s
            m_old = m_sc[...]
            m_new = jnp.maximum(m_old, jnp.max(s, axis=-1, keepdims=True))
            l_sc[...] = jnp.exp(m_old - m_new) * l_sc[...] + jnp.sum(jnp.exp(s - m_new), axis=-1, keepdims=True)
            m_sc[...] = m_new

    @pl.when(phase == 1)
    def _():
        @pl.when(j == 0)
        def _():
            self_score = jnp.sum(rnd(q_ref[0]) * rnd(kn_ref[0]), axis=-1, keepdims=True) * scale
            acc_sc[...] = rnd(jnp.exp(self_score - m_sc[...]) / l_sc[...]) * rnd(vn_ref[0])

        for g in range(npg):
            rows = v_refs[g].shape[2] * FOX_HEADS
            vf = v_refs[g][0, 0].reshape(rows, FOX_HD).astype(BF16)
            p = jnp.exp(s_sc[j * npg + g] - m_sc[...]) / l_sc[...]
            acc_sc[...] += jnp.dot(p.astype(BF16), vf, preferred_element_type=F32)

        @pl.when(j == last)
        def _():
            o_ref[0] = acc_sc[...]


def _fox_decode(q, k_new, v_new, logf_new, cache_k, cache_v, cache_logf, page_table, layer):
    nb, n_pages = page_table.shape
    page = cache_k.shape[2]
    npg = _pick(n_pages, (16, 8, 4, 2, 1))
    plogf = cache_logf[layer][page_table].astype(F32)
    totals = jnp.sum(plogf, axis=2)
    later = lax.cumsum(totals, axis=1, reverse=True) - totals
    dsuf = lax.cumsum(plogf, axis=2, reverse=True) - plogf + later[:, :, None, :]
    bias = dsuf + logf_new[:, None, None, :]
    own = jnp.eye(FOX_HEADS, dtype=bool)[None, None, :, None, :]
    bias = jnp.where(own, bias[:, :, None, :, :], NEG_BIG).reshape(nb, n_pages, FOX_HEADS, page * FOX_HEADS)
    nst = n_pages // npg
    tok = pl.BlockSpec((1, FOX_HEADS, FOX_HD), lambda b, ph, j, pt: (b, 0, 0))
    k_step = lambda ph, j: j * (1 - ph) + (nst - 1) * ph
    v_step = lambda ph, j: j * ph
    page_spec = lambda step, g: pl.BlockSpec(
        (1, 1, page, FOX_HEADS, FOX_HD), lambda b, ph, j, pt: (layer, pt[b, step(ph, j) * npg + g], 0, 0, 0))
    grid_spec = pltpu.PrefetchScalarGridSpec(
        num_scalar_prefetch=1,
        grid=(nb, 2, nst),
        in_specs=[tok, tok, tok,
                  pl.BlockSpec((1, npg, FOX_HEADS, page * FOX_HEADS), lambda b, ph, j, pt: (b, k_step(ph, j), 0, 0))]
                 + [page_spec(k_step, g) for g in range(npg)] + [page_spec(v_step, g) for g in range(npg)],
        out_specs=tok,
        scratch_shapes=[pltpu.VMEM((n_pages, FOX_HEADS, page * FOX_HEADS), F32), pltpu.VMEM((FOX_HEADS, 1), F32),
                        pltpu.VMEM((FOX_HEADS, 1), F32), pltpu.VMEM((FOX_HEADS, FOX_HD), F32)],
    )
    r3 = lambda x: x.reshape(nb, FOX_HEADS, FOX_HD)
    o = pl.pallas_call(
        functools.partial(_fox_decode_body, npg=npg),
        grid_spec=grid_spec,
        out_shape=jax.ShapeDtypeStruct((nb, FOX_HEADS, FOX_HD), F32),
        compiler_params=_cparams("parallel", "arbitrary", "arbitrary"),
        name="fox_decode",
    )(page_table, r3(q), r3(k_new), r3(v_new), bias, *([cache_k] * npg), *([cache_v] * npg))
    return o.reshape(nb, FOX_W)


def _moe_sample_body(e_ref, h_ref, w1_ref, w3_ref, w2_ref, wv_ref, o_ref):
    del e_ref
    m = o_ref.shape[0]

    @pl.when(pl.program_id(0) == 0)
    def _():
        o_ref[...] = jnp.zeros(o_ref.shape, F32)

    h2 = h_ref[...]
    a = jnp.dot(h2, w1_ref[0], preferred_element_type=F32)
    b = jnp.dot(h2, w3_ref[0], preferred_element_type=F32)
    y = jnp.dot((_silu(a) * b).astype(BF16), w2_ref[0], preferred_element_type=F32)
    o_ref[...] += wv_ref[0][:, 0:1] * y[:m]


def _moe_sample(h, eid, gate, w1, w3, w2):
    m, d = h.shape
    na = m * TOP_K
    order = jnp.argsort(eid.reshape(-1))
    e_sorted = eid.reshape(-1)[order].astype(jnp.int32)
    wv = jnp.zeros((na, m), F32).at[jnp.arange(na), order // TOP_K].set(gate.reshape(-1)[order])
    wv = jnp.broadcast_to(wv[:, :, None], (na, m, 128))
    ff = w1.shape[2]
    grid_spec = pltpu.PrefetchScalarGridSpec(
        num_scalar_prefetch=1,
        grid=(na,),
        in_specs=[pl.BlockSpec((2 * m, d), lambda s, e: (0, 0)),
                  pl.BlockSpec((1, d, ff), lambda s, e: (e[s], 0, 0)),
                  pl.BlockSpec((1, d, ff), lambda s, e: (e[s], 0, 0)),
                  pl.BlockSpec((1, ff, d), lambda s, e: (e[s], 0, 0)),
                  pl.BlockSpec((1, m, 128), lambda s, e: (s, 0, 0))],
        out_specs=pl.BlockSpec((m, d), lambda s, e: (0, 0)),
    )
    return pl.pallas_call(
        _moe_sample_body,
        grid_spec=grid_spec,
        out_shape=jax.ShapeDtypeStruct((m, d), F32),
        compiler_params=_cparams("arbitrary"),
        name="moe_sample",
    )(e_sorted, _rows16(h), w1, w3, w2, wv)


def _sample_layer(x, lp, moe, layer, cache_k, cache_v, cache_logf, page_table, s_ret, s_rwkv, s_shift, pos):
    nb = x.shape[0]
    proj = _mm_sample(_rmsnorm(x, lp['norm_mix'], F32), lp['w_in'])
    ret_o, ret_s = _retention_sample(proj, s_ret, pos)
    logf = jax.nn.log_sigmoid(proj[:, COL_FF:COL_FF + FOX_HEADS] + lp['fox_b'].astype(F32))
    fk, fv = proj[:, COL_FK:COL_FK + FOX_W], proj[:, COL_FV:COL_FV + FOX_W]
    fox_o = _fox_decode(proj[:, COL_FQ:COL_FQ + FOX_W], fk, fv, logf, cache_k, cache_v, cache_logf, page_table, layer)
    rw_o, rw_s = _rwkv_sample(proj, lp, s_rwkv, s_shift)
    g2 = jnp.concatenate([proj[:, COL_GATE:]] * 2, axis=0)
    m2 = _merge(_rows16(ret_o), _rows16(fox_o), _rows16(rw_o), lp['wb_ret'], lp['wb_fox'], lp['wb_rwkv'], g2, 0, F32)
    x = x + _mm_sample(m2[:nb], lp['w_out'])
    h2, logits = _rmsnorm_router(x, lp['norm_ffn'], moe['wr'])
    eid, gate = _route(logits, moe['bg'], moe['be'])
    x = x + _moe_sample(h2, eid, gate, moe['w1'], moe['w3'], moe['w2'])
    heads = lambda a: a.reshape(nb, 1, FOX_HEADS, FOX_HD)
    state = (heads(fk), heads(fv), logf.reshape(nb, 1, FOX_HEADS), ret_s, rw_s, proj[:, COL_RW:COL_RW + RWKV_PROJ])
    return x, state


def kernel(x_prompt, x_sample, cache_k, cache_v, cache_logf, page_table, state_ret, state_rwkv, state_shift,
           meta_tokens, norm_mix, norm_ffn, norm_final, w_in, fox_forget_bias,
           rwkv_mu, rwkv_w0, rwkv_w2, rwkv_a0, rwkv_a2, rwkv_g2, rwkv_kk, rwkv_ka, rwkv_rk, rwkv_ln_w, rwkv_ln_b,
           w_branch_ret, w_branch_fox, w_branch_rwkv, w_out,
           router_group_w, router_group_b, router_expert_w, router_expert_b, expert_w1, expert_w3, expert_w2):
    batch, s_len, d = x_prompt.shape
    nb, n_new, _ = x_sample.shape
    assert n_new == 1 and d == D_MODEL and s_len % CHUNK == 0
    depth = w_in.shape[0]
    tp = PAD_FRONT + N_META + s_len
    past_len = page_table.shape[1] * cache_k.shape[2]
    valid = np.tile(np.arange(tp) >= PAD_FRONT, batch)
    xp = jnp.concatenate([jnp.zeros((batch, PAD_FRONT, d), F32),
                          jnp.broadcast_to(meta_tokens[None].astype(F32), (batch, N_META, d)), x_prompt], axis=1)
    xp = xp.reshape(batch * tp, d)
    xs = x_sample.reshape(nb, d)
    pos_s = jnp.full((1,), past_len, jnp.int32)
    outs_p = [[] for _ in range(6)]
    outs_s = [[] for _ in range(6)]
    for l in range(depth):
        lp = dict(norm_mix=norm_mix[l], norm_ffn=norm_ffn[l], w_in=_pack_w_in(w_in[l]), fox_b=fox_forget_bias[l],
                  mu=rwkv_mu[l], w0=rwkv_w0[l], w2=rwkv_w2[l], a0=rwkv_a0[l], a2=rwkv_a2[l], g2=rwkv_g2[l],
                  kk=rwkv_kk[l], ka=rwkv_ka[l], rk=rwkv_rk[l], ln_w=rwkv_ln_w[l], ln_b=rwkv_ln_b[l],
                  wb_ret=w_branch_ret[l].astype(BF16), wb_fox=w_branch_fox[l].astype(BF16),
                  wb_rwkv=w_branch_rwkv[l].astype(BF16), w_out=w_out[l].astype(BF16))
        moe = dict(wr=_router_weights(router_group_w[l], router_expert_w[l]), bg=router_group_b[l], be=router_expert_b[l],
                   w1=_expert_weights_bf16(expert_w1, l), w3=_expert_weights_bf16(expert_w3, l),
                   w2=_expert_weights_bf16(expert_w2, l))
        xp, st = _prompt_layer(xp, lp, moe, batch, tp, valid)
        for j in range(6):
            outs_p[j].append(st[j])
        xs, st = _sample_layer(xs, lp, moe, l, cache_k, cache_v, cache_logf, page_table,
                               state_ret[l], state_rwkv[l], state_shift[l], pos_s)
        for j in range(6):
            outs_s[j].append(st[j])
    y_prompt = _final_norm_prompt(xp, norm_final, batch, tp)
    y_sample = _rmsnorm(xs, norm_final, F32).reshape(nb, 1, d)
    return (y_prompt, y_sample, *[jnp.stack(o, axis=0) for o in outs_p], *[jnp.stack(o, axis=0) for o in outs_s])
```

```python
import functools

import numpy as np
import jax
import jax.numpy as jnp
from jax import lax
from jax.experimental import pallas as pl
from jax.experimental.pallas import tpu as pltpu

F32 = jnp.float32
BF16 = jnp.bfloat16
HIGHEST = lax.Precision.HIGHEST

D_MODEL = 2048
N_META = 16
CHUNK = 128
PAD_FRONT = CHUNK - N_META
RMS_EPS = 1e-6
GN_EPS = 1e-5
RET_HEADS = 4
RET_DK = 256
RET_DV = 256
RET_W = RET_HEADS * RET_DK
ROPE_BASE = 10000.0
FOX_HEADS = 8
FOX_HD = 128
FOX_W = FOX_HEADS * FOX_HD
FOX_PAIR = 2
LOG2E = 1.4426950408889634
RWKV_HEADS = 16
RWKV_HD = 64
RWKV_W = RWKV_HEADS * RWKV_HD
RWKV_W_RANK = 64
RWKV_A_RANK = 64
RWKV_G_RANK = 128
RWKV_GN_EPS = 64e-5
RWKV_PROJ = 3 * RWKV_W + RWKV_W_RANK + RWKV_A_RANK + RWKV_G_RANK
RWKV_CHUNK = 64
RWKV_GROUP = 4
N_BRANCH = 3
N_GROUPS = 4
EXPERTS_PER_GROUP = 8
N_EXPERTS = N_GROUPS * EXPERTS_PER_GROUP
TOP_K = 2
EXPERT_FF = 1024
MOE_BLOCK = 128
ROUTER_COLS = 128

FF_PAD = 256
COL_RW = 0
COL_RQ = COL_RW + RWKV_PROJ
COL_RK = COL_RQ + RET_W
COL_RV = COL_RK + RET_W
COL_RG = COL_RV + RET_W
COL_FQ = COL_RG + RET_W
COL_FK = COL_FQ + FOX_W
COL_FV = COL_FK + FOX_W
COL_FF = COL_FV + FOX_W
COL_GATE = COL_FF + FF_PAD
PROJ_PACKED = COL_GATE + N_BRANCH * D_MODEL
SRC_RQ = 0
SRC_FF = 4 * RET_W + 3 * FOX_W
SRC_RW = SRC_FF + FOX_HEADS
SRC_GATE = SRC_RW + RWKV_PROJ

VMEM_LIMIT = 56 * 1024 * 1024
NEG_BIG = -1e30
_NT = (((1,), (1,)), ((), ()))
_TN = (((0,), (0,)), ((), ()))


def _cparams(*sem):
    return pltpu.CompilerParams(dimension_semantics=sem, vmem_limit_bytes=VMEM_LIMIT)


def _pick(n, prefs):
    for p in prefs:
        if n % p == 0:
            return p
    return n


def _rms(x, g):
    return x * lax.rsqrt(jnp.mean(x * x, axis=-1, keepdims=True) + RMS_EPS) * g


def _rms_body(x_ref, g_ref, o_ref):
    o_ref[...] = _rms(x_ref[...], g_ref[...]).astype(o_ref.dtype)


def _rmsnorm(x, g, out_dtype):
    n, d = x.shape
    tm = _pick(n, (256, 128, 8))
    return pl.pallas_call(
        _rms_body,
        grid=(n // tm,),
        in_specs=[pl.BlockSpec((tm, d), lambda i: (i, 0)), pl.BlockSpec((1, d), lambda i: (0, 0))],
        out_specs=pl.BlockSpec((tm, d), lambda i: (i, 0)),
        out_shape=jax.ShapeDtypeStruct((n, d), out_dtype),
        compiler_params=_cparams("parallel"),
        name="rmsnorm",
    )(x, g.reshape(1, d))


def _rms_router_body(x_ref, g_ref, wr_ref, h_ref, lg_ref):
    h = _rms(x_ref[...], g_ref[...])
    h_ref[...] = h
    lg_ref[...] = jnp.dot(h.astype(BF16), wr_ref[...].astype(BF16), preferred_element_type=F32)


def _rmsnorm_router(x, g, wr):
    n, d = x.shape
    tm = _pick(n, (256, 128, 8))
    return pl.pallas_call(
        _rms_router_body,
        grid=(n // tm,),
        in_specs=[pl.BlockSpec((tm, d), lambda i: (i, 0)), pl.BlockSpec((1, d), lambda i: (0, 0)),
                  pl.BlockSpec((d, ROUTER_COLS), lambda i: (0, 0))],
        out_specs=[pl.BlockSpec((tm, d), lambda i: (i, 0)), pl.BlockSpec((tm, ROUTER_COLS), lambda i: (i, 0))],
        out_shape=[jax.ShapeDtypeStruct((n, d), F32), jax.ShapeDtypeStruct((n, ROUTER_COLS), F32)],
        compiler_params=_cparams("parallel"),
        name="rmsnorm_router",
    )(x, g.reshape(1, d), wr)


def _final_norm_prompt(x, g, batch, tp):
    d = x.shape[1]
    nb = tp // CHUNK
    return pl.pallas_call(
        _rms_body,
        grid=(batch, nb - 1),
        in_specs=[pl.BlockSpec((CHUNK, d), lambda b, j: (b * nb + 1 + j, 0)), pl.BlockSpec((1, d), lambda b, j: (0, 0))],
        out_specs=pl.BlockSpec((CHUNK, d), lambda b, j: (b * (nb - 1) + j, 0)),
        out_shape=jax.ShapeDtypeStruct((batch * (tp - CHUNK), d), F32),
        compiler_params=_cparams("parallel", "parallel"),
        name="final_norm",
    )(x, g.reshape(1, d)).reshape(batch, tp - CHUNK, d)


def _mm_body(a_ref, w_ref, o_ref):
    o_ref[...] = jnp.dot(a_ref[...], w_ref[...], preferred_element_type=F32).astype(o_ref.dtype)


def _matmul(a, w, out_dtype=F32, tm_prefs=(1408, 768, 512, 384, 256, 128), tn_prefs=(512, 256, 128)):
    m, k = a.shape
    n = w.shape[1]
    tm = _pick(m, tm_prefs)
    tn = _pick(n, tn_prefs)
    return pl.pallas_call(
        _mm_body,
        grid=(m // tm, n // tn),
        in_specs=[pl.BlockSpec((tm, k), lambda i, j: (i, 0)), pl.BlockSpec((k, tn), lambda i, j: (0, j))],
        out_specs=pl.BlockSpec((tm, tn), lambda i, j: (i, j)),
        out_shape=jax.ShapeDtypeStruct((m, n), out_dtype),
        compiler_params=_cparams("parallel", "arbitrary"),
        name="matmul",
    )(a, w)


def _mm_res_body(a_ref, w_ref, r_ref, o_ref, *, blocks_per_seq, pad):
    y = r_ref[...] + jnp.dot(a_ref[...], w_ref[...], preferred_element_type=F32)
    if pad:
        first = (pl.program_id(0) % blocks_per_seq) == 0
        row = lax.broadcasted_iota(jnp.int32, y.shape, 0)
        y = jnp.where(jnp.logical_and(first, row < pad), 0.0, y)
    o_ref[...] = y


def _matmul_residual(a, w, res, tp=None):
    m, k = a.shape
    n = w.shape[1]
    tm = _pick(tp, (768, 384, 128)) if tp else m
    assert m % tm == 0
    tn = _pick(n, (512, 256, 128))
    body = functools.partial(_mm_res_body, blocks_per_seq=(tp // tm if tp else 1), pad=(PAD_FRONT if tp else 0))
    return pl.pallas_call(
        body,
        grid=(m // tm, n // tn),
        in_specs=[pl.BlockSpec((tm, k), lambda i, j: (i, 0)), pl.BlockSpec((k, tn), lambda i, j: (0, j)),
                  pl.BlockSpec((tm, tn), lambda i, j: (i, j))],
        out_specs=pl.BlockSpec((tm, tn), lambda i, j: (i, j)),
        out_shape=jax.ShapeDtypeStruct((m, n), F32),
        compiler_params=_cparams("parallel", "arbitrary"),
        name="matmul_residual",
    )(a, w, res)


def _rope_halves(x, cos, sin):
    half = x.shape[-1] // 2
    x1, x2 = x[:, :half], x[:, half:]
    return jnp.concatenate([x1 * cos - x2 * sin, x1 * sin + x2 * cos], axis=-1)


def _head_norm(y, eps):
    mu = jnp.mean(y, axis=-1, keepdims=True)
    yc = y - mu
    return yc * lax.rsqrt(jnp.mean(yc * yc, axis=-1, keepdims=True) + eps)


def _silu(x):
    return x / (1.0 + jnp.exp(-x))


def _sigmoid(x):
    return 1.0 / (1.0 + jnp.exp(-x))


def _ret_body(q_ref, k_ref, v_ref, g_ref, cos_ref, sin_ref, dm_ref, cd_ref, kd_ref, sd_ref, o_ref, s_ref):
    c = pl.program_id(2)

    @pl.when(c == 0)
    def _():
        s_ref[...] = jnp.zeros(s_ref.shape, F32)

    cos = cos_ref[...]
    sin = sin_ref[...]
    q = _rope_halves(q_ref[...], cos, sin)
    k = _rope_halves(k_ref[...], cos, sin) * (RET_DK ** -0.5)
    qb = q.astype(BF16)
    kb = k.astype(BF16)
    vb = v_ref[...].astype(BF16)
    s_old = s_ref[0, 0]
    scores = lax.dot_general(qb, kb, (((1,), (1,)), ((), ())), preferred_element_type=F32) * dm_ref[0]
    intra = jnp.dot(scores.astype(BF16), vb, preferred_element_type=F32)
    cross = jnp.dot(qb, s_old.astype(BF16), preferred_element_type=F32) * cd_ref[0]
    kdec = (k * kd_ref[0]).astype(BF16)
    s_ref[0, 0] = sd_ref[0] * s_old + lax.dot_general(kdec, vb, (((0,), (0,)), ((), ())), preferred_element_type=F32)
    o = _head_norm(intra + cross, GN_EPS) * _silu(g_ref[...])
    o_ref[...] = o.astype(o_ref.dtype)


def _ret_tables(length):
    lg = jnp.log1p(-jnp.power(2.0, -5.0 - jnp.arange(RET_HEADS, dtype=F32)))
    i = jnp.arange(length, dtype=F32)
    diff = i[:, None] - i[None, :]
    dmask = jnp.where(diff >= 0, jnp.exp(lg[:, None, None] * jnp.maximum(diff, 0.0)), 0.0)
    cdec = jnp.exp(lg[:, None] * (i + 1.0)[None, :])
    kdec = jnp.exp(lg[:, None] * (length - 1.0 - i)[None, :])
    sdec = jnp.exp(lg * length)
    return lg, dmask, cdec, kdec, sdec


def _rope_tables(pos):
    half = RET_DK // 2
    inv = ROPE_BASE ** (-jnp.arange(half, dtype=F32) / half)
    ang = pos.astype(F32)[:, None] * inv[None, :]
    return jnp.cos(ang), jnp.sin(ang)


def _retention_prompt(proj, batch, tp):
    n = proj.shape[0]
    nc = tp // CHUNK
    _, dmask, cdec, kdec, sdec = _ret_tables(CHUNK)
    cdec = jnp.broadcast_to(cdec[:, :, None], (RET_HEADS, CHUNK, RET_DV))
    kdec = jnp.broadcast_to(kdec[:, :, None], (RET_HEADS, CHUNK, RET_DK))
    sdec = jnp.broadcast_to(sdec[:, None, None], (RET_HEADS, 1, RET_DV))
    cos, sin = _rope_tables(jnp.arange(tp, dtype=jnp.int32) - PAD_FRONT)
    blk = lambda col: pl.BlockSpec((CHUNK, RET_DK), lambda b, h, c, col=col: (b * nc + c, col // RET_DK + h))
    tab = lambda shape: pl.BlockSpec((1,) + shape, lambda b, h, c: (h, 0, 0))
    rope = pl.BlockSpec((CHUNK, RET_DK // 2), lambda b, h, c: (c, 0))
    return pl.pallas_call(
        _ret_body,
        grid=(batch, RET_HEADS, nc),
        in_specs=[blk(COL_RQ), blk(COL_RK), blk(COL_RV), blk(COL_RG), rope, rope,
                  tab((CHUNK, CHUNK)), tab((CHUNK, RET_DV)), tab((CHUNK, RET_DK)), tab((1, RET_DV))],
        out_specs=[pl.BlockSpec((CHUNK, RET_DV), lambda b, h, c: (b * nc + c, h)),
                   pl.BlockSpec((1, 1, RET_DK, RET_DV), lambda b, h, c: (b, h, 0, 0))],
        out_shape=[jax.ShapeDtypeStruct((n, RET_W), BF16), jax.ShapeDtypeStruct((batch, RET_HEADS, RET_DK, RET_DV), F32)],
        compiler_params=_cparams("parallel", "parallel", "arbitrary"),
        name="retention_prompt",
    )(proj, proj, proj, proj, cos, sin, dmask, cdec, kdec, sdec)


def _fox_body(qi_ref, ki_ref, q_ref, k_ref, v_ref, qx_ref, kx_ref, o_ref, m_sc, l_sc, acc_sc, *, tq, tk, hp):
    step = pl.program_id(2)
    qi = qi_ref[step]
    ki = ki_ref[step]

    @pl.when(ki == 0)
    def _():
        m_sc[...] = jnp.full(m_sc.shape, NEG_BIG, F32)
        l_sc[...] = jnp.zeros(l_sc.shape, F32)
        acc_sc[...] = jnp.zeros(acc_sc.shape, F32)

    def update(masked):
        if masked:
            qpos = qi * tq + lax.broadcasted_iota(jnp.int32, (tq, tk), 0)
            kpos = ki * tk + lax.broadcasted_iota(jnp.int32, (tq, tk), 1)
            valid = jnp.logical_and(kpos <= qpos, kpos >= PAD_FRONT)
        m_old, l_old, acc_old = m_sc[...], l_sc[...], acc_sc[...]
        m_out, l_out, acc_out = [], [], []
        for h in range(hp):
            sl = slice(h * FOX_HD, (h + 1) * FOX_HD)
            qa = jnp.concatenate([q_ref[:, sl].astype(BF16), qx_ref[0, h]], axis=1)
            ka = jnp.concatenate([k_ref[:, sl].astype(BF16), kx_ref[0, h]], axis=1)
            s = lax.dot_general(qa, ka, _NT, preferred_element_type=F32) * (FOX_HD ** -0.5 * LOG2E)
            if masked:
                s = jnp.where(valid, s, NEG_BIG)
            m_new = jnp.maximum(m_old[h], jnp.max(s, axis=1, keepdims=True))
            alpha = jnp.exp2(m_old[h] - m_new)
            p = jnp.exp2(s - m_new)
            m_out.append(m_new)
            l_out.append(alpha * l_old[h] + jnp.sum(p, axis=1, keepdims=True))
            acc_out.append(alpha * acc_old[:, sl] + jnp.dot(p.astype(BF16), v_ref[:, sl].astype(BF16),
                                                            preferred_element_type=F32))
        for h in range(hp):
            m_sc[h] = m_out[h]
            l_sc[h] = l_out[h]
            acc_sc[:, h * FOX_HD:(h + 1) * FOX_HD] = acc_out[h]

    edge = jnp.logical_or(ki == qi, ki == 0)
    pl.when(edge)(functools.partial(update, True))
    pl.when(jnp.logical_not(edge))(functools.partial(update, False))

    @pl.when(ki == qi)
    def _():
        for h in range(hp):
            sl = slice(h * FOX_HD, (h + 1) * FOX_HD)
            o_ref[:, sl] = (acc_sc[:, sl] / l_sc[h]).astype(o_ref.dtype)


def _split3(x):
    hi = x.astype(BF16)
    r1 = x - hi.astype(F32)
    mid = r1.astype(BF16)
    return hi, mid, (r1 - mid.astype(F32)).astype(BF16)


def _fox_prompt(proj, c, batch, tp):
    n = proj.shape[0]
    tq = _pick(tp, (384, 256, 128))
    nq = tp // tq
    hp = FOX_PAIR
    pairs = [(i, j) for i in range(nq) for j in range(i + 1)]
    qi_tab = jnp.asarray(np.array([p[0] for p in pairs], np.int32))
    ki_tab = jnp.asarray(np.array([p[1] for p in pairs], np.int32))
    hi, mid, lo = _split3(c * (FOX_HD ** 0.5))
    one = jnp.ones_like(hi)
    fill = jnp.zeros(c.shape + (FOX_HD - 6,), BF16)
    qx = jnp.concatenate([jnp.stack([hi, mid, lo, one, one, one], axis=-1), fill], axis=-1)
    kx = jnp.concatenate([jnp.stack([one, one, one, -hi, -mid, -lo], axis=-1), fill], axis=-1)
    cb = lambda col: col // (hp * FOX_HD)
    wide = hp * FOX_HD
    grid_spec = pltpu.PrefetchScalarGridSpec(
        num_scalar_prefetch=2,
        grid=(batch, FOX_HEADS // hp, len(pairs)),
        in_specs=[
            pl.BlockSpec((tq, wide), lambda b, h, s, qi, ki: (b * nq + qi[s], cb(COL_FQ) + h)),
            pl.BlockSpec((tq, wide), lambda b, h, s, qi, ki: (b * nq + ki[s], cb(COL_FK) + h)),
            pl.BlockSpec((tq, wide), lambda b, h, s, qi, ki: (b * nq + ki[s], cb(COL_FV) + h)),
            pl.BlockSpec((1, hp, tq, FOX_HD), lambda b, h, s, qi, ki: (b, h, qi[s], 0)),
            pl.BlockSpec((1, hp, tq, FOX_HD), lambda b, h, s, qi, ki: (b, h, ki[s], 0)),
        ],
        out_specs=pl.BlockSpec((tq, wide), lambda b, h, s, qi, ki: (b * nq + qi[s], h)),
        scratch_shapes=[pltpu.VMEM((hp, tq, 1), F32), pltpu.VMEM((hp, tq, 1), F32), pltpu.VMEM((tq, wide), F32)],
    )
    return pl.pallas_call(
        functools.partial(_fox_body, tq=tq, tk=tq, hp=hp),
        grid_spec=grid_spec,
        out_shape=jax.ShapeDtypeStruct((n, FOX_W), BF16),
        compiler_params=_cparams("parallel", "parallel", "arbitrary"),
        name="fox_prompt",
    )(qi_tab, ki_tab, proj, proj, proj, qx, kx)


def _softplus(z):
    return jnp.maximum(z, 0.0) + jnp.log1p(jnp.exp(-jnp.abs(z)))


def _rwkv_prep_math(c, prev, mu, w0, w2, a0, a2, g2, kkp, ka, exact):
    w = RWKV_W
    xm = c + mu * (prev - c)
    r, k, v = xm[:, 0:w], xm[:, w:2 * w], xm[:, 2 * w:3 * w]
    wd = xm[:, 3 * w:3 * w + RWKV_W_RANK]
    ad = xm[:, 3 * w + RWKV_W_RANK:3 * w + RWKV_W_RANK + RWKV_A_RANK]
    gd = xm[:, 3 * w + RWKV_W_RANK + RWKV_A_RANK:]
    if exact:
        mm = lambda x, m: jnp.dot(x, m, precision=HIGHEST, preferred_element_type=F32)
    else:
        mm = lambda x, m: jnp.dot(x.astype(BF16), m.astype(BF16), preferred_element_type=F32)
    w_log = -_softplus(-(w0 + mm(jnp.tanh(wd), w2))) - 0.5
    lw = -jnp.exp(w_log)
    a = _sigmoid(a0 + mm(ad, a2))
    g = mm(_sigmoid(gd), g2)
    kk0 = k * kkp
    kmod = k * (1.0 + (a - 1.0) * ka)
    return r, kmod, v, lw, kk0, a, g


def _rwkv_prep_body(c_ref, mu_ref, w0_ref, w2_ref, a0_ref, a2_ref, g2_ref, kkp_ref, ka_ref,
                    r_ref, k_ref, v_ref, lw_ref, kk_ref, a_ref, g_ref, carry):
    t = pl.program_id(1)

    @pl.when(t == 0)
    def _():
        carry[...] = jnp.zeros(carry.shape, F32)

    c = c_ref[...]
    rows = c.shape[0]
    prev = pltpu.roll(c, 1, axis=0)
    row = lax.broadcasted_iota(jnp.int32, c.shape, 0)
    prev = jnp.where(row == 0, carry[...], prev)
    carry[...] = c[rows - 1:rows, :]
    outs = _rwkv_prep_math(c, prev, mu_ref[...], w0_ref[...], w2_ref[...], a0_ref[...], a2_ref[...], g2_ref[...],
                           kkp_ref[...], ka_ref[...], exact=False)
    for ref, val in zip((r_ref, k_ref, v_ref, lw_ref, kk_ref, a_ref, g_ref), outs):
        ref[...] = val


def _rwkv_prep_prompt(proj, lp, batch, tp):
    n = proj.shape[0]
    tb = CHUNK
    nt = tp // tb
    row = lambda x: x.reshape(1, -1)
    full = lambda shape: pl.BlockSpec(shape, lambda b, t: (0, 0))
    out_spec = pl.BlockSpec((tb, RWKV_W), lambda b, t: (b * nt + t, 0))
    return pl.pallas_call(
        _rwkv_prep_body,
        grid=(batch, nt),
        in_specs=[pl.BlockSpec((tb, RWKV_PROJ), lambda b, t: (b * nt + t, COL_RW // RWKV_PROJ)),
                  full((1, RWKV_PROJ)), full((1, RWKV_W)), full((RWKV_W_RANK, RWKV_W)), full((1, RWKV_W)),
                  full((RWKV_A_RANK, RWKV_W)), full((RWKV_G_RANK, RWKV_W)), full((1, RWKV_W)), full((1, RWKV_W))],
        out_specs=[out_spec] * 7,
        out_shape=[jax.ShapeDtypeStruct((n, RWKV_W), F32)] * 7,
        scratch_shapes=[pltpu.VMEM((1, RWKV_PROJ), F32)],
        compiler_params=_cparams("parallel", "arbitrary"),
        name="rwkv_prep",
    )(proj, row(lp['mu']), row(lp['w0']), lp['w2'], row(lp['a0']), lp['a2'], lp['g2'], row(lp['kk']), row(lp['ka']))


def _rwkv_chunk_body(r_ref, k_ref, v_ref, lw_ref, kk_ref, a_ref, g_ref, rk_ref, lnw_ref, lnb_ref, o_ref, s_ref):
    cn = pl.program_id(1)

    @pl.when(cn == 0)
    def _():
        s_ref[...] = jnp.zeros(s_ref.shape, F32)

    cs = RWKV_CHUNK
    hd = RWKV_HD
    gh = RWKV_GROUP
    n = gh * cs
    gw = gh * hd
    bits = int(np.log2(cs))
    row = lax.broadcasted_iota(jnp.int32, (n, n), 0)
    col = lax.broadcasted_iota(jnp.int32, (n, n), 1)
    same_head = (row >> bits) == (col >> bits)
    strict = same_head & (col < row)
    incl = same_head & (col <= row)
    eye = (row == col).astype(F32)
    pair_masks = [((row >> (bit + 1)) == (col >> (bit + 1))) & ((row & (1 << bit)) != 0) & ((col & (1 << bit)) == 0)
                  for bit in range(bits)]
    trow = lax.broadcasted_iota(jnp.int32, (cs, cs), 0)
    tcol = lax.broadcasted_iota(jnp.int32, (cs, cs), 1)
    cum_all = jnp.dot((tcol <= trow).astype(F32), lw_ref[...], precision=HIGHEST, preferred_element_type=F32)
    decay_all = jnp.exp(cum_all[cs - 1:cs, :])
    tail_all = jnp.exp(cum_all[cs - 1:cs, :] - cum_all)
    dot = lambda x, y: jnp.dot(x.astype(BF16), y.astype(BF16), preferred_element_type=F32)
    dot_nt = lambda x, y: lax.dot_general(x.astype(BF16), y.astype(BF16), _NT, preferred_element_type=F32)
    dot_tn = lambda x, y: lax.dot_general(x.astype(BF16), y.astype(BF16), _TN, preferred_element_type=F32)
    for g in range(RWKV_HEADS // gh):
        lanes = slice(g * gw, (g + 1) * gw)
        stack = lambda x: jnp.concatenate([x[:, g * gw + h * hd:g * gw + (h + 1) * hd] for h in range(gh)], axis=0)
        rows = lambda x: jnp.concatenate([jnp.broadcast_to(x[:, g * gw + h * hd:g * gw + (h + 1) * hd], (cs, hd))
                                          for h in range(gh)], axis=0)
        lw, cum, tail = stack(lw_ref[...]), stack(cum_all), stack(tail_all)
        kk0 = stack(kk_ref[...])
        kk = kk0 * lax.rsqrt(jnp.sum(kk0 * kk0, axis=-1, keepdims=True) + 1e-12)
        b = kk * stack(a_ref[...])
        r, k, v = stack(r_ref[...]), stack(k_ref[...]), stack(v_ref[...])
        e_in = jnp.exp(cum)
        e_neg = jnp.exp(-cum)
        ar = jnp.concatenate([-kk * jnp.exp(cum - lw), r * e_in], axis=0)
        bk = jnp.concatenate([b * e_neg, k * e_neg], axis=0)
        gram = dot_nt(ar, bk)
        l_ab = jnp.where(strict, gram[:n, :n], 0.0)
        l_ak = jnp.where(strict, gram[:n, n:], 0.0)
        m_rbk = jnp.concatenate([jnp.where(incl, gram[n:, :n], 0.0), jnp.where(incl, gram[n:, n:], 0.0)], axis=1)
        tinv = eye + jnp.where(pair_masks[0], l_ab, 0.0)
        for mask in pair_masks[1:]:
            tinv = tinv + dot(dot(tinv, jnp.where(mask, l_ab, 0.0)), tinv)
        s_old = s_ref[0, g * n:(g + 1) * n, :]
        ars = dot_nt(ar, s_old)
        v_bd = jnp.where(same_head, jnp.concatenate([v_ref[:, lanes]] * gh, axis=0), 0.0)
        u = dot(tinv, jnp.where(same_head, ars[:n], 0.0) + dot(l_ak, v_bd))
        uv = jnp.concatenate([u, v_bd], axis=0)
        y_bd = jnp.where(same_head, ars[n:], 0.0) + dot(m_rbk, uv)
        s_ref[0, g * n:(g + 1) * n, :] = s_old * rows(decay_all) + dot_tn(uv, jnp.concatenate([b * tail, k * tail], axis=0))
        y = sum(y_bd[:, h * hd:(h + 1) * hd] for h in range(gh))
        yn = _head_norm(y, RWKV_GN_EPS) * rows(lnw_ref[...]) + rows(lnb_ref[...])
        bonus = jnp.sum(r * k * rows(rk_ref[...]), axis=-1, keepdims=True) * v
        out = (yn + bonus) * stack(g_ref[...])
        o_ref[:, lanes] = jnp.concatenate([out[h * cs:(h + 1) * cs] for h in range(gh)], axis=1).astype(o_ref.dtype)


def _rwkv_chunk_prompt(prep, lp, batch, tp):
    n = prep[0].shape[0]
    cs = RWKV_CHUNK
    ncn = tp // cs
    row = lambda x: x.reshape(1, -1)
    blk = pl.BlockSpec((cs, RWKV_W), lambda b, c: (b * ncn + c, 0))
    full = pl.BlockSpec((1, RWKV_W), lambda b, c: (0, 0))
    out, state = pl.pallas_call(
        _rwkv_chunk_body,
        grid=(batch, ncn),
        in_specs=[blk] * 7 + [full] * 3,
        out_specs=[blk, pl.BlockSpec((1, RWKV_W, RWKV_HD), lambda b, c: (b, 0, 0))],
        out_shape=[jax.ShapeDtypeStruct((n, RWKV_W), BF16), jax.ShapeDtypeStruct((batch, RWKV_W, RWKV_HD), F32)],
        compiler_params=_cparams("parallel", "arbitrary"),
        name="rwkv_chunk",
    )(*prep, row(lp['rk']), row(lp['ln_w']), row(lp['ln_b']))
    return out, state.reshape(batch, RWKV_HEADS, RWKV_HD, RWKV_HD)


def _merge_body(ro_ref, fo_ref, wo_ref, wr_ref, wf_ref, ww_ref, g0_ref, g1_ref, g2_ref, o_ref):
    dot = functools.partial(jnp.dot, preferred_element_type=F32)
    m = (_sigmoid(g0_ref[...]) * dot(ro_ref[...], wr_ref[...])
         + _sigmoid(g1_ref[...]) * dot(fo_ref[...], wf_ref[...])
         + _sigmoid(g2_ref[...]) * dot(wo_ref[...], ww_ref[...]))
    o_ref[...] = m.astype(o_ref.dtype)


def _merge(ret_o, fox_o, rw_o, wb_ret, wb_fox, wb_rwkv, gates, gate_col, out_dtype):
    m = ret_o.shape[0]
    tm = _pick(m, (768, 384, 128))
    tn = 512
    nj = D_MODEL // tn
    act = lambda width: pl.BlockSpec((tm, width), lambda i, j: (i, 0))
    wgt = lambda width: pl.BlockSpec((width, tn), lambda i, j: (0, j))
    gate = lambda br: pl.BlockSpec((tm, tn), lambda i, j, br=br: (i, gate_col // tn + br * nj + j))
    return pl.pallas_call(
        _merge_body,
        grid=(m // tm, nj),
        in_specs=[act(RET_W), act(FOX_W), act(RWKV_W), wgt(RET_W), wgt(FOX_W), wgt(RWKV_W), gate(0), gate(1), gate(2)],
        out_specs=pl.BlockSpec((tm, tn), lambda i, j: (i, j)),
        out_shape=jax.ShapeDtypeStruct((m, D_MODEL), out_dtype),
        compiler_params=_cparams("parallel", "arbitrary"),
        name="merge",
    )(ret_o, fox_o, rw_o, wb_ret, wb_fox, wb_rwkv, gates, gates, gates)


def _cast_body(x_ref, o_ref):
    o_ref[...] = x_ref[...].astype(o_ref.dtype)


def _expert_weights_bf16(w, layer):
    _, e, r, c = w.shape
    return pl.pallas_call(
        _cast_body,
        grid=(e,),
        in_specs=[pl.BlockSpec((None, 1, r, c), lambda i: (layer, i, 0, 0))],
        out_specs=pl.BlockSpec((1, r, c), lambda i: (i, 0, 0)),
        out_shape=jax.ShapeDtypeStruct((e, r, c), BF16),
        compiler_params=_cparams("parallel"),
        name="expert_weights_bf16",
    )(w)


def _route(logits, bg, be):
    n = logits.shape[0]
    gp = jax.nn.softmax(logits[:, :N_GROUPS] + bg.astype(F32), axis=-1)
    gidx = jnp.argmax(gp, axis=-1)
    pg = jnp.take_along_axis(gp, gidx[:, None], axis=-1)
    el = logits[:, N_GROUPS:N_GROUPS + N_EXPERTS].reshape(n, N_GROUPS, EXPERTS_PER_GROUP) + be.astype(F32)[None]
    el = jnp.take_along_axis(el, gidx[:, None, None], axis=1)[:, 0]
    topv, topi = lax.top_k(jax.nn.softmax(el, axis=-1), TOP_K)
    gate = pg * topv / jnp.sum(topv, axis=-1, keepdims=True)
    eid = (gidx[:, None] * EXPERTS_PER_GROUP + topi).astype(jnp.int32)
    return eid, gate


def _router_weights(wg, we):
    d = wg.shape[0]
    wr = jnp.concatenate([wg, jnp.transpose(we, (1, 0, 2)).reshape(d, N_EXPERTS)], axis=1)
    return jnp.pad(wr, ((0, 0), (0, ROUTER_COLS - wr.shape[1])))


def _moe_body(be_ref, nused_ref, tok_ref, dst_ref, h_hbm, w1_ref, w3_ref, w2_ref, y_in, y_hbm, xbuf, ybuf, sem_in, sem_out):
    del y_in
    i = pl.program_id(0)
    n_used = nused_ref[0]
    slot = i % 2
    rows = range(MOE_BLOCK)

    def gather(blk, buf, r):
        return pltpu.make_async_copy(h_hbm.at[pl.ds(tok_ref[blk * MOE_BLOCK + r], 1)], xbuf.at[buf, pl.ds(r, 1)],
                                     sem_in.at[buf])

    def scatter(blk, r):
        return pltpu.make_async_copy(ybuf.at[pl.ds(r, 1)], y_hbm.at[pl.ds(dst_ref[blk * MOE_BLOCK + r], 1)], sem_out)

    @pl.when(jnp.logical_and(i == 0, n_used > 0))
    def _():
        for r in rows:
            gather(0, 0, r).start()

    @pl.when(i < n_used)
    def _():
        for r in rows:
            gather(i, slot, r).wait()
        nxt = jnp.minimum(i + 1, n_used - 1)
        for r in rows:
            gather(nxt, 1 - slot, r).start()
        x = xbuf[slot].astype(BF16)
        a = jnp.dot(x, w1_ref[0], preferred_element_type=F32)
        b = jnp.dot(x, w3_ref[0], preferred_element_type=F32)
        y = jnp.dot((_silu(a) * b).astype(BF16), w2_ref[0], preferred_element_type=F32)

        @pl.when(i > 0)
        def _():
            for r in rows:
                scatter(i - 1, r).wait()

        ybuf[...] = y
        for r in rows:
            scatter(i, r).start(priority=r % 2)

        @pl.when(i == n_used - 1)
        def _():
            for r in rows:
                gather(nxt, 1 - slot, r).wait()
                scatter(i, r).wait()


def _moe_prompt(h, eid, gate, valid, w1, w3, w2):
    n, d = h.shape
    a_tot = n * TOP_K
    n_real = int(np.sum(valid)) * TOP_K
    n_blk = (n_real + N_EXPERTS * (MOE_BLOCK - 1) + MOE_BLOCK - 1) // MOE_BLOCK
    cap = n_blk * MOE_BLOCK
    validf = jnp.repeat(jnp.asarray(valid), TOP_K)
    eflat = eid.reshape(-1)
    onehot = jnp.logical_and(eflat[:, None] == jnp.arange(N_EXPERTS, dtype=jnp.int32)[None, :], validf[:, None]).astype(jnp.int32)
    blocks = onehot.astype(F32).reshape(a_tot // CHUNK, CHUNK, N_EXPERTS)
    below = jnp.tril(jnp.ones((CHUNK, CHUNK), F32), -1)
    inner = jnp.einsum('ij,bjk->bik', below, blocks)
    totals = jnp.sum(blocks, axis=1)
    offset = jnp.cumsum(totals, axis=0) - totals
    before = (inner + offset[:, None, :]).reshape(a_tot, N_EXPERTS).astype(jnp.int32)
    rank = jnp.sum(before * onehot, axis=1)
    counts = jnp.sum(onehot, axis=0)
    padded = (counts + MOE_BLOCK - 1) // MOE_BLOCK * MOE_BLOCK
    pad_end = jnp.cumsum(padded)
    pad_start = pad_end - padded
    dest = jnp.where(validf, pad_start[eflat] + rank, cap)
    assign = jnp.arange(a_tot, dtype=jnp.int32)
    zero_row = int(np.argmin(valid))
    held = jnp.full((cap,), -1, jnp.int32).at[dest].set(assign, mode='drop')
    tok_buf = jnp.where(held >= 0, held // TOP_K, zero_row)
    dump = a_tot + (jnp.arange(cap, dtype=jnp.int32) % MOE_BLOCK)
    dst_buf = jnp.where(held >= 0, (held % TOP_K) * n + held // TOP_K, dump)
    n_used = (pad_end[-1] // MOE_BLOCK).astype(jnp.int32)
    blk = jnp.arange(n_blk, dtype=jnp.int32)
    blk_e = jnp.sum((pad_end[None, :] <= (blk * MOE_BLOCK)[:, None]).astype(jnp.int32), axis=1)
    blk_e = jnp.minimum(blk_e, N_EXPERTS - 1)
    blk_e = jnp.where(blk < n_used, blk_e, blk_e[jnp.maximum(n_used - 1, 0)])
    y_rows = a_tot + MOE_BLOCK
    y0 = jnp.zeros((y_rows, d), F32)
    ff = w1.shape[2]
    grid_spec = pltpu.PrefetchScalarGridSpec(
        num_scalar_prefetch=4,
        grid=(n_blk,),
        in_specs=[pl.BlockSpec(memory_space=pl.ANY),
                  pl.BlockSpec((1, d, ff), lambda i, be, nu, tk, ds: (be[i], 0, 0)),
                  pl.BlockSpec((1, d, ff), lambda i, be, nu, tk, ds: (be[i], 0, 0)),
                  pl.BlockSpec((1, ff, d), lambda i, be, nu, tk, ds: (be[i], 0, 0)),
                  pl.BlockSpec(memory_space=pl.ANY)],
        out_specs=pl.BlockSpec(memory_space=pl.ANY),
        scratch_shapes=[pltpu.VMEM((2, MOE_BLOCK, d), F32), pltpu.VMEM((MOE_BLOCK, d), F32),
                        pltpu.SemaphoreType.DMA((2,)), pltpu.SemaphoreType.DMA(())],
    )
    y2 = pl.pallas_call(
        _moe_body,
        grid_spec=grid_spec,
        out_shape=jax.ShapeDtypeStruct((y_rows, d), F32),
        input_output_aliases={8: 0},
        compiler_params=_cparams("arbitrary"),
        name="moe_experts",
    )(blk_e, n_used.reshape(1), tok_buf, dst_buf, h, w1, w3, w2, y0)
    return y2


def _combine_body(x_ref, y0_ref, y1_ref, g_ref, o_ref):
    g = g_ref[...]
    o_ref[...] = x_ref[...] + (y0_ref[...] * g[:, 0:1] + y1_ref[...] * g[:, 1:2])


def _moe_combine(x, y2, gate):
    n, d = x.shape
    tm = _pick(n, (256, 128, 8))
    gpad = jnp.pad(gate, ((0, 0), (0, 128 - TOP_K)))
    return pl.pallas_call(
        _combine_body,
        grid=(n // tm,),
        in_specs=[pl.BlockSpec((tm, d), lambda i: (i, 0)), pl.BlockSpec((tm, d), lambda i: (i, 0)),
                  pl.BlockSpec((tm, d), lambda i: (n // tm + i, 0)), pl.BlockSpec((tm, 128), lambda i: (i, 0))],
        out_specs=pl.BlockSpec((tm, d), lambda i: (i, 0)),
        out_shape=jax.ShapeDtypeStruct((n, d), F32),
        compiler_params=_cparams("parallel"),
        name="moe_combine",
    )(x, y2, y2, gpad)


def _pack_w_in(w_in):
    ff = jnp.pad(w_in[:, SRC_FF:SRC_FF + FOX_HEADS], ((0, 0), (0, FF_PAD - FOX_HEADS)))
    return jnp.concatenate([w_in[:, SRC_RW:SRC_RW + RWKV_PROJ], w_in[:, :SRC_FF], ff, w_in[:, SRC_GATE:]],
                           axis=1).astype(BF16)


def _prompt_layer(x, lp, moe, batch, tp, valid):
    h = _rmsnorm(x, lp['norm_mix'], BF16)
    proj = _matmul(h, lp['w_in'], tn_prefs=(1536, 1024, 512, 256, 128))
    ret_o, ret_s = _retention_prompt(proj, batch, tp)
    ff = proj[:, COL_FF:COL_FF + FOX_HEADS].reshape(batch, tp, FOX_HEADS)
    logf = jax.nn.log_sigmoid(ff + lp['fox_b'].astype(F32))
    c = jnp.cumsum(logf, axis=1).transpose(0, 2, 1)
    fox_o = _fox_prompt(proj, c, batch, tp)
    prep = _rwkv_prep_prompt(proj, lp, batch, tp)
    rw_o, rw_s = _rwkv_chunk_prompt(prep, lp, batch, tp)
    merged = _merge(ret_o, fox_o, rw_o, lp['wb_ret'], lp['wb_fox'], lp['wb_rwkv'], proj, COL_GATE, BF16)
    x = _matmul_residual(merged, lp['w_out'], x, tp=tp)
    h2, logits = _rmsnorm_router(x, lp['norm_ffn'], moe['wr'])
    eid, gate = _route(logits, moe['bg'], moe['be'])
    y2 = _moe_prompt(h2, eid, gate, valid, moe['w1'], moe['w3'], moe['w2'])
    x = _moe_combine(x, y2, gate)
    p3 = proj.reshape(batch, tp, PROJ_PACKED)
    heads = lambda col: p3[:, PAD_FRONT:, col:col + FOX_W].reshape(batch, tp - PAD_FRONT, FOX_HEADS, FOX_HD)
    state = (heads(COL_FK), heads(COL_FV), logf[:, PAD_FRONT:], ret_s, rw_s, p3[:, tp - 1, COL_RW:COL_RW + RWKV_PROJ])
    return x, state


def _rows16(x):
    return jnp.concatenate([x.astype(BF16), jnp.zeros(x.shape, BF16)], axis=0)


def _mm_sample(x, w):
    return _matmul(_rows16(x), w, tn_prefs=(1536, 1024, 512, 256, 128))[:x.shape[0]]


def _pad8(x):
    first = lax.broadcasted_iota(jnp.int32, (8, x.shape[1]), 0) == 0
    return jnp.where(first, jnp.broadcast_to(x, (8, x.shape[1])), 0.0)


def _ret_sample_body(q_ref, k_ref, v_ref, g_ref, cos_ref, sin_ref, dec_ref, s0_ref, o_ref, s_ref):
    b = pl.program_id(1)
    row1 = lambda ref: ref[pl.ds(b, 1), :]
    cos = cos_ref[...]
    sin = sin_ref[...]
    q = _rope_halves(row1(q_ref), cos, sin)
    k = _rope_halves(row1(k_ref), cos, sin) * (RET_DK ** -0.5)
    v = row1(v_ref)
    dec = dec_ref[0]
    s0 = s0_ref[0, 0]
    rnd = lambda x: x.astype(BF16).astype(F32)
    cross = jnp.dot(_pad8(q).astype(BF16), s0.astype(BF16), preferred_element_type=F32)[0:1] * dec
    intra = jnp.sum(rnd(q) * rnd(k), axis=-1, keepdims=True) * v
    s_ref[0, 0] = dec * s0 + lax.dot_general(_pad8(k), _pad8(v), _TN, precision=HIGHEST, preferred_element_type=F32)
    o_ref[pl.ds(b, 1), :] = _head_norm(intra + cross, GN_EPS) * _silu(row1(g_ref))


def _retention_sample(proj, s0, pos):
    nb = proj.shape[0]
    lg = _ret_tables(1)[0]
    dec = jnp.broadcast_to(jnp.exp(lg)[:, None, None], (RET_HEADS, 1, RET_DV))
    cos, sin = _rope_tables(pos)
    blk = lambda col: pl.BlockSpec((nb, RET_DK), lambda h, b, col=col: (0, col // RET_DK + h))
    rope = pl.BlockSpec((1, RET_DK // 2), lambda h, b: (0, 0))
    st = pl.BlockSpec((1, 1, RET_DK, RET_DV), lambda h, b: (b, h, 0, 0))
    return pl.pallas_call(
        _ret_sample_body,
        grid=(RET_HEADS, nb),
        in_specs=[blk(COL_RQ), blk(COL_RK), blk(COL_RV), blk(COL_RG), rope, rope,
                  pl.BlockSpec((1, 1, RET_DV), lambda h, b: (h, 0, 0)), st],
        out_specs=[pl.BlockSpec((nb, RET_DV), lambda h, b: (0, h)), st],
        out_shape=[jax.ShapeDtypeStruct((nb, RET_W), F32), jax.ShapeDtypeStruct(s0.shape, F32)],
        compiler_params=_cparams("parallel", "arbitrary"),
        name="retention_sample",
    )(proj, proj, proj, proj, cos, sin, dec, s0)


def _rwkv_prep_sample_body(c_ref, prev_ref, mu_ref, w0_ref, w2_ref, a0_ref, a2_ref, g2_ref, kkp_ref, ka_ref, *out_refs):
    outs = _rwkv_prep_math(c_ref[...], prev_ref[...], mu_ref[...], w0_ref[...], w2_ref[...], a0_ref[...], a2_ref[...],
                           g2_ref[...], kkp_ref[...], ka_ref[...], exact=False)
    for ref, val in zip(out_refs, outs):
        ref[...] = val


def _rwkv_step_body(r_ref, k_ref, v_ref, lw_ref, kk_ref, a_ref, g_ref, rk_ref, lnw_ref, lnb_ref, s0_ref, o_ref, s_ref):
    b = pl.program_id(0)
    dg = functools.partial(lax.dot_general, precision=HIGHEST, preferred_element_type=F32)
    r_all, k_all, v_all, lw_all, kk_all, a_all, g_all = (
        ref[pl.ds(b, 1), :] for ref in (r_ref, k_ref, v_ref, lw_ref, kk_ref, a_ref, g_ref))
    rk_all, lnw_all, lnb_all = rk_ref[...], lnw_ref[...], lnb_ref[...]
    outs = []
    for h in range(RWKV_HEADS):
        sl = slice(h * RWKV_HD, (h + 1) * RWKV_HD)
        kk0 = kk_all[:, sl]
        kk = kk0 * lax.rsqrt(jnp.sum(kk0 * kk0, axis=-1, keepdims=True) + 1e-12)
        bb = kk * a_all[:, sl]
        w = jnp.exp(lw_all[:, sl])
        r, k, v = r_all[:, sl], k_all[:, sl], v_all[:, sl]
        s0 = s0_ref[0, h]
        sa = lax.dot_general(s0.astype(BF16), _pad8(-kk).astype(BF16), _NT, preferred_element_type=F32)[:, 0:1]
        s_new = s0 * w + sa * bb + dg(_pad8(v), _pad8(k), _TN)
        s_ref[0, h] = s_new
        y = lax.dot_general(_pad8(r).astype(BF16), s_new.astype(BF16), _NT, preferred_element_type=F32)[0:1]
        yn = _head_norm(y, RWKV_GN_EPS) * lnw_all[:, sl] + lnb_all[:, sl]
        bonus = jnp.sum(r * k * rk_all[:, sl], axis=-1, keepdims=True) * v
        outs.append((yn + bonus) * g_all[:, sl])
    o_ref[pl.ds(b, 1), :] = jnp.concatenate(outs, axis=1)


def _rwkv_sample(proj, lp, s0, shift0):
    nb = proj.shape[0]
    row = lambda x: x.reshape(1, -1)
    prep = pl.pallas_call(
        _rwkv_prep_sample_body,
        grid=(1,),
        in_specs=[pl.BlockSpec((nb, RWKV_PROJ), lambda i: (0, COL_RW // RWKV_PROJ)), pl.BlockSpec((nb, RWKV_PROJ), lambda i: (0, 0)),
                  pl.BlockSpec((1, RWKV_PROJ), lambda i: (0, 0)), pl.BlockSpec((1, RWKV_W), lambda i: (0, 0)),
                  pl.BlockSpec((RWKV_W_RANK, RWKV_W), lambda i: (0, 0)), pl.BlockSpec((1, RWKV_W), lambda i: (0, 0)),
                  pl.BlockSpec((RWKV_A_RANK, RWKV_W), lambda i: (0, 0)), pl.BlockSpec((RWKV_G_RANK, RWKV_W), lambda i: (0, 0)),
                  pl.BlockSpec((1, RWKV_W), lambda i: (0, 0)), pl.BlockSpec((1, RWKV_W), lambda i: (0, 0))],
        out_specs=[pl.BlockSpec((nb, RWKV_W), lambda i: (0, 0))] * 7,
        out_shape=[jax.ShapeDtypeStruct((nb, RWKV_W), F32)] * 7,
        compiler_params=_cparams("arbitrary"),
        name="rwkv_prep_sample",
    )(proj, shift0, row(lp['mu']), row(lp['w0']), lp['w2'], row(lp['a0']), lp['a2'], lp['g2'], row(lp['kk']), row(lp['ka']))
    act = pl.BlockSpec((nb, RWKV_W), lambda b: (0, 0))
    par = pl.BlockSpec((1, RWKV_W), lambda b: (0, 0))
    st = pl.BlockSpec((1, RWKV_HEADS, RWKV_HD, RWKV_HD), lambda b: (b, 0, 0, 0))
    return pl.pallas_call(
        _rwkv_step_body,
        grid=(nb,),
        in_specs=[act] * 7 + [par] * 3 + [st],
        out_specs=[act, st],
        out_shape=[jax.ShapeDtypeStruct((nb, RWKV_W), F32), jax.ShapeDtypeStruct(s0.shape, F32)],
        compiler_params=_cparams("arbitrary"),
        name="rwkv_step",
    )(*prep, row(lp['rk']), row(lp['ln_w']), row(lp['ln_b']), s0)


def _fox_decode_body(pt_ref, q_ref, kn_ref, vn_ref, bias_ref, *refs, npg):
    del pt_ref
    k_refs, v_refs = refs[:npg], refs[npg:2 * npg]
    o_ref, s_sc, m_sc, l_sc, acc_sc = refs[2 * npg:]
    phase = pl.program_id(1)
    j = pl.program_id(2)
    last = pl.num_programs(2) - 1
    scale = FOX_HD ** -0.5
    rnd = lambda x: x.astype(BF16).astype(F32)

    @pl.when(jnp.logical_and(phase == 0, j == 0))
    def _():
        m_sc[...] = jnp.sum(rnd(q_ref[0]) * rnd(kn_ref[0]), axis=-1, keepdims=True) * scale
        l_sc[...] = jnp.ones(l_sc.shape, F32)

    @pl.when(phase == 0)
    def _():
        qb = q_ref[0].astype(BF16)
        for g in range(npg):
            rows = k_refs[g].shape[2] * FOX_HEADS
            kf = k_refs[g][0, 0].reshape(rows, FOX_HD).astype(BF16)
            s = lax.dot_general(qb, kf, _NT, preferred_element_type=F32) * scale + bias_ref[0, g]
            s_sc[j * npg + g] = s
            m_old = m_sc[...]
            m_new = jnp.maximum(m_old, jnp.max(s, axis=-1, keepdims=True))
            l_sc[...] = jnp.exp(m_old - m_new) * l_sc[...] + jnp.sum(jnp.exp(s - m_new), axis=-1, keepdims=True)
            m_sc[...] = m_new

    @pl.when(phase == 1)
    def _():
        @pl.when(j == 0)
        def _():
            self_score = jnp.sum(rnd(q_ref[0]) * rnd(kn_ref[0]), axis=-1, keepdims=True) * scale
            acc_sc[...] = rnd(jnp.exp(self_score - m_sc[...]) / l_sc[...]) * rnd(vn_ref[0])

        for g in range(npg):
            rows = v_refs[g].shape[2] * FOX_HEADS
            vf = v_refs[g][0, 0].reshape(rows, FOX_HD).astype(BF16)
            p = jnp.exp(s_sc[j * npg + g] - m_sc[...]) / l_sc[...]
            acc_sc[...] += jnp.dot(p.astype(BF16), vf, preferred_element_type=F32)

        @pl.when(j == last)
        def _():
            o_ref[0] = acc_sc[...]


def _fox_decode(q, k_new, v_new, logf_new, cache_k, cache_v, cache_logf, page_table, layer):
    nb, n_pages = page_table.shape
    page = cache_k.shape[2]
    npg = _pick(n_pages, (16, 8, 4, 2, 1))
    plogf = cache_logf[layer][page_table].astype(F32)
    totals = jnp.sum(plogf, axis=2)
    later = lax.cumsum(totals, axis=1, reverse=True) - totals
    dsuf = lax.cumsum(plogf, axis=2, reverse=True) - plogf + later[:, :, None, :]
    bias = dsuf + logf_new[:, None, None, :]
    own = jnp.eye(FOX_HEADS, dtype=bool)[None, None, :, None, :]
    bias = jnp.where(own, bias[:, :, None, :, :], NEG_BIG).reshape(nb, n_pages, FOX_HEADS, page * FOX_HEADS)
    nst = n_pages // npg
    tok = pl.BlockSpec((1, FOX_HEADS, FOX_HD), lambda b, ph, j, pt: (b, 0, 0))
    k_step = lambda ph, j: j * (1 - ph) + (nst - 1) * ph
    v_step = lambda ph, j: j * ph
    page_spec = lambda step, g: pl.BlockSpec(
        (1, 1, page, FOX_HEADS, FOX_HD), lambda b, ph, j, pt: (layer, pt[b, step(ph, j) * npg + g], 0, 0, 0))
    grid_spec = pltpu.PrefetchScalarGridSpec(
        num_scalar_prefetch=1,
        grid=(nb, 2, nst),
        in_specs=[tok, tok, tok,
                  pl.BlockSpec((1, npg, FOX_HEADS, page * FOX_HEADS), lambda b, ph, j, pt: (b, k_step(ph, j), 0, 0))]
                 + [page_spec(k_step, g) for g in range(npg)] + [page_spec(v_step, g) for g in range(npg)],
        out_specs=tok,
        scratch_shapes=[pltpu.VMEM((n_pages, FOX_HEADS, page * FOX_HEADS), F32), pltpu.VMEM((FOX_HEADS, 1), F32),
                        pltpu.VMEM((FOX_HEADS, 1), F32), pltpu.VMEM((FOX_HEADS, FOX_HD), F32)],
    )
    r3 = lambda x: x.reshape(nb, FOX_HEADS, FOX_HD)
    o = pl.pallas_call(
        functools.partial(_fox_decode_body, npg=npg),
        grid_spec=grid_spec,
        out_shape=jax.ShapeDtypeStruct((nb, FOX_HEADS, FOX_HD), F32),
        compiler_params=_cparams("parallel", "arbitrary", "arbitrary"),
        name="fox_decode",
    )(page_table, r3(q), r3(k_new), r3(v_new), bias, *([cache_k] * npg), *([cache_v] * npg))
    return o.reshape(nb, FOX_W)


def _moe_sample_body(e_ref, h_ref, w1_ref, w3_ref, w2_ref, wv_ref, o_ref):
    del e_ref
    m = o_ref.shape[0]

    @pl.when(pl.program_id(0) == 0)
    def _():
        o_ref[...] = jnp.zeros(o_ref.shape, F32)

    h2 = h_ref[...]
    a = jnp.dot(h2, w1_ref[0], preferred_element_type=F32)
    b = jnp.dot(h2, w3_ref[0], preferred_element_type=F32)
    y = jnp.dot((_silu(a) * b).astype(BF16), w2_ref[0], preferred_element_type=F32)
    o_ref[...] += wv_ref[0][:, 0:1] * y[:m]


def _moe_sample(h, eid, gate, w1, w3, w2):
    m, d = h.shape
    na = m * TOP_K
    order = jnp.argsort(eid.reshape(-1))
    e_sorted = eid.reshape(-1)[order].astype(jnp.int32)
    wv = jnp.zeros((na, m), F32).at[jnp.arange(na), order // TOP_K].set(gate.reshape(-1)[order])
    wv = jnp.broadcast_to(wv[:, :, None], (na, m, 128))
    ff = w1.shape[2]
    grid_spec = pltpu.PrefetchScalarGridSpec(
        num_scalar_prefetch=1,
        grid=(na,),
        in_specs=[pl.BlockSpec((2 * m, d), lambda s, e: (0, 0)),
                  pl.BlockSpec((1, d, ff), lambda s, e: (e[s], 0, 0)),
                  pl.BlockSpec((1, d, ff), lambda s, e: (e[s], 0, 0)),
                  pl.BlockSpec((1, ff, d), lambda s, e: (e[s], 0, 0)),
                  pl.BlockSpec((1, m, 128), lambda s, e: (s, 0, 0))],
        out_specs=pl.BlockSpec((m, d), lambda s, e: (0, 0)),
    )
    return pl.pallas_call(
        _moe_sample_body,
        grid_spec=grid_spec,
        out_shape=jax.ShapeDtypeStruct((m, d), F32),
        compiler_params=_cparams("arbitrary"),
        name="moe_sample",
    )(e_sorted, _rows16(h), w1, w3, w2, wv)


def _sample_layer(x, lp, moe, layer, cache_k, cache_v, cache_logf, page_table, s_ret, s_rwkv, s_shift, pos):
    nb = x.shape[0]
    proj = _mm_sample(_rmsnorm(x, lp['norm_mix'], F32), lp['w_in'])
    ret_o, ret_s = _retention_sample(proj, s_ret, pos)
    logf = jax.nn.log_sigmoid(proj[:, COL_FF:COL_FF + FOX_HEADS] + lp['fox_b'].astype(F32))
    fk, fv = proj[:, COL_FK:COL_FK + FOX_W], proj[:, COL_FV:COL_FV + FOX_W]
    fox_o = _fox_decode(proj[:, COL_FQ:COL_FQ + FOX_W], fk, fv, logf, cache_k, cache_v, cache_logf, page_table, layer)
    rw_o, rw_s = _rwkv_sample(proj, lp, s_rwkv, s_shift)
    g2 = jnp.concatenate([proj[:, COL_GATE:]] * 2, axis=0)
    m2 = _merge(_rows16(ret_o), _rows16(fox_o), _rows16(rw_o), lp['wb_ret'], lp['wb_fox'], lp['wb_rwkv'], g2, 0, F32)
    x = x + _mm_sample(m2[:nb], lp['w_out'])
    h2, logits = _rmsnorm_router(x, lp['norm_ffn'], moe['wr'])
    eid, gate = _route(logits, moe['bg'], moe['be'])
    x = x + _moe_sample(h2, eid, gate, moe['w1'], moe['w3'], moe['w2'])
    heads = lambda a: a.reshape(nb, 1, FOX_HEADS, FOX_HD)
    state = (heads(fk), heads(fv), logf.reshape(nb, 1, FOX_HEADS), ret_s, rw_s, proj[:, COL_RW:COL_RW + RWKV_PROJ])
    return x, state


def kernel(x_prompt, x_sample, cache_k, cache_v, cache_logf, page_table, state_ret, state_rwkv, state_shift,
           meta_tokens, norm_mix, norm_ffn, norm_final, w_in, fox_forget_bias,
           rwkv_mu, rwkv_w0, rwkv_w2, rwkv_a0, rwkv_a2, rwkv_g2, rwkv_kk, rwkv_ka, rwkv_rk, rwkv_ln_w, rwkv_ln_b,
           w_branch_ret, w_branch_fox, w_branch_rwkv, w_out,
           router_group_w, router_group_b, router_expert_w, router_expert_b, expert_w1, expert_w3, expert_w2):
    batch, s_len, d = x_prompt.shape
    nb, n_new, _ = x_sample.shape
    assert n_new == 1 and d == D_MODEL and s_len % CHUNK == 0
    depth = w_in.shape[0]
    tp = PAD_FRONT + N_META + s_len
    past_len = page_table.shape[1] * cache_k.shape[2]
    valid = np.tile(np.arange(tp) >= PAD_FRONT, batch)
    xp = jnp.concatenate([jnp.zeros((batch, PAD_FRONT, d), F32),
                          jnp.broadcast_to(meta_tokens[None].astype(F32), (batch, N_META, d)), x_prompt], axis=1)
    xp = xp.reshape(batch * tp, d)
    xs = x_sample.reshape(nb, d)
    pos_s = jnp.full((1,), past_len, jnp.int32)
    outs_p = [[] for _ in range(6)]
    outs_s = [[] for _ in range(6)]
    for l in range(depth):
        lp = dict(norm_mix=norm_mix[l], norm_ffn=norm_ffn[l], w_in=_pack_w_in(w_in[l]), fox_b=fox_forget_bias[l],
                  mu=rwkv_mu[l], w0=rwkv_w0[l], w2=rwkv_w2[l], a0=rwkv_a0[l], a2=rwkv_a2[l], g2=rwkv_g2[l],
                  kk=rwkv_kk[l], ka=rwkv_ka[l], rk=rwkv_rk[l], ln_w=rwkv_ln_w[l], ln_b=rwkv_ln_b[l],
                  wb_ret=w_branch_ret[l].astype(BF16), wb_fox=w_branch_fox[l].astype(BF16),
                  wb_rwkv=w_branch_rwkv[l].astype(BF16), w_out=w_out[l].astype(BF16))
        moe = dict(wr=_router_weights(router_group_w[l], router_expert_w[l]), bg=router_group_b[l], be=router_expert_b[l],
                   w1=_expert_weights_bf16(expert_w1, l), w3=_expert_weights_bf16(expert_w3, l),
                   w2=_expert_weights_bf16(expert_w2, l))
        xp, st = _prompt_layer(xp, lp, moe, batch, tp, valid)
        for j in range(6):
            outs_p[j].append(st[j])
        xs, st = _sample_layer(xs, lp, moe, l, cache_k, cache_v, cache_logf, page_table,
                               state_ret[l], state_rwkv[l], state_shift[l], pos_s)
        for j in range(6):
            outs_s[j].append(st[j])
    y_prompt = _final_norm_prompt(xp, norm_final, batch, tp)
    y_sample = _rmsnorm(xs, norm_final, F32).reshape(nb, 1, d)
    return (y_prompt, y_sample, *[jnp.stack(o, axis=0) for o in outs_p], *[jnp.stack(o, axis=0) for o in outs_s])
```
